```python
import jax, jax.numpy as jnp
from jax import lax
import numpy as np

D_MODEL = 1024
BATCH = 4
SEQ = 8192
DEPTH = 4

HEAD_DIM = 128
A_GROUPS = 4
A_GROUP_DIM = 128
A_WIDTH = A_GROUPS * A_GROUP_DIM
CHUNK = 128
NSA_HEADS = 8
NSA_KV_GROUPS = 2
NSA_WIDTH = NSA_HEADS * HEAD_DIM
NSA_KV_WIDTH = 3 * 2 * NSA_KV_GROUPS * HEAD_DIM
CMP_BLOCK = 32
CMP_STRIDE = 16
SLC_BLOCK = 64
N_SELECT = 16
WINDOW = 512
Q_BLOCK = 32
MEM_TOKENS = 256
MEM_HEADS = 4
MEM_WIDTH = MEM_HEADS * HEAD_DIM
ROPE_DIM = HEAD_DIM // 4
ROPE_THETA = 500000.0
EPS = 1e-6
NEG_INF = -1e30
FORCED_SCORE = 1e4
IN_SIZES = (A_WIDTH, A_WIDTH, A_WIDTH,
            NSA_WIDTH, NSA_WIDTH, NSA_HEADS * 3, NSA_KV_WIDTH,
            MEM_WIDTH, MEM_WIDTH,
            3 * D_MODEL)
D_IN = sum(IN_SIZES)

kernel_name = "hybrid_gmlp_nsa_memory_block"


def rmsnorm(x, gain):
    x32 = x.astype(jnp.float32)
    y = x32 * lax.rsqrt(jnp.mean(x32 * x32, axis=-1, keepdims=True) + EPS)
    return (y * gain.astype(jnp.float32)).astype(x.dtype)


def layernorm(x, gain, bias):
    x32 = x.astype(jnp.float32)
    xc = x32 - jnp.mean(x32, axis=-1, keepdims=True)
    y = xc * lax.rsqrt(jnp.mean(xc * xc, axis=-1, keepdims=True) + EPS)
    return (y * gain.astype(jnp.float32) + bias.astype(jnp.float32)).astype(x.dtype)


def rope_tables(positions):
    inv_freq = ROPE_THETA ** (-jnp.arange(0, ROPE_DIM, 2, dtype=jnp.float32) / ROPE_DIM)
    ang = positions.astype(jnp.float32)[..., None] * inv_freq
    return jnp.cos(ang)[:, :, None, :], jnp.sin(ang)[:, :, None, :]


def apply_partial_rope(t, cos, sin):
    half = ROPE_DIM // 2
    t1 = t[..., :half].astype(jnp.float32)
    t2 = t[..., half:ROPE_DIM].astype(jnp.float32)
    rot = jnp.concatenate([t1 * cos - t2 * sin, t2 * cos + t1 * sin], axis=-1).astype(t.dtype)
    return jnp.concatenate([rot, t[..., ROPE_DIM:]], axis=-1)


def chunk_spatial_gating(u, v, ln_g, ln_b, w_s, b_s):
    B, S, _ = v.shape
    vn = layernorm(v, ln_g, ln_b).reshape(B, S // CHUNK, CHUNK, A_GROUPS, A_GROUP_DIM)
    ws = w_s * jnp.tril(jnp.ones((CHUNK, CHUNK), w_s.dtype))
    mixed = jnp.einsum('gts,bnsgc->bntgc', ws, vn) + b_s.T[:, :, None]
    return u * mixed.reshape(B, S, A_WIDTH)


def compress_blocks(t, pe, w1, w2):
    B, S, G, Dh = t.shape
    tb = t.reshape(B, S // CMP_STRIDE, CMP_STRIDE, G, Dh)
    blocks = jnp.concatenate([tb[:, :-1], tb[:, 1:]], axis=2) + pe[:, None, :]
    h = jax.nn.silu(jnp.einsum('bnlgd,ldh->bngh', blocks, w1))
    return jnp.einsum('bngh,hd->bngd', h, w2)


def cmp_to_slc_matrix(n_cmp, n_slc):
    i = np.arange(n_cmp)[:, None] * CMP_STRIDE
    j = np.arange(n_slc)[None, :] * SLC_BLOCK
    ov = np.clip(np.minimum(i + CMP_BLOCK, j + SLC_BLOCK) - np.maximum(i, j), 0, None)
    return jnp.asarray(ov / CMP_BLOCK, dtype=jnp.float32)


def nsa_attention(q, k_cmp, v_cmp, k_slc, v_slc, k_win, v_win, gates):
    B, S, H, Dh = q.shape
    G = NSA_KV_GROUPS
    R = H // G
    n_cmp = k_cmp.shape[1]
    n_slc = S // SLC_BLOCK
    n_sel = min(N_SELECT, n_slc)
    nb = S // Q_BLOCK
    scale = Dh ** -0.5
    cmp_end = jnp.arange(n_cmp) * CMP_STRIDE + (CMP_BLOCK - 1)
    m_cs = cmp_to_slc_matrix(n_cmp, n_slc)
    ks_blk = k_slc.reshape(B, n_slc, SLC_BLOCK, G, Dh).transpose(0, 3, 1, 2, 4)
    vs_blk = v_slc.reshape(B, n_slc, SLC_BLOCK, G, Dh).transpose(0, 3, 1, 2, 4)
    kw_pad = jnp.pad(k_win, ((0, 0), (WINDOW, 0), (0, 0), (0, 0)))
    vw_pad = jnp.pad(v_win, ((0, 0), (WINDOW, 0), (0, 0), (0, 0)))
    b_ix = jnp.arange(B)[:, None, None, None]
    g_ix = jnp.arange(G)[None, None, :, None]
    slc_off = jnp.arange(SLC_BLOCK)
    blk_ids = jnp.arange(n_slc)

    def one_block(args):
        qb, gb, blk = args
        s0 = blk * Q_BLOCK
        t = s0 + jnp.arange(Q_BLOCK)
        sc = jnp.einsum('btgrd,bngd->bgrtn', qb, k_cmp).astype(jnp.float32) * scale
        cmask = cmp_end[None, :] <= t[:, None]
        pc = jax.nn.softmax(jnp.where(cmask, sc, NEG_INF), axis=-1) * cmask
        o_cmp = jnp.einsum('bgrtn,bngd->btgrd', pc.astype(qb.dtype), v_cmp)
        imp = jnp.einsum('bgrtn,nj->btgj', pc, m_cs)
        cur = (t // SLC_BLOCK)[None, :, None, None]
        forced = (blk_ids == 0) | (blk_ids == cur) | (blk_ids == cur - 1)
        imp = jnp.where(forced, FORCED_SCORE, jnp.where(blk_ids <= cur, imp, -1.0))
        _, idx = lax.top_k(imp, n_sel)
        kg = ks_blk[b_ix, g_ix, idx]
        vg = vs_blk[b_ix, g_ix, idx]
        ss = jnp.einsum('btgrd,btgkld->btgrkl', qb, kg).astype(jnp.float32) * scale
        tok = idx[..., None] * SLC_BLOCK + slc_off
        smask = (tok <= t[None, :, None, None, None])[:, :, :, None]
        ss = jnp.where(smask, ss, NEG_INF)
        ps = jax.nn.softmax(ss.reshape(B, Q_BLOCK, G, R, -1), axis=-1).reshape(ss.shape)
        o_slc = jnp.einsum('btgrkl,btgkld->btgrd', ps.astype(qb.dtype), vg)
        kw = lax.dynamic_slice_in_dim(kw_pad, s0, Q_BLOCK + WINDOW, axis=1)
        vw = lax.dynamic_slice_in_dim(vw_pad, s0, Q_BLOCK + WINDOW, axis=1)
        kpos = s0 - WINDOW + jnp.arange(Q_BLOCK + WINDOW)
        rel = t[:, None] - kpos[None, :]
        wmask = (rel >= 0) & (rel < WINDOW) & (kpos[None, :] >= 0)
        sw = jnp.einsum('btgrd,bsgd->bgrts', qb, kw).astype(jnp.float32) * scale
        pw = jax.nn.softmax(jnp.where(wmask, sw, NEG_INF), axis=-1)
        o_win = jnp.einsum('bgrts,bsgd->btgrd', pw.astype(qb.dtype), vw)
        gb = gb.reshape(B, Q_BLOCK, G, R, 3)
        return gb[..., 0:1] * o_cmp + gb[..., 1:2] * o_slc + gb[..., 2:3] * o_win

    q_blocks = q.reshape(B, nb, Q_BLOCK, G, R, Dh).transpose(1, 0, 2, 3, 4, 5)
    g_blocks = gates.reshape(B, nb, Q_BLOCK, H, 3).transpose(1, 0, 2, 3, 4)
    out = lax.map(one_block, (q_blocks, g_blocks, jnp.arange(nb)))
    return out.transpose(1, 0, 2, 3, 4, 5).reshape(B, S, H * Dh)


def memory_attention(q, k, v):
    B, S, H, Dh = q.shape
    s = jnp.einsum('bshd,bmhd->bhsm', q, k).astype(jnp.float32) * (Dh ** -0.5)
    p = jax.nn.softmax(s, axis=-1)
    return jnp.einsum('bhsm,bmhd->bshd', p.astype(q.dtype), v).reshape(B, S, H * Dh)


def setup_inputs(seed: int = 0) -> dict:
    key = jax.random.key(seed)
    ks = jax.random.split(key, 24)
    f32 = jnp.float32

    def nrm(k, shape, scale):
        return jax.random.normal(k, shape, f32) * scale

    x = nrm(ks[0], (BATCH, SEQ, D_MODEL), 1.0)
    mem = nrm(ks[1], (BATCH, MEM_TOKENS, D_MODEL), 1.0)
    offsets = jax.random.randint(ks[2], (BATCH, 1), 0, 4096, dtype=jnp.int32)
    positions = offsets + jnp.arange(SEQ, dtype=jnp.int32)[None, :]
    return {
        "x": x,
        "mem": mem,
        "positions": positions,
        "norm_gain": 1.0 + nrm(ks[3], (DEPTH, D_MODEL), 0.05),
        "w_in": nrm(ks[4], (DEPTH, D_MODEL, D_IN), D_MODEL ** -0.5),
        "ln_v_gain": 1.0 + nrm(ks[5], (DEPTH, A_WIDTH), 0.05),
        "ln_v_bias": nrm(ks[6], (DEPTH, A_WIDTH), 0.02),
        "w_spatial": nrm(ks[7], (DEPTH, A_GROUPS, CHUNK, CHUNK), 0.5 * CHUNK ** -0.5),
        "b_spatial": 1.0 + nrm(ks[8], (DEPTH, A_GROUPS, CHUNK), 0.1),
        "cmp_pe_k": nrm(ks[9], (DEPTH, CMP_BLOCK, HEAD_DIM), 0.1),
        "cmp_w1_k": nrm(ks[10], (DEPTH, CMP_BLOCK, HEAD_DIM, HEAD_DIM), (CMP_BLOCK * HEAD_DIM) ** -0.5),
        "cmp_w2_k": nrm(ks[11], (DEPTH, HEAD_DIM, HEAD_DIM), HEAD_DIM ** -0.5),
        "cmp_pe_v": nrm(ks[12], (DEPTH, CMP_BLOCK, HEAD_DIM), 0.1),
        "cmp_w1_v": nrm(ks[13], (DEPTH, CMP_BLOCK, HEAD_DIM, HEAD_DIM), (CMP_BLOCK * HEAD_DIM) ** -0.5),
        "cmp_w2_v": nrm(ks[14], (DEPTH, HEAD_DIM, HEAD_DIM), HEAD_DIM ** -0.5),
        "mem_norm_gain": 1.0 + nrm(ks[15], (D_MODEL,), 0.05),
        "w_mem_kv": nrm(ks[16], (DEPTH, D_MODEL, 2 * MEM_WIDTH), D_MODEL ** -0.5),
        "w_branch_a": nrm(ks[17], (DEPTH, A_WIDTH, D_MODEL), A_WIDTH ** -0.5),
        "w_branch_b": nrm(ks[18], (DEPTH, NSA_WIDTH, D_MODEL), NSA_WIDTH ** -0.5),
        "w_branch_c": nrm(ks[19], (DEPTH, MEM_WIDTH, D_MODEL), MEM_WIDTH ** -0.5),
        "w_out": nrm(ks[20], (DEPTH, D_MODEL, D_MODEL), D_MODEL ** -0.5),
        "final_norm_gain": 1.0 + nrm(ks[21], (D_MODEL,), 0.05),
    }


def reference(x, mem, positions, norm_gain, w_in, ln_v_gain, ln_v_bias, w_spatial, b_spatial,
              cmp_pe_k, cmp_w1_k, cmp_w2_k, cmp_pe_v, cmp_w1_v, cmp_w2_v, mem_norm_gain, w_mem_kv,
              w_branch_a, w_branch_b, w_branch_c, w_out, final_norm_gain):
    B, S, D = x.shape
    G = NSA_KV_GROUPS
    splits = [int(s) for s in np.cumsum(IN_SIZES)[:-1]]
    cos, sin = rope_tables(positions)
    mem_n = rmsnorm(mem, mem_norm_gain)
    for l in range(DEPTH):
        h = rmsnorm(x, norm_gain[l])
        proj = h @ w_in[l]
        a_u, a_v, a_z, b_q, b_z, b_g, b_kv, c_q, c_z, merge = jnp.split(proj, splits, axis=-1)
        o_a = chunk_spatial_gating(a_u, a_v, ln_v_gain[l], ln_v_bias[l], w_spatial[l], b_spatial[l])
        o_a = o_a * jax.nn.silu(a_z)
        q = apply_partial_rope(b_q.reshape(B, S, NSA_HEADS, HEAD_DIM), cos, sin)
        kv = b_kv.reshape(B, S, 3, 2, G, HEAD_DIM)
        k_cmp = compress_blocks(apply_partial_rope(kv[:, :, 0, 0], cos, sin), cmp_pe_k[l], cmp_w1_k[l], cmp_w2_k[l])
        v_cmp = compress_blocks(kv[:, :, 0, 1], cmp_pe_v[l], cmp_w1_v[l], cmp_w2_v[l])
        k_slc = apply_partial_rope(kv[:, :, 1, 0], cos, sin)
        k_win = apply_partial_rope(kv[:, :, 2, 0], cos, sin)
        gates = jax.nn.sigmoid(b_g).reshape(B, S, NSA_HEADS, 3)
        o_b = nsa_attention(q, k_cmp, v_cmp, k_slc, kv[:, :, 1, 1], k_win, kv[:, :, 2, 1], gates)
        o_b = o_b * jax.nn.silu(b_z)
        mkv = (mem_n @ w_mem_kv[l]).reshape(B, MEM_TOKENS, 2, MEM_HEADS, HEAD_DIM)
        o_c = memory_attention(c_q.reshape(B, S, MEM_HEADS, HEAD_DIM), mkv[:, :, 0], mkv[:, :, 1])
        o_c = o_c * jax.nn.silu(c_z)
        g = jax.nn.sigmoid(merge).reshape(B, S, 3, D)
        mixed = (g[:, :, 0] * (o_a @ w_branch_a[l]) + g[:, :, 1] * (o_b @ w_branch_b[l])
                 + g[:, :, 2] * (o_c @ w_branch_c[l]))
        x = x + mixed @ w_out[l]
    return rmsnorm(x, final_norm_gain)
```

```python
import functools

import numpy as np
import jax
import jax.numpy as jnp
from jax import lax
from jax.experimental import pallas as pl
from jax.experimental.pallas import tpu as pltpu

F32 = jnp.float32
BF16 = jnp.bfloat16

HEAD_DIM = 128
A_GROUPS = 4
A_WIDTH = 512
CHUNK = 128
NSA_HEADS = 8
NSA_KV_GROUPS = 2
HEADS_PER_GROUP = NSA_HEADS // NSA_KV_GROUPS
NSA_WIDTH = NSA_HEADS * HEAD_DIM
CMP_BLOCK = 32
CMP_STRIDE = 16
SLC_BLOCK = 64
N_SELECT = 16
WINDOW = 512
MEM_HEADS = 4
MEM_WIDTH = MEM_HEADS * HEAD_DIM
ROPE_DIM = HEAD_DIM // 4
ROPE_THETA = 500000.0
EPS = 1e-6
NEG_INF = -1e30
FORCED_SCORE = 1e4
ATT_SCALE = HEAD_DIM ** -0.5

_OFF = np.cumsum((0, A_WIDTH, A_WIDTH, A_WIDTH, NSA_WIDTH, NSA_WIDTH, NSA_HEADS * 3, 3 * 2 * NSA_KV_GROUPS * HEAD_DIM,
                  MEM_WIDTH, MEM_WIDTH))
(_A_U, _A_V, _A_Z, _B_Q, _B_Z, _B_G, _B_KV, _C_Q, _C_Z, _MERGE) = (int(v) for v in _OFF)
GATE_PAD = 128

PROJ_TM = 512
NSA_TQ = 128
NSA_TK = 512
FUSE_TM = 256
VMEM_LIMIT = 56 * 1024 * 1024


def _nt_dot(a, b):
    return lax.dot_general(a, b, (((1,), (1,)), ((), ())), preferred_element_type=F32)


def _dot(a, b):
    return jnp.dot(a, b, preferred_element_type=F32)


def _rmsnorm(x, gain):
    return x * lax.rsqrt(jnp.mean(x * x, axis=-1, keepdims=True) + EPS) * gain


def _silu(x):
    return x * jax.nn.sigmoid(x)


def _resident(block_shape, index_map):
    return pl.BlockSpec(block_shape, index_map, pipeline_mode=pl.Buffered(1))


def _proj_kernel(x_ref, gain_ref, w_ref, cos_ref, sina_ref, sinb_ref,
                 q_ref, gate_ref, cmp_ref, ksel_ref, vsel_ref, kwin_ref, vwin_ref, *, n_slc):
    tm = x_ref.shape[1]
    t0 = pl.program_id(1) * tm
    h = _rmsnorm(x_ref[0], gain_ref[...]).astype(BF16)
    proj = _dot(h, w_ref[...])
    cosf, sina, sinb = cos_ref[0], sina_ref[0], sinb_ref[0]

    def rope(t):
        return t * cosf + pltpu.roll(t, HEAD_DIM - ROPE_DIM // 2, 1) * sina + pltpu.roll(t, ROPE_DIM // 2, 1) * sinb

    def col(j):
        return proj[:, j * HEAD_DIM:(j + 1) * HEAD_DIM]

    for hd in range(NSA_HEADS):
        q_ref[0, :, hd * HEAD_DIM:(hd + 1) * HEAD_DIM] = (rope(col(hd)) * ATT_SCALE).astype(BF16)
    kv0 = NSA_HEADS
    g2 = NSA_KV_GROUPS
    row = t0 + lax.broadcasted_iota(jnp.int32, (tm, n_slc), 0)
    blk = lax.broadcasted_iota(jnp.int32, (tm, n_slc), 1)
    sel_onehot = (row // SLC_BLOCK == blk).astype(BF16)
    ones_col = (lax.broadcasted_iota(jnp.int32, (tm, HEAD_DIM), 1) == 0).astype(BF16)
    for g in range(g2):
        cmp_ref[0, g] = rope(col(kv0 + g))
        cmp_ref[0, g2 + g] = col(kv0 + g2 + g)
        ksel_ref[0, g, :, 0:HEAD_DIM] = rope(col(kv0 + 2 * g2 + g)).astype(BF16)
        ksel_ref[0, g, :, HEAD_DIM:] = sel_onehot
        vsel_ref[0, g, :, 0:HEAD_DIM] = col(kv0 + 3 * g2 + g).astype(BF16)
        vsel_ref[0, g, :, HEAD_DIM:] = ones_col
        kwin_ref[0, g] = rope(col(kv0 + 4 * g2 + g)).astype(BF16)
        vwin_ref[0, g, :, 0:HEAD_DIM] = col(kv0 + 5 * g2 + g).astype(BF16)
        vwin_ref[0, g, :, HEAD_DIM:] = ones_col
    gate_ref[0] = jax.nn.sigmoid(proj[:, (kv0 + 6 * g2) * HEAD_DIM:])


def _proj_call(x, gain, w_p, cosf, sina, sinb):
    b, s, d = x.shape
    tm = PROJ_TM
    n_slc = s // SLC_BLOCK
    g2 = NSA_KV_GROUPS
    wp_cols = w_p.shape[1]
    tok = lambda bi, i: (bi, i, 0)
    grp = lambda bi, i: (bi, 0, i, 0)
    return pl.pallas_call(
        functools.partial(_proj_kernel, n_slc=n_slc),
        grid=(b, s // tm),
        in_specs=[
            pl.BlockSpec((1, tm, d), tok),
            pl.BlockSpec((1, d), lambda bi, i: (0, 0)),
            _resident((d, wp_cols), lambda bi, i: (0, 0)),
            pl.BlockSpec((1, tm, HEAD_DIM), tok),
            pl.BlockSpec((1, tm, HEAD_DIM), tok),
            pl.BlockSpec((1, tm, HEAD_DIM), tok),
        ],
        out_specs=[
            pl.BlockSpec((1, tm, NSA_WIDTH), tok),
            pl.BlockSpec((1, tm, GATE_PAD), tok),
            pl.BlockSpec((1, 2 * g2, tm, HEAD_DIM), grp),
            pl.BlockSpec((1, g2, tm, HEAD_DIM + n_slc), grp),
            pl.BlockSpec((1, g2, tm, 2 * HEAD_DIM), grp),
            pl.BlockSpec((1, g2, tm, HEAD_DIM), grp),
            pl.BlockSpec((1, g2, tm, 2 * HEAD_DIM), grp),
        ],
        out_shape=[
            jax.ShapeDtypeStruct((b, s, NSA_WIDTH), BF16),
            jax.ShapeDtypeStruct((b, s, GATE_PAD), F32),
            jax.ShapeDtypeStruct((b, 2 * g2, s, HEAD_DIM), F32),
            jax.ShapeDtypeStruct((b, g2, s, HEAD_DIM + n_slc), BF16),
            jax.ShapeDtypeStruct((b, g2, s, 2 * HEAD_DIM), BF16),
            jax.ShapeDtypeStruct((b, g2, s, HEAD_DIM), BF16),
            jax.ShapeDtypeStruct((b, g2, s, 2 * HEAD_DIM), BF16),
        ],
        compiler_params=pltpu.CompilerParams(
            dimension_semantics=("arbitrary", "arbitrary"), vmem_limit_bytes=VMEM_LIMIT),
        name="nsa_proj",
    )(x, gain, w_p, cosf, sina, sinb)


def _compress_kernel(x_ref, pe_ref, w1_ref, w2_ref, o_ref):
    n_chunk = o_ref.shape[2]
    first = jnp.zeros((n_chunk, HEAD_DIM), F32)
    second = jnp.zeros((n_chunk, HEAD_DIM), F32)
    for c in range(CMP_STRIDE):
        rows = x_ref[0, 0, pl.ds(c, n_chunk, stride=CMP_STRIDE), :]
        first = first + _dot((rows + pe_ref[0, c:c + 1, :]).astype(BF16), w1_ref[0, c])
        second = second + _dot((rows + pe_ref[0, CMP_STRIDE + c:CMP_STRIDE + c + 1, :]).astype(BF16),
                               w1_ref[0, CMP_STRIDE + c])
    hidden = first + pltpu.roll(second, n_chunk - 1, 0)
    o_ref[0, 0] = _dot(_silu(hidden).astype(BF16), w2_ref[0]).astype(BF16)


def _compress_call(cmp_in, pe, w1, w2):
    b, n4, s, dh = cmp_in.shape
    g2 = NSA_KV_GROUPS
    n_chunk = s // CMP_STRIDE
    return pl.pallas_call(
        _compress_kernel,
        grid=(b, n4),
        in_specs=[
            pl.BlockSpec((1, 1, s, dh), lambda bi, j: (bi, j, 0, 0)),
            pl.BlockSpec((1, CMP_BLOCK, dh), lambda bi, j: (j // g2, 0, 0)),
            pl.BlockSpec((1, CMP_BLOCK, dh, dh), lambda bi, j: (j // g2, 0, 0, 0)),
            pl.BlockSpec((1, dh, dh), lambda bi, j: (j // g2, 0, 0)),
        ],
        out_specs=pl.BlockSpec((1, 1, n_chunk, dh), lambda bi, j: (bi, j, 0, 0)),
        out_shape=jax.ShapeDtypeStruct((b, n4, n_chunk, dh), BF16),
        compiler_params=pltpu.CompilerParams(
            dimension_semantics=("arbitrary", "arbitrary"), vmem_limit_bytes=VMEM_LIMIT),
        name="nsa_compress",
    )(cmp_in, pe, w1, w2)


def _nsa_kernel(q_ref, gate_ref, cmp_ref, ksel_ref, vsel_ref, kwin_ref, vwin_ref, mcs_ref, o_ref,
                m_ref, acc_ref, *, seq):
    tq = q_ref.shape[1]
    rows = HEADS_PER_GROUP * tq
    n_chunk = seq // CMP_STRIDE
    n_cmp = n_chunk - 1
    n_slc = seq // SLC_BLOCK
    i = pl.program_id(1)
    t0 = i * tq
    tpos = t0 + lax.broadcasted_iota(jnp.int32, (rows, 1), 0) % tq

    for g in range(NSA_KV_GROUPS):
        qg = jnp.concatenate(
            [q_ref[0, :, (HEADS_PER_GROUP * g + r) * HEAD_DIM:(HEADS_PER_GROUP * g + r + 1) * HEAD_DIM]
             for r in range(HEADS_PER_GROUP)], axis=0)

        s_c = _nt_dot(qg, cmp_ref[0, g])
        ncol = lax.broadcasted_iota(jnp.int32, (1, n_chunk), 1)
        cmask = (ncol * CMP_STRIDE + (CMP_BLOCK - 1) <= tpos) & (ncol < n_cmp)
        s_c = jnp.where(cmask, s_c, NEG_INF)
        e_c = jnp.where(cmask, jnp.exp(s_c - jnp.max(s_c, axis=1, keepdims=True)), 0.0)
        l_c = jnp.sum(e_c, axis=1, keepdims=True)
        p_c = e_c * (1.0 / jnp.where(l_c > 0.0, l_c, 1.0))
        o_cmp = _dot(p_c.astype(BF16), cmp_ref[0, NSA_KV_GROUPS + g])

        p_sum = p_c[0:tq]
        for r in range(1, HEADS_PER_GROUP):
            p_sum = p_sum + p_c[r * tq:(r + 1) * tq]
        p_hi = p_sum.astype(BF16)
        rest = p_sum - p_hi.astype(F32)
        p_mid = rest.astype(BF16)
        p_lo = (rest - p_mid.astype(F32)).astype(BF16)
        mcs = mcs_ref[...]
        imp = _dot(p_hi, mcs) + _dot(p_mid, mcs) + _dot(p_lo, mcs)

        imp_t = imp.T
        jrow = lax.broadcasted_iota(jnp.int32, (n_slc, tq), 0)
        cur = (t0 + lax.broadcasted_iota(jnp.int32, (n_slc, tq), 1)) // SLC_BLOCK
        forced = (jrow == 0) | (jrow == cur) | (jrow == cur - 1)
        cand = jnp.where(forced, FORCED_SCORE, jnp.where(jrow <= cur, imp_t, -1.0))
        sel_t = jnp.zeros((n_slc, tq), F32)
        for _ in range(min(N_SELECT, n_slc)):
            best = jnp.max(cand, axis=0, keepdims=True)
            first = jnp.min(jnp.where(cand == best, jrow, n_slc), axis=0, keepdims=True)
            hit = jrow == first
            sel_t = jnp.where(hit, 1.0, sel_t)
            cand = jnp.where(hit, -jnp.inf, cand)
        sel_bias = jnp.where(sel_t.T > 0.5, 0.0, NEG_INF).astype(BF16)

        lhs = jnp.concatenate(
            [qg, jnp.concatenate([sel_bias] * HEADS_PER_GROUP, axis=0)], axis=1)
        m_ref[...] = jnp.full((rows, 1), NEG_INF, F32)
        acc_ref[...] = jnp.zeros((rows, 2 * HEAD_DIM), F32)

        def sweep(kt, masked):
            k0 = pl.multiple_of(kt * NSA_TK, NSA_TK)
            s_s = _nt_dot(lhs, ksel_ref[0, g, pl.ds(k0, NSA_TK), :])
            if masked:
                kpos = k0 + lax.broadcasted_iota(jnp.int32, (1, NSA_TK), 1)
                s_s = jnp.where(kpos <= tpos, s_s, NEG_INF)
            m_old = m_ref[...]
            m_new = jnp.maximum(m_old, jnp.max(s_s, axis=1, keepdims=True))
            p_s = jnp.exp(s_s - m_new).astype(BF16)
            acc_ref[...] = jnp.exp(m_old - m_new) * acc_ref[...] + _dot(p_s, vsel_ref[0, g, pl.ds(k0, NSA_TK), :])
            m_ref[...] = m_new

        n_full = t0 // NSA_TK

        def full_body(kt, carry):
            sweep(kt, False)
            return carry

        lax.fori_loop(0, n_full, full_body, 0)
        sweep(n_full, True)
        acc = acc_ref[...]
        o_slc = acc[:, 0:HEAD_DIM] * (1.0 / acc[:, HEAD_DIM:HEAD_DIM + 1])

        wlen = WINDOW + tq
        w0 = pl.multiple_of(jnp.maximum(t0 - WINDOW, 0), tq)
        s_w = _nt_dot(qg, kwin_ref[0, g, pl.ds(w0, wlen), :])
        rel = tpos - (w0 + lax.broadcasted_iota(jnp.int32, (1, wlen), 1))
        s_w = jnp.where((rel >= 0) & (rel < WINDOW), s_w, NEG_INF)
        p_w = jnp.exp(s_w - jnp.max(s_w, axis=1, keepdims=True)).astype(BF16)
        acc_w = _dot(p_w, vwin_ref[0, g, pl.ds(w0, wlen), :])
        o_win = acc_w[:, 0:HEAD_DIM] * (1.0 / acc_w[:, HEAD_DIM:HEAD_DIM + 1])

        for r in range(HEADS_PER_GROUP):
            hd = HEADS_PER_GROUP * g + r
            gates = gate_ref[0, :, 3 * hd:3 * hd + 3]
            sl = slice(r * tq, (r + 1) * tq)
            o_ref[0, :, hd * HEAD_DIM:(hd + 1) * HEAD_DIM] = (
                gates[:, 0:1] * o_cmp[sl] + gates[:, 1:2] * o_slc[sl] + gates[:, 2:3] * o_win[sl])


def _nsa_call(q, gates, cmp, ksel, vsel, kwin, vwin, mcs):
    b, s, _ = q.shape
    tq = NSA_TQ
    g2 = NSA_KV_GROUPS
    n_chunk = s // CMP_STRIDE
    n_slc = s // SLC_BLOCK
    rows = HEADS_PER_GROUP * tq
    tok = lambda bi, i: (bi, i, 0)
    per_batch = lambda bi, i: (bi, 0, 0, 0)
    return pl.pallas_call(
        functools.partial(_nsa_kernel, seq=s),
        grid=(b, s // tq),
        in_specs=[
            pl.BlockSpec((1, tq, NSA_WIDTH), tok),
            pl.BlockSpec((1, tq, GATE_PAD), tok),
            _resident((1, 2 * g2, n_chunk, HEAD_DIM), per_batch),
            _resident((1, g2, s, HEAD_DIM + n_slc), per_batch),
            _resident((1, g2, s, 2 * HEAD_DIM), per_batch),
            _resident((1, g2, s, HEAD_DIM), per_batch),
            _resident((1, g2, s, 2 * HEAD_DIM), per_batch),
            _resident((n_chunk, n_slc), lambda bi, i: (0, 0)),
        ],
        out_specs=pl.BlockSpec((1, tq, NSA_WIDTH), tok),
        out_shape=jax.ShapeDtypeStruct((b, s, NSA_WIDTH), F32),
        scratch_shapes=[pltpu.VMEM((rows, 1), F32), pltpu.VMEM((rows, 2 * HEAD_DIM), F32)],
        compiler_params=pltpu.CompilerParams(
            dimension_semantics=("arbitrary", "arbitrary"), vmem_limit_bytes=VMEM_LIMIT),
        name="nsa_attention",
    )(q, gates, cmp, ksel, vsel, kwin, vwin, mcs)


def _memkv_kernel(mem_ref, gain_ref, w_ref, o_ref):
    mem_n = _rmsnorm(mem_ref[0], gain_ref[...]).astype(BF16)
    o_ref[0, 0] = _dot(mem_n, w_ref[0]).astype(BF16)


def _memkv_call(mem, gain, w_mem):
    depth = w_mem.shape[0]
    b, m, d = mem.shape
    return pl.pallas_call(
        _memkv_kernel,
        grid=(depth, b),
        in_specs=[
            pl.BlockSpec((1, m, d), lambda l, bi: (bi, 0, 0)),
            pl.BlockSpec((1, d), lambda l, bi: (0, 0)),
            pl.BlockSpec((1, d, 2 * MEM_WIDTH), lambda l, bi: (l, 0, 0)),
        ],
        out_specs=pl.BlockSpec((1, 1, m, 2 * MEM_WIDTH), lambda l, bi: (l, bi, 0, 0)),
        out_shape=jax.ShapeDtypeStruct((depth, b, m, 2 * MEM_WIDTH), BF16),
        compiler_params=pltpu.CompilerParams(
            dimension_semantics=("arbitrary", "arbitrary"), vmem_limit_bytes=VMEM_LIMIT),
        name="mem_kv",
    )(mem, gain, w_mem)


_F_AUVZ = 0
_F_BZ = 3 * A_WIDTH
_F_CQZ = _F_BZ + NSA_WIDTH
_F_MERGE = _F_CQZ + 2 * MEM_WIDTH


def _fused_kernel(x_ref, ob_ref, gain_ref, w_ref, lng_ref, lnb_ref, wsp_ref, bsp_ref, mkv_ref,
                  wa_ref, wb_ref, wc_ref, wo_ref, fgain_ref, o_ref, *, final):
    tm, d = x_ref.shape[1], x_ref.shape[2]
    x = x_ref[0]
    h = _rmsnorm(x, gain_ref[...]).astype(BF16)

    def proj(c0, width):
        return _dot(h, w_ref[:, c0:c0 + width])

    def merge_gate(k):
        return jax.nn.sigmoid(proj(_F_MERGE + k * d, d))

    uvz = proj(_F_AUVZ, 3 * A_WIDTH)
    u, v, z = uvz[:, 0:A_WIDTH], uvz[:, A_WIDTH:2 * A_WIDTH], uvz[:, 2 * A_WIDTH:]
    vc = v - jnp.mean(v, axis=-1, keepdims=True)
    vn = (vc * lax.rsqrt(jnp.mean(vc * vc, axis=-1, keepdims=True) + EPS) * lng_ref[...] + lnb_ref[...]).astype(BF16)
    gdim = A_WIDTH // A_GROUPS
    mixed = jnp.concatenate([
        jnp.concatenate([
            _dot(wsp_ref[gi], vn[c * CHUNK:(c + 1) * CHUNK, gi * gdim:(gi + 1) * gdim]) + bsp_ref[gi]
            for gi in range(A_GROUPS)], axis=1)
        for c in range(tm // CHUNK)], axis=0)
    o_a = (u * mixed * _silu(z)).astype(BF16)
    acc = merge_gate(0) * _dot(o_a, wa_ref[...])

    o_b = (ob_ref[0] * _silu(proj(_F_BZ, NSA_WIDTH))).astype(BF16)
    acc = acc + merge_gate(1) * _dot(o_b, wb_ref[...])

    cqz = proj(_F_CQZ, 2 * MEM_WIDTH)
    heads = []
    for hd in range(MEM_HEADS):
        sl = slice(hd * HEAD_DIM, (hd + 1) * HEAD_DIM)
        s_m = _nt_dot(cqz[:, sl].astype(BF16), mkv_ref[0, 0, :, sl]) * ATT_SCALE
        e_m = jnp.exp(s_m - jnp.max(s_m, axis=1, keepdims=True))
        p_m = e_m * (1.0 / jnp.sum(e_m, axis=1, keepdims=True))
        heads.append(_dot(p_m.astype(BF16), mkv_ref[0, 0, :, MEM_WIDTH + hd * HEAD_DIM:MEM_WIDTH + (hd + 1) * HEAD_DIM]))
    o_c = (jnp.concatenate(heads, axis=1) * _silu(cqz[:, MEM_WIDTH:])).astype(BF16)
    acc = acc + merge_gate(2) * _dot(o_c, wc_ref[...])

    x_new = x + _dot(acc.astype(BF16), wo_ref[...])
    if final:
        x_new = _rmsnorm(x_new, fgain_ref[...])
    o_ref[0] = x_new


def _fused_call(x, ob, gain, w_f, lng, lnb, wsp, bsp, mkv, layer, wa, wb, wc, wo, fgain, final):
    b, s, d = x.shape
    tm = FUSE_TM
    m = mkv.shape[2]
    tok = lambda bi, i: (bi, i, 0)
    c2 = lambda bi, i: (0, 0)
    c3 = lambda bi, i: (0, 0, 0)
    return pl.pallas_call(
        functools.partial(_fused_kernel, final=final),
        grid=(b, s // tm),
        in_specs=[
            pl.BlockSpec((1, tm, d), tok),
            pl.BlockSpec((1, tm, NSA_WIDTH), tok),
            pl.BlockSpec((1, d), c2),
            _resident(w_f.shape, c2),
            pl.BlockSpec((1, A_WIDTH), c2),
            pl.BlockSpec((1, A_WIDTH), c2),
            _resident(wsp.shape, c3),
            _resident(bsp.shape, c3),
            _resident((1, 1, m, 2 * MEM_WIDTH), lambda bi, i: (layer, bi, 0, 0)),
            _resident(wa.shape, c2),
            _resident(wb.shape, c2),
            _resident(wc.shape, c2),
            _resident(wo.shape, c2),
            pl.BlockSpec((1, d), c2),
        ],
        out_specs=pl.BlockSpec((1, tm, d), tok),
        out_shape=jax.ShapeDtypeStruct((b, s, d), F32),
        compiler_params=pltpu.CompilerParams(
            dimension_semantics=("arbitrary", "arbitrary"), vmem_limit_bytes=VMEM_LIMIT),
        name="fused_mix",
    )(x, ob, gain, w_f, lng, lnb, wsp, bsp, mkv, wa, wb, wc, wo, fgain)


def _rope_lane_tables(positions):
    half = ROPE_DIM // 2
    inv_freq = ROPE_THETA ** (-jnp.arange(0, ROPE_DIM, 2, dtype=F32) / ROPE_DIM)
    ang = positions.astype(F32)[..., None] * inv_freq
    cos, sin = jnp.cos(ang), jnp.sin(ang)
    zeros = jnp.zeros(cos.shape[:-1] + (HEAD_DIM - ROPE_DIM,), F32)
    cosf = jnp.concatenate([cos, cos, zeros + 1.0], axis=-1)
    sina = jnp.concatenate([-sin, jnp.zeros_like(sin), zeros], axis=-1)
    sinb = jnp.concatenate([jnp.zeros_like(sin), sin, zeros], axis=-1)
    del half
    return cosf, sina, sinb


def _cmp_to_slc(n_chunk, n_slc):
    i = np.arange(n_chunk)[:, None] * CMP_STRIDE
    j = np.arange(n_slc)[None, :] * SLC_BLOCK
    ov = np.clip(np.minimum(i + CMP_BLOCK, j + SLC_BLOCK) - np.maximum(i, j), 0, None) / CMP_BLOCK
    ov[n_chunk - 1] = 0.0
    return jnp.asarray(ov, dtype=BF16)


def kernel(x, mem, positions, norm_gain, w_in, ln_v_gain, ln_v_bias, w_spatial, b_spatial, cmp_pe_k, cmp_w1_k, cmp_w2_k, cmp_pe_v, cmp_w1_v, cmp_w2_v, mem_norm_gain, w_mem_kv, w_branch_a, w_branch_b, w_branch_c, w_out, final_norm_gain):
    depth = w_in.shape[0]
    b, s, d = x.shape
    assert s % NSA_TK == 0 and s % PROJ_TM == 0 and s >= WINDOW + NSA_TQ and d == w_in.shape[1]

    cosf, sina, sinb = _rope_lane_tables(positions)
    mcs = _cmp_to_slc(s // CMP_STRIDE, s // SLC_BLOCK)

    gate_w = jnp.pad(w_in[:, :, _B_G:_B_KV], ((0, 0), (0, 0), (0, GATE_PAD - (_B_KV - _B_G))))
    w_p = jnp.concatenate([w_in[:, :, _B_Q:_B_Z], w_in[:, :, _B_KV:_C_Q], gate_w], axis=-1).astype(BF16)
    w_f = jnp.concatenate([w_in[:, :, _A_U:_B_Q], w_in[:, :, _B_Z:_B_G], w_in[:, :, _C_Q:_MERGE],
                           w_in[:, :, _MERGE:]], axis=-1).astype(BF16)
    wsp = (w_spatial * jnp.tril(jnp.ones((CHUNK, CHUNK), w_spatial.dtype))).astype(BF16)
    bsp = jnp.broadcast_to(b_spatial[..., None], b_spatial.shape + (A_WIDTH // A_GROUPS,))
    pe = jnp.stack([cmp_pe_k, cmp_pe_v], axis=1)
    w1 = jnp.stack([cmp_w1_k, cmp_w1_v], axis=1).astype(BF16)
    w2 = jnp.stack([cmp_w2_k, cmp_w2_v], axis=1).astype(BF16)
    wa, wb, wc, wo = (w.astype(BF16) for w in (w_branch_a, w_branch_b, w_branch_c, w_out))

    mkv = _memkv_call(mem, mem_norm_gain[None, :], w_mem_kv.astype(BF16))
    fgain = final_norm_gain[None, :]
    for l in range(depth):
        gain = norm_gain[l][None, :]
        q, gates, cmp_in, ksel, vsel, kwin, vwin = _proj_call(x, gain, w_p[l], cosf, sina, sinb)
        cmp = _compress_call(cmp_in, pe[l], w1[l], w2[l])
        ob = _nsa_call(q, gates, cmp, ksel, vsel, kwin, vwin, mcs)
        x = _fused_call(x, ob, gain, w_f[l], ln_v_gain[l][None, :], ln_v_bias[l][None, :], wsp[l], bsp[l],
                        mkv, l, wa[l], wb[l], wc[l], wo[l], fgain, l == depth - 1)
    return x
```

```python
import functools

import numpy as np
import jax
import jax.numpy as jnp
from jax import lax
from jax.experimental import pallas as pl
from jax.experimental.pallas import tpu as pltpu

F32 = jnp.float32
BF16 = jnp.bfloat16

HEAD_DIM = 128
A_GROUPS = 4
A_WIDTH = 512
CHUNK = 128
NSA_HEADS = 8
NSA_KV_GROUPS = 2
HEADS_PER_GROUP = NSA_HEADS // NSA_KV_GROUPS
NSA_WIDTH = NSA_HEADS * HEAD_DIM
CMP_BLOCK = 32
CMP_STRIDE = 16
SLC_BLOCK = 64
N_SELECT = 16
WINDOW = 512
MEM_HEADS = 4
MEM_WIDTH = MEM_HEADS * HEAD_DIM
ROPE_DIM = HEAD_DIM // 4
ROPE_THETA = 500000.0
EPS = 1e-6
NEG_INF = -1e30
FORCED_SCORE = 1e4
ATT_SCALE = HEAD_DIM ** -0.5
LOG2_E = 1.4426950408889634

_OFF = np.cumsum((0, A_WIDTH, A_WIDTH, A_WIDTH, NSA_WIDTH, NSA_WIDTH, NSA_HEADS * 3, 3 * 2 * NSA_KV_GROUPS * HEAD_DIM,
                  MEM_WIDTH, MEM_WIDTH))
(_A_U, _A_V, _A_Z, _B_Q, _B_Z, _B_G, _B_KV, _C_Q, _C_Z, _MERGE) = (int(v) for v in _OFF)
GATE_PAD = 128

PROJ_TM = 512
NSA_TQ = 128
NSA_TK = 512
FUSE_TM = 256
VMEM_LIMIT = 56 * 1024 * 1024


def _nt_dot(a, b):
    return lax.dot_general(a, b, (((1,), (1,)), ((), ())), preferred_element_type=F32)


def _dot(a, b):
    return jnp.dot(a, b, preferred_element_type=F32)


def _rmsnorm(x, gain):
    return x * lax.rsqrt(jnp.mean(x * x, axis=-1, keepdims=True) + EPS) * gain


def _silu(x):
    return x * jax.nn.sigmoid(x)


def _resident(block_shape, index_map):
    return pl.BlockSpec(block_shape, index_map, pipeline_mode=pl.Buffered(1))


def _proj_kernel(x_ref, gain_ref, w_ref, cos_ref, sina_ref, sinb_ref,
                 q_ref, gate_ref, cmp_ref, ksel_ref, vsel_ref, kwin_ref, vwin_ref, *, n_slc):
    tm = x_ref.shape[1]
    t0 = pl.program_id(1) * tm
    h = _rmsnorm(x_ref[0], gain_ref[...]).astype(BF16)
    proj = _dot(h, w_ref[...])
    cosf, sina, sinb = cos_ref[0], sina_ref[0], sinb_ref[0]

    def rope(t):
        return t * cosf + pltpu.roll(t, HEAD_DIM - ROPE_DIM // 2, 1) * sina + pltpu.roll(t, ROPE_DIM // 2, 1) * sinb

    def col(j):
        return proj[:, j * HEAD_DIM:(j + 1) * HEAD_DIM]

    for hd in range(NSA_HEADS):
        q_ref[0, :, hd * HEAD_DIM:(hd + 1) * HEAD_DIM] = (rope(col(hd)) * (ATT_SCALE * LOG2_E)).astype(BF16)
    kv0 = NSA_HEADS
    g2 = NSA_KV_GROUPS
    row = t0 + lax.broadcasted_iota(jnp.int32, (tm, n_slc), 0)
    blk = lax.broadcasted_iota(jnp.int32, (tm, n_slc), 1)
    sel_onehot = (row // SLC_BLOCK == blk).astype(BF16)
    ones_col = (lax.broadcasted_iota(jnp.int32, (tm, HEAD_DIM), 1) == 0).astype(BF16)
    for g in range(g2):
        cmp_ref[0, g] = rope(col(kv0 + g))
        cmp_ref[0, g2 + g] = col(kv0 + g2 + g)
        ksel_ref[0, g, :, 0:HEAD_DIM] = rope(col(kv0 + 2 * g2 + g)).astype(BF16)
        ksel_ref[0, g, :, HEAD_DIM:] = sel_onehot
        vsel_ref[0, g, :, 0:HEAD_DIM] = col(kv0 + 3 * g2 + g).astype(BF16)
        vsel_ref[0, g, :, HEAD_DIM:] = ones_col
        kwin_ref[0, g] = rope(col(kv0 + 4 * g2 + g)).astype(BF16)
        vwin_ref[0, g, :, 0:HEAD_DIM] = col(kv0 + 5 * g2 + g).astype(BF16)
        vwin_ref[0, g, :, HEAD_DIM:] = ones_col
    gate_ref[0] = jax.nn.sigmoid(proj[:, (kv0 + 6 * g2) * HEAD_DIM:])


def _proj_call(x, gain, w_p, cosf, sina, sinb):
    b, s, d = x.shape
    tm = PROJ_TM
    n_slc = s // SLC_BLOCK
    g2 = NSA_KV_GROUPS
    wp_cols = w_p.shape[1]
    tok = lambda bi, i: (bi, i, 0)
    grp = lambda bi, i: (bi, 0, i, 0)
    return pl.pallas_call(
        functools.partial(_proj_kernel, n_slc=n_slc),
        grid=(b, s // tm),
        in_specs=[
            pl.BlockSpec((1, tm, d), tok),
            pl.BlockSpec((1, d), lambda bi, i: (0, 0)),
            _resident((d, wp_cols), lambda bi, i: (0, 0)),
            pl.BlockSpec((1, tm, HEAD_DIM), tok),
            pl.BlockSpec((1, tm, HEAD_DIM), tok),
            pl.BlockSpec((1, tm, HEAD_DIM), tok),
        ],
        out_specs=[
            pl.BlockSpec((1, tm, NSA_WIDTH), tok),
            pl.BlockSpec((1, tm, GATE_PAD), tok),
            pl.BlockSpec((1, 2 * g2, tm, HEAD_DIM), grp),
            pl.BlockSpec((1, g2, tm, HEAD_DIM + n_slc), grp),
            pl.BlockSpec((1, g2, tm, 2 * HEAD_DIM), grp),
            pl.BlockSpec((1, g2, tm, HEAD_DIM), grp),
            pl.BlockSpec((1, g2, tm, 2 * HEAD_DIM), grp),
        ],
        out_shape=[
            jax.ShapeDtypeStruct((b, s, NSA_WIDTH), BF16),
            jax.ShapeDtypeStruct((b, s, GATE_PAD), F32),
            jax.ShapeDtypeStruct((b, 2 * g2, s, HEAD_DIM), F32),
            jax.ShapeDtypeStruct((b, g2, s, HEAD_DIM + n_slc), BF16),
            jax.ShapeDtypeStruct((b, g2, s, 2 * HEAD_DIM), BF16),
            jax.ShapeDtypeStruct((b, g2, s, HEAD_DIM), BF16),
            jax.ShapeDtypeStruct((b, g2, s, 2 * HEAD_DIM), BF16),
        ],
        compiler_params=pltpu.CompilerParams(
            dimension_semantics=("arbitrary", "arbitrary"), vmem_limit_bytes=VMEM_LIMIT),
        name="nsa_proj",
    )(x, gain, w_p, cosf, sina, sinb)


def _compress_kernel(x_ref, pe_ref, w1_ref, w2_ref, o_ref):
    n_chunk = o_ref.shape[2]
    first = jnp.zeros((n_chunk, HEAD_DIM), F32)
    second = jnp.zeros((n_chunk, HEAD_DIM), F32)
    for c in range(CMP_STRIDE):
        rows = x_ref[0, 0, pl.ds(c, n_chunk, stride=CMP_STRIDE), :]
        first = first + _dot((rows + pe_ref[0, c:c + 1, :]).astype(BF16), w1_ref[0, c])
        second = second + _dot((rows + pe_ref[0, CMP_STRIDE + c:CMP_STRIDE + c + 1, :]).astype(BF16),
                               w1_ref[0, CMP_STRIDE + c])
    hidden = first + pltpu.roll(second, n_chunk - 1, 0)
    o_ref[0, 0] = _dot(_silu(hidden).astype(BF16), w2_ref[0]).astype(BF16)


def _compress_call(cmp_in, pe, w1, w2):
    b, n4, s, dh = cmp_in.shape
    g2 = NSA_KV_GROUPS
    n_chunk = s // CMP_STRIDE
    return pl.pallas_call(
        _compress_kernel,
        grid=(b, n4),
        in_specs=[
            pl.BlockSpec((1, 1, s, dh), lambda bi, j: (bi, j, 0, 0)),
            pl.BlockSpec((1, CMP_BLOCK, dh), lambda bi, j: (j // g2, 0, 0)),
            pl.BlockSpec((1, CMP_BLOCK, dh, dh), lambda bi, j: (j // g2, 0, 0, 0)),
            pl.BlockSpec((1, dh, dh), lambda bi, j: (j // g2, 0, 0)),
        ],
        out_specs=pl.BlockSpec((1, 1, n_chunk, dh), lambda bi, j: (bi, j, 0, 0)),
        out_shape=jax.ShapeDtypeStruct((b, n4, n_chunk, dh), BF16),
        compiler_params=pltpu.CompilerParams(
            dimension_semantics=("arbitrary", "arbitrary"), vmem_limit_bytes=VMEM_LIMIT),
        name="nsa_compress",
    )(cmp_in, pe, w1, w2)


def _nsa_kernel(q_ref, gate_ref, cmp_ref, ksel_ref, vsel_ref, kwin_ref, vwin_ref, mcs_ref, o_ref,
                lhs_ref, m_ref, acc_ref, part_ref, *, seq):
    tq = q_ref.shape[1]
    rows = HEADS_PER_GROUP * tq
    n_chunk = seq // CMP_STRIDE
    n_cmp = n_chunk - 1
    n_slc = seq // SLC_BLOCK
    groups = range(NSA_KV_GROUPS)
    i = pl.program_id(1)
    t0 = i * tq
    tpos = t0 + lax.broadcasted_iota(jnp.int32, (rows, 1), 0) % tq

    def softmax_numerator(s, m):
        return jnp.concatenate(
            [jnp.exp2(s[:, c * HEAD_DIM:(c + 1) * HEAD_DIM] - m) for c in range(s.shape[1] // HEAD_DIM)],
            axis=1).astype(BF16)

    def normalise(acc):
        return acc[:, 0:HEAD_DIM] * (1.0 / acc[:, HEAD_DIM:HEAD_DIM + 1])

    qgs = [jnp.concatenate(
        [q_ref[0, :, (HEADS_PER_GROUP * g + r) * HEAD_DIM:(HEADS_PER_GROUP * g + r + 1) * HEAD_DIM]
         for r in range(HEADS_PER_GROUP)], axis=0) for g in groups]

    ncol = lax.broadcasted_iota(jnp.int32, (1, n_chunk), 1)
    cmask = (ncol * CMP_STRIDE + (CMP_BLOCK - 1) <= tpos) & (ncol < n_cmp)
    imps = []
    for g in groups:
        s_c = jnp.where(cmask, _nt_dot(qgs[g], cmp_ref[0, g]), NEG_INF)
        e_c = jnp.where(cmask, jnp.exp2(s_c - jnp.max(s_c, axis=1, keepdims=True)), 0.0)
        l_c = jnp.sum(e_c, axis=1, keepdims=True)
        p_c = e_c * (1.0 / jnp.where(l_c > 0.0, l_c, 1.0))
        part_ref[0, g] = _dot(p_c.astype(BF16), cmp_ref[0, NSA_KV_GROUPS + g])
        p_sum = p_c[0:tq]
        for r in range(1, HEADS_PER_GROUP):
            p_sum = p_sum + p_c[r * tq:(r + 1) * tq]
        p_hi = p_sum.astype(BF16)
        rest = p_sum - p_hi.astype(F32)
        p_mid = rest.astype(BF16)
        p_lo = (rest - p_mid.astype(F32)).astype(BF16)
        mcs = mcs_ref[...]
        imps.append(_dot(p_hi, mcs) + _dot(p_mid, mcs) + _dot(p_lo, mcs))

    jrow = lax.broadcasted_iota(jnp.int32, (n_slc, tq), 0)
    cur = (t0 + lax.broadcasted_iota(jnp.int32, (n_slc, tq), 1)) // SLC_BLOCK
    forced = (jrow == 0) | (jrow == cur) | (jrow == cur - 1)
    for g in groups:
        cand = jnp.where(forced, -jnp.inf, jnp.where(jrow <= cur, imps[g].T, -1.0))
        sel_t = forced
        for _ in range(min(N_SELECT, n_slc) - 3):
            best = jnp.max(cand, axis=0, keepdims=True)
            first = jnp.min(jnp.where(cand == best, jrow, n_slc), axis=0, keepdims=True)
            hit = jrow == first
            sel_t = sel_t | hit
            cand = jnp.where(hit, -jnp.inf, cand)
        sel_bias = jnp.where(sel_t, 0.0, NEG_INF).T.astype(BF16)
        lhs_ref[g, :, 0:HEAD_DIM] = qgs[g]
        for r in range(HEADS_PER_GROUP):
            lhs_ref[g, r * tq:(r + 1) * tq, HEAD_DIM:] = sel_bias
        m_ref[g] = jnp.full((rows, HEAD_DIM), NEG_INF, F32)
        acc_ref[g] = jnp.zeros((rows, 2 * HEAD_DIM), F32)

    wlen = WINDOW + tq
    w0 = pl.multiple_of(jnp.maximum(t0 - WINDOW, 0), tq)
    rel = tpos - (w0 + lax.broadcasted_iota(jnp.int32, (1, wlen), 1))
    wmask = (rel >= 0) & (rel < WINDOW)
    for g in groups:
        s_w = jnp.where(wmask, _nt_dot(qgs[g], kwin_ref[0, g, pl.ds(w0, wlen), :]), NEG_INF)
        p_w = jnp.exp2(s_w - jnp.max(s_w, axis=1, keepdims=True)).astype(BF16)
        part_ref[1, g] = normalise(_dot(p_w, vwin_ref[0, g, pl.ds(w0, wlen), :]))

    def sweep(kt, masked):
        k0 = pl.multiple_of(kt * NSA_TK, NSA_TK)
        for g in groups:
            s_s = _nt_dot(lhs_ref[g], ksel_ref[0, g, pl.ds(k0, NSA_TK), :])
            if masked:
                kpos = k0 + lax.broadcasted_iota(jnp.int32, (1, NSA_TK), 1)
                s_s = jnp.where(kpos <= tpos, s_s, NEG_INF)
            m_old = m_ref[g]
            m_new = jnp.maximum(m_old, jnp.max(s_s, axis=1, keepdims=True))
            alpha = jnp.exp2(m_old - m_new)
            pv = _dot(softmax_numerator(s_s, m_new), vsel_ref[0, g, pl.ds(k0, NSA_TK), :])
            acc_ref[g] = jnp.concatenate([alpha, alpha], axis=1) * acc_ref[g] + pv
            m_ref[g] = m_new

    n_full = t0 // NSA_TK

    def full_body(kt, carry):
        sweep(kt, False)
        return carry

    lax.fori_loop(0, n_full, full_body, 0)
    sweep(n_full, True)

    for g in groups:
        o_cmp, o_win, o_slc = part_ref[0, g], part_ref[1, g], normalise(acc_ref[g])
        for r in range(HEADS_PER_GROUP):
            hd = HEADS_PER_GROUP * g + r
            gates = gate_ref[0, :, 3 * hd:3 * hd + 3]
            sl = slice(r * tq, (r + 1) * tq)
            o_ref[0, :, hd * HEAD_DIM:(hd + 1) * HEAD_DIM] = (
                gates[:, 0:1] * o_cmp[sl] + gates[:, 1:2] * o_slc[sl] + gates[:, 2:3] * o_win[sl])


def _nsa_call(q, gates, cmp, ksel, vsel, kwin, vwin, mcs):
    b, s, _ = q.shape
    tq = NSA_TQ
    g2 = NSA_KV_GROUPS
    n_chunk = s // CMP_STRIDE
    n_slc = s // SLC_BLOCK
    rows = HEADS_PER_GROUP * tq
    tok = lambda bi, i: (bi, i, 0)
    per_batch = lambda bi, i: (bi, 0, 0, 0)
    return pl.pallas_call(
        functools.partial(_nsa_kernel, seq=s),
        grid=(b, s // tq),
        in_specs=[
            pl.BlockSpec((1, tq, NSA_WIDTH), tok),
            pl.BlockSpec((1, tq, GATE_PAD), tok),
            _resident((1, 2 * g2, n_chunk, HEAD_DIM), per_batch),
            _resident((1, g2, s, HEAD_DIM + n_slc), per_batch),
            _resident((1, g2, s, 2 * HEAD_DIM), per_batch),
            _resident((1, g2, s, HEAD_DIM), per_batch),
            _resident((1, g2, s, 2 * HEAD_DIM), per_batch),
            _resident((n_chunk, n_slc), lambda bi, i: (0, 0)),
        ],
        out_specs=pl.BlockSpec((1, tq, NSA_WIDTH), tok),
        out_shape=jax.ShapeDtypeStruct((b, s, NSA_WIDTH), F32),
        scratch_shapes=[
            pltpu.VMEM((g2, rows, HEAD_DIM + n_slc), BF16),
            pltpu.VMEM((g2, rows, HEAD_DIM), F32),
            pltpu.VMEM((g2, rows, 2 * HEAD_DIM), F32),
            pltpu.VMEM((2, g2, rows, HEAD_DIM), F32),
        ],
        compiler_params=pltpu.CompilerParams(
            dimension_semantics=("arbitrary", "arbitrary"), vmem_limit_bytes=VMEM_LIMIT),
        name="nsa_attention",
    )(q, gates, cmp, ksel, vsel, kwin, vwin, mcs)


def _memkv_kernel(mem_ref, gain_ref, w_ref, o_ref):
    mem_n = _rmsnorm(mem_ref[0], gain_ref[...]).astype(BF16)
    o_ref[0, 0] = _dot(mem_n, w_ref[0]).astype(BF16)


def _memkv_call(mem, gain, w_mem):
    depth = w_mem.shape[0]
    b, m, d = mem.shape
    return pl.pallas_call(
        _memkv_kernel,
        grid=(depth, b),
        in_specs=[
            pl.BlockSpec((1, m, d), lambda l, bi: (bi, 0, 0)),
            pl.BlockSpec((1, d), lambda l, bi: (0, 0)),
            pl.BlockSpec((1, d, 2 * MEM_WIDTH), lambda l, bi: (l, 0, 0)),
        ],
        out_specs=pl.BlockSpec((1, 1, m, 2 * MEM_WIDTH), lambda l, bi: (l, bi, 0, 0)),
        out_shape=jax.ShapeDtypeStruct((depth, b, m, 2 * MEM_WIDTH), BF16),
        compiler_params=pltpu.CompilerParams(
            dimension_semantics=("arbitrary", "arbitrary"), vmem_limit_bytes=VMEM_LIMIT),
        name="mem_kv",
    )(mem, gain, w_mem)


_F_AUVZ = 0
_F_BZ = 3 * A_WIDTH
_F_CQZ = _F_BZ + NSA_WIDTH
_F_MERGE = _F_CQZ + 2 * MEM_WIDTH


def _fused_kernel(x_ref, ob_ref, gain_ref, w_ref, lng_ref, lnb_ref, wsp_ref, bsp_ref, mkv_ref,
                  wa_ref, wb_ref, wc_ref, wo_ref, fgain_ref, o_ref, *, final):
    tm, d = x_ref.shape[1], x_ref.shape[2]
    x = x_ref[0]
    h = _rmsnorm(x, gain_ref[...]).astype(BF16)

    def proj(c0, width):
        return _dot(h, w_ref[:, c0:c0 + width])

    def merge_gate(k):
        return jax.nn.sigmoid(proj(_F_MERGE + k * d, d))

    uvz = proj(_F_AUVZ, 3 * A_WIDTH)
    u, v, z = uvz[:, 0:A_WIDTH], uvz[:, A_WIDTH:2 * A_WIDTH], uvz[:, 2 * A_WIDTH:]
    vc = v - jnp.mean(v, axis=-1, keepdims=True)
    vn = (vc * lax.rsqrt(jnp.mean(vc * vc, axis=-1, keepdims=True) + EPS) * lng_ref[...] + lnb_ref[...]).astype(BF16)
    gdim = A_WIDTH // A_GROUPS
    mixed = jnp.concatenate([
        jnp.concatenate([
            _dot(wsp_ref[gi], vn[c * CHUNK:(c + 1) * CHUNK, gi * gdim:(gi + 1) * gdim]) + bsp_ref[gi]
            for gi in range(A_GROUPS)], axis=1)
        for c in range(tm // CHUNK)], axis=0)
    o_a = (u * mixed * _silu(z)).astype(BF16)
    acc = merge_gate(0) * _dot(o_a, wa_ref[...])

    o_b = (ob_ref[0] * _silu(proj(_F_BZ, NSA_WIDTH))).astype(BF16)
    acc = acc + merge_gate(1) * _dot(o_b, wb_ref[...])

    cqz = proj(_F_CQZ, 2 * MEM_WIDTH)
    heads = []
    for hd in range(MEM_HEADS):
        sl = slice(hd * HEAD_DIM, (hd + 1) * HEAD_DIM)
        s_m = _nt_dot(cqz[:, sl].astype(BF16), mkv_ref[0, 0, :, sl]) * ATT_SCALE
        e_m = jnp.exp(s_m - jnp.max(s_m, axis=1, keepdims=True))
        p_m = e_m * (1.0 / jnp.sum(e_m, axis=1, keepdims=True))
        heads.append(_dot(p_m.astype(BF16), mkv_ref[0, 0, :, MEM_WIDTH + hd * HEAD_DIM:MEM_WIDTH + (hd + 1) * HEAD_DIM]))
    o_c = (jnp.concatenate(heads, axis=1) * _silu(cqz[:, MEM_WIDTH:])).astype(BF16)
    acc = acc + merge_gate(2) * _dot(o_c, wc_ref[...])

    x_new = x + _dot(acc.astype(BF16), wo_ref[...])
    if final:
        x_new = _rmsnorm(x_new, fgain_ref[...])
    o_ref[0] = x_new


def _fused_call(x, ob, gain, w_f, lng, lnb, wsp, bsp, mkv, layer, wa, wb, wc, wo, fgain, final):
    b, s, d = x.shape
    tm = FUSE_TM
    m = mkv.shape[2]
    tok = lambda bi, i: (bi, i, 0)
    c2 = lambda bi, i: (0, 0)
    c3 = lambda bi, i: (0, 0, 0)
    return pl.pallas_call(
        functools.partial(_fused_kernel, final=final),
        grid=(b, s // tm),
        in_specs=[
            pl.BlockSpec((1, tm, d), tok),
            pl.BlockSpec((1, tm, NSA_WIDTH), tok),
            pl.BlockSpec((1, d), c2),
            _resident(w_f.shape, c2),
            pl.BlockSpec((1, A_WIDTH), c2),
            pl.BlockSpec((1, A_WIDTH), c2),
            _resident(wsp.shape, c3),
            _resident(bsp.shape, c3),
            _resident((1, 1, m, 2 * MEM_WIDTH), lambda bi, i: (layer, bi, 0, 0)),
            _resident(wa.shape, c2),
            _resident(wb.shape, c2),
            _resident(wc.shape, c2),
            _resident(wo.shape, c2),
            pl.BlockSpec((1, d), c2),
        ],
        out_specs=pl.BlockSpec((1, tm, d), tok),
        out_shape=jax.ShapeDtypeStruct((b, s, d), F32),
        compiler_params=pltpu.CompilerParams(
            dimension_semantics=("arbitrary", "arbitrary"), vmem_limit_bytes=VMEM_LIMIT),
        name="fused_mix",
    )(x, ob, gain, w_f, lng, lnb, wsp, bsp, mkv, wa, wb, wc, wo, fgain)


def _rope_lane_tables(positions):
    half = ROPE_DIM // 2
    inv_freq = ROPE_THETA ** (-jnp.arange(0, ROPE_DIM, 2, dtype=F32) / ROPE_DIM)
    ang = positions.astype(F32)[..., None] * inv_freq
    cos, sin = jnp.cos(ang), jnp.sin(ang)
    zeros = jnp.zeros(cos.shape[:-1] + (HEAD_DIM - ROPE_DIM,), F32)
    cosf = jnp.concatenate([cos, cos, zeros + 1.0], axis=-1)
    sina = jnp.concatenate([-sin, jnp.zeros_like(sin), zeros], axis=-1)
    sinb = jnp.concatenate([jnp.zeros_like(sin), sin, zeros], axis=-1)
    del half
    return cosf, sina, sinb


def _cmp_to_slc(n_chunk, n_slc):
    i = np.arange(n_chunk)[:, None] * CMP_STRIDE
    j = np.arange(n_slc)[None, :] * SLC_BLOCK
    ov = np.clip(np.minimum(i + CMP_BLOCK, j + SLC_BLOCK) - np.maximum(i, j), 0, None) / CMP_BLOCK
    ov[n_chunk - 1] = 0.0
    return jnp.asarray(ov, dtype=BF16)


def kernel(x, mem, positions, norm_gain, w_in, ln_v_gain, ln_v_bias, w_spatial, b_spatial, cmp_pe_k, cmp_w1_k, cmp_w2_k, cmp_pe_v, cmp_w1_v, cmp_w2_v, mem_norm_gain, w_mem_kv, w_branch_a, w_branch_b, w_branch_c, w_out, final_norm_gain):
    depth = w_in.shape[0]
    b, s, d = x.shape
    assert s % NSA_TK == 0 and s % PROJ_TM == 0 and s >= WINDOW + NSA_TQ and d == w_in.shape[1]

    cosf, sina, sinb = _rope_lane_tables(positions)
    mcs = _cmp_to_slc(s // CMP_STRIDE, s // SLC_BLOCK)

    gate_w = jnp.pad(w_in[:, :, _B_G:_B_KV], ((0, 0), (0, 0), (0, GATE_PAD - (_B_KV - _B_G))))
    w_p = jnp.concatenate([w_in[:, :, _B_Q:_B_Z], w_in[:, :, _B_KV:_C_Q], gate_w], axis=-1).astype(BF16)
    w_f = jnp.concatenate([w_in[:, :, _A_U:_B_Q], w_in[:, :, _B_Z:_B_G], w_in[:, :, _C_Q:_MERGE],
                           w_in[:, :, _MERGE:]], axis=-1).astype(BF16)
    wsp = (w_spatial * jnp.tril(jnp.ones((CHUNK, CHUNK), w_spatial.dtype))).astype(BF16)
    bsp = jnp.broadcast_to(b_spatial[..., None], b_spatial.shape + (A_WIDTH // A_GROUPS,))
    pe = jnp.stack([cmp_pe_k, cmp_pe_v], axis=1)
    w1 = jnp.stack([cmp_w1_k, cmp_w1_v], axis=1).astype(BF16)
    w2 = jnp.stack([cmp_w2_k, cmp_w2_v], axis=1).astype(BF16)
    wa, wb, wc, wo = (w.astype(BF16) for w in (w_branch_a, w_branch_b, w_branch_c, w_out))

    mkv = _memkv_call(mem, mem_norm_gain[None, :], w_mem_kv.astype(BF16))
    fgain = final_norm_gain[None, :]
    for l in range(depth):
        gain = norm_gain[l][None, :]
        q, gates, cmp_in, ksel, vsel, kwin, vwin = _proj_call(x, gain, w_p[l], cosf, sina, sinb)
        cmp = _compress_call(cmp_in, pe[l], w1[l], w2[l])
        ob = _nsa_call(q, gates, cmp, ksel, vsel, kwin, vwin, mcs)
        x = _fused_call(x, ob, gain, w_f[l], ln_v_gain[l][None, :], ln_v_bias[l][None, :], wsp[l], bsp[l],
                        mkv, l, wa[l], wb[l], wc[l], wo[l], fgain, l == depth - 1)
    return x
```

```python
import functools

import numpy as np
import jax
import jax.numpy as jnp
from jax import lax
from jax.experimental import pallas as pl
from jax.experimental.pallas import tpu as pltpu

F32 = jnp.float32
BF16 = jnp.bfloat16

HEAD_DIM = 128
A_GROUPS = 4
A_WIDTH = 512
CHUNK = 128
NSA_HEADS = 8
NSA_KV_GROUPS = 2
HEADS_PER_GROUP = NSA_HEADS // NSA_KV_GROUPS
NSA_WIDTH = NSA_HEADS * HEAD_DIM
CMP_BLOCK = 32
CMP_STRIDE = 16
SLC_BLOCK = 64
N_SELECT = 16
WINDOW = 512
MEM_HEADS = 4
MEM_WIDTH = MEM_HEADS * HEAD_DIM
ROPE_DIM = HEAD_DIM // 4
ROPE_THETA = 500000.0
EPS = 1e-6
NEG_INF = -1e30
FORCED_SCORE = 1e4
ATT_SCALE = HEAD_DIM ** -0.5
LOG2_E = 1.4426950408889634

_OFF = np.cumsum((0, A_WIDTH, A_WIDTH, A_WIDTH, NSA_WIDTH, NSA_WIDTH, NSA_HEADS * 3, 3 * 2 * NSA_KV_GROUPS * HEAD_DIM,
                  MEM_WIDTH, MEM_WIDTH))
(_A_U, _A_V, _A_Z, _B_Q, _B_Z, _B_G, _B_KV, _C_Q, _C_Z, _MERGE) = (int(v) for v in _OFF)
GATE_PAD = 128
GATE_ROWS = 32
V_ROWS = HEAD_DIM + 16

PROJ_TM = 512
NSA_TQ = 128
NSA_TK = 512
FUSE_TM = 256
VMEM_LIMIT = 56 * 1024 * 1024


def _nt_dot(a, b):
    return lax.dot_general(a, b, (((1,), (1,)), ((), ())), preferred_element_type=F32)


def _dot(a, b):
    return jnp.dot(a, b, preferred_element_type=F32)


def _rmsnorm(x, gain):
    return x * lax.rsqrt(jnp.mean(x * x, axis=-1, keepdims=True) + EPS) * gain


def _silu(x):
    return x * jax.nn.sigmoid(x)


def _resident(block_shape, index_map):
    return pl.BlockSpec(block_shape, index_map, pipeline_mode=pl.Buffered(1))


def _proj_kernel(x_ref, gain_ref, w_ref, cos_ref, sina_ref, sinb_ref,
                 q_ref, gate_ref, cmp_ref, ksel_ref, vsel_ref, kwin_ref, vwin_ref, *, n_slc):
    tm = x_ref.shape[1]
    t0 = pl.program_id(1) * tm
    h = _rmsnorm(x_ref[0], gain_ref[...]).astype(BF16)
    proj = _dot(h, w_ref[...])
    cosf, sina, sinb = cos_ref[0], sina_ref[0], sinb_ref[0]

    def rope(t):
        return t * cosf + pltpu.roll(t, HEAD_DIM - ROPE_DIM // 2, 1) * sina + pltpu.roll(t, ROPE_DIM // 2, 1) * sinb

    def col(j):
        return proj[:, j * HEAD_DIM:(j + 1) * HEAD_DIM]

    for hd in range(NSA_HEADS):
        q_ref[0, :, hd * HEAD_DIM:(hd + 1) * HEAD_DIM] = (rope(col(hd)) * (ATT_SCALE * LOG2_E)).astype(BF16)
    kv0 = NSA_HEADS
    g2 = NSA_KV_GROUPS
    row = t0 + lax.broadcasted_iota(jnp.int32, (tm, n_slc), 0)
    blk = lax.broadcasted_iota(jnp.int32, (tm, n_slc), 1)
    sel_onehot = (row // SLC_BLOCK == blk).astype(BF16)
    ones_row = (lax.broadcasted_iota(jnp.int32, (V_ROWS - HEAD_DIM, tm), 0) == 0).astype(BF16)
    for g in range(g2):
        cmp_ref[0, g] = rope(col(kv0 + g))
        cmp_ref[0, g2 + g] = col(kv0 + g2 + g)
        ksel_ref[0, g, :, 0:HEAD_DIM] = rope(col(kv0 + 2 * g2 + g)).astype(BF16)
        ksel_ref[0, g, :, HEAD_DIM:] = sel_onehot
        kwin_ref[0, g] = rope(col(kv0 + 4 * g2 + g)).astype(BF16)
        for v_ref, j in ((vsel_ref, kv0 + 3 * g2 + g), (vwin_ref, kv0 + 5 * g2 + g)):
            v_ref[0, g, 0:HEAD_DIM, :] = col(j).T.astype(BF16)
            v_ref[0, g, HEAD_DIM:, :] = ones_row
    gate_ref[0] = jax.nn.sigmoid(proj[:, (kv0 + 6 * g2) * HEAD_DIM:]).T[0:GATE_ROWS]


def _proj_call(x, gain, w_p, cosf, sina, sinb):
    b, s, d = x.shape
    tm = PROJ_TM
    n_slc = s // SLC_BLOCK
    g2 = NSA_KV_GROUPS
    wp_cols = w_p.shape[1]
    tok = lambda bi, i: (bi, i, 0)
    grp = lambda bi, i: (bi, 0, i, 0)
    grp_t = lambda bi, i: (bi, 0, 0, i)
    return pl.pallas_call(
        functools.partial(_proj_kernel, n_slc=n_slc),
        grid=(b, s // tm),
        in_specs=[
            pl.BlockSpec((1, tm, d), tok),
            pl.BlockSpec((1, d), lambda bi, i: (0, 0)),
            _resident((d, wp_cols), lambda bi, i: (0, 0)),
            pl.BlockSpec((1, tm, HEAD_DIM), tok),
            pl.BlockSpec((1, tm, HEAD_DIM), tok),
            pl.BlockSpec((1, tm, HEAD_DIM), tok),
        ],
        out_specs=[
            pl.BlockSpec((1, tm, NSA_WIDTH), tok),
            pl.BlockSpec((1, GATE_ROWS, tm), lambda bi, i: (bi, 0, i)),
            pl.BlockSpec((1, 2 * g2, tm, HEAD_DIM), grp),
            pl.BlockSpec((1, g2, tm, HEAD_DIM + n_slc), grp),
            pl.BlockSpec((1, g2, V_ROWS, tm), grp_t),
            pl.BlockSpec((1, g2, tm, HEAD_DIM), grp),
            pl.BlockSpec((1, g2, V_ROWS, tm), grp_t),
        ],
        out_shape=[
            jax.ShapeDtypeStruct((b, s, NSA_WIDTH), BF16),
            jax.ShapeDtypeStruct((b, GATE_ROWS, s), F32),
            jax.ShapeDtypeStruct((b, 2 * g2, s, HEAD_DIM), F32),
            jax.ShapeDtypeStruct((b, g2, s, HEAD_DIM + n_slc), BF16),
            jax.ShapeDtypeStruct((b, g2, V_ROWS, s), BF16),
            jax.ShapeDtypeStruct((b, g2, s, HEAD_DIM), BF16),
            jax.ShapeDtypeStruct((b, g2, V_ROWS, s), BF16),
        ],
        compiler_params=pltpu.CompilerParams(
            dimension_semantics=("arbitrary", "arbitrary"), vmem_limit_bytes=VMEM_LIMIT),
        name="nsa_proj",
    )(x, gain, w_p, cosf, sina, sinb)


def _compress_kernel(x_ref, pe_ref, w1_ref, w2_ref, o_ref, ot_ref):
    n_chunk = o_ref.shape[2]
    first = jnp.zeros((n_chunk, HEAD_DIM), F32)
    second = jnp.zeros((n_chunk, HEAD_DIM), F32)
    for c in range(CMP_STRIDE):
        rows = x_ref[0, 0, pl.ds(c, n_chunk, stride=CMP_STRIDE), :]
        first = first + _dot((rows + pe_ref[0, c:c + 1, :]).astype(BF16), w1_ref[0, c])
        second = second + _dot((rows + pe_ref[0, CMP_STRIDE + c:CMP_STRIDE + c + 1, :]).astype(BF16),
                               w1_ref[0, CMP_STRIDE + c])
    hidden = first + pltpu.roll(second, n_chunk - 1, 0)
    out = _dot(_silu(hidden).astype(BF16), w2_ref[0])
    o_ref[0, 0] = out.astype(BF16)
    ot_ref[0, 0] = out.T.astype(BF16)


def _compress_call(cmp_in, pe, w1, w2):
    b, n4, s, dh = cmp_in.shape
    g2 = NSA_KV_GROUPS
    n_chunk = s // CMP_STRIDE
    return pl.pallas_call(
        _compress_kernel,
        grid=(b, n4),
        in_specs=[
            pl.BlockSpec((1, 1, s, dh), lambda bi, j: (bi, j, 0, 0)),
            pl.BlockSpec((1, CMP_BLOCK, dh), lambda bi, j: (j // g2, 0, 0)),
            pl.BlockSpec((1, CMP_BLOCK, dh, dh), lambda bi, j: (j // g2, 0, 0, 0)),
            pl.BlockSpec((1, dh, dh), lambda bi, j: (j // g2, 0, 0)),
        ],
        out_specs=[pl.BlockSpec((1, 1, n_chunk, dh), lambda bi, j: (bi, j, 0, 0)),
                   pl.BlockSpec((1, 1, dh, n_chunk), lambda bi, j: (bi, j, 0, 0))],
        out_shape=[jax.ShapeDtypeStruct((b, n4, n_chunk, dh), BF16),
                   jax.ShapeDtypeStruct((b, n4, dh, n_chunk), BF16)],
        compiler_params=pltpu.CompilerParams(
            dimension_semantics=("arbitrary", "arbitrary"), vmem_limit_bytes=VMEM_LIMIT),
        name="nsa_compress",
    )(cmp_in, pe, w1, w2)


def _nsa_kernel(q_ref, gate_ref, cmp_ref, cmpt_ref, ksel_ref, vsel_ref, kwin_ref, vwin_ref, mcs_ref, o_ref,
                lhs_ref, m_ref, acc_ref, part_ref, *, seq):
    tq = q_ref.shape[1]
    rows = HEADS_PER_GROUP * tq
    n_chunk = seq // CMP_STRIDE
    n_cmp = n_chunk - 1
    n_slc = seq // SLC_BLOCK
    groups = range(NSA_KV_GROUPS)
    i = pl.program_id(1)
    t0 = i * tq
    tpos = t0 + lax.broadcasted_iota(jnp.int32, (1, rows), 1) % tq

    def normalise(acc_t):
        return acc_t[0:HEAD_DIM] * (1.0 / acc_t[HEAD_DIM:HEAD_DIM + 1])

    qgs = [jnp.concatenate(
        [q_ref[0, :, (HEADS_PER_GROUP * g + r) * HEAD_DIM:(HEADS_PER_GROUP * g + r + 1) * HEAD_DIM]
         for r in range(HEADS_PER_GROUP)], axis=0) for g in groups]

    nrow = lax.broadcasted_iota(jnp.int32, (n_chunk, 1), 0)
    cmask = (nrow * CMP_STRIDE + (CMP_BLOCK - 1) <= tpos) & (nrow < n_cmp)
    sees_block = tpos >= CMP_BLOCK - 1
    wlen = WINDOW + tq
    w0 = pl.multiple_of(jnp.maximum(t0 - WINDOW, 0), tq)
    rel = tpos - (w0 + lax.broadcasted_iota(jnp.int32, (wlen, 1), 0))
    wmask = (rel >= 0) & (rel < WINDOW)
    imps = [None] * NSA_KV_GROUPS

    def run_pipelined(items, lag=2):
        scores = []
        for k in range(len(items) + lag):
            if k < len(items):
                scores.append(items[k][0]())
            if k >= lag:
                items[k - lag][1](scores[k - lag])
                scores[k - lag] = None

    def cmp_consume(g, s_c):
        s_c = jnp.where(cmask, s_c, NEG_INF)
        e_c = jnp.exp2(s_c - jnp.max(s_c, axis=0, keepdims=True))
        p_c = e_c * jnp.where(sees_block, 1.0 / jnp.sum(e_c, axis=0, keepdims=True), 0.0)
        part_ref[0, g] = _dot(cmpt_ref[0, NSA_KV_GROUPS + g], p_c.astype(BF16))
        p_sum = p_c[:, 0:tq]
        for r in range(1, HEADS_PER_GROUP):
            p_sum = p_sum + p_c[:, r * tq:(r + 1) * tq]
        p_hi = p_sum.astype(BF16)
        rest = p_sum - p_hi.astype(F32)
        p_mid = rest.astype(BF16)
        p_lo = (rest - p_mid.astype(F32)).astype(BF16)
        mcs_t = mcs_ref[...]
        imps[g] = _dot(mcs_t, p_hi) + _dot(mcs_t, p_mid) + _dot(mcs_t, p_lo)

    def win_consume(g, s_w):
        s_w = jnp.where(wmask, s_w, NEG_INF)
        p_w = jnp.exp2(s_w - jnp.max(s_w, axis=0, keepdims=True)).astype(BF16)
        part_ref[1, g] = normalise(_dot(vwin_ref[0, g, :, pl.ds(w0, wlen)], p_w))

    run_pipelined(
        [(functools.partial(_nt_dot, cmp_ref[0, g], qgs[g]), functools.partial(cmp_consume, g)) for g in groups]
        + [(functools.partial(_nt_dot, kwin_ref[0, g, pl.ds(w0, wlen), :], qgs[g]), functools.partial(win_consume, g))
           for g in groups])

    jrow = lax.broadcasted_iota(jnp.int32, (n_slc, tq), 0)
    cur = (t0 + lax.broadcasted_iota(jnp.int32, (n_slc, tq), 1)) // SLC_BLOCK
    forced = (jrow == 0) | (jrow == cur) | (jrow == cur - 1)
    for g in groups:
        cand = jnp.where(forced, -jnp.inf, jnp.where(jrow <= cur, imps[g], -1.0))
        sel_t = forced
        for _ in range(min(N_SELECT, n_slc) - 3):
            best = jnp.max(cand, axis=0, keepdims=True)
            first = jnp.min(jnp.where(cand == best, jrow, n_slc), axis=0, keepdims=True)
            hit = jrow == first
            sel_t = sel_t | hit
            cand = jnp.where(hit, -jnp.inf, cand)
        sel_bias = jnp.where(sel_t, 0.0, NEG_INF).T.astype(BF16)
        lhs_ref[g, :, 0:HEAD_DIM] = qgs[g]
        for r in range(HEADS_PER_GROUP):
            lhs_ref[g, r * tq:(r + 1) * tq, HEAD_DIM:] = sel_bias
        m_ref[g] = jnp.full((1, rows), NEG_INF, F32)
        acc_ref[g] = jnp.zeros((V_ROWS, rows), F32)

    def sel_score(g, k0):
        return _nt_dot(ksel_ref[0, g, pl.ds(k0, NSA_TK), :], lhs_ref[g])

    def sel_consume(g, k0, masked, s_s):
        if masked:
            kpos = k0 + lax.broadcasted_iota(jnp.int32, (NSA_TK, 1), 0)
            s_s = jnp.where(kpos <= tpos, s_s, NEG_INF)
        m_old = m_ref[g]
        m_new = jnp.maximum(m_old, jnp.max(s_s, axis=0, keepdims=True))
        p_s = jnp.exp2(s_s - m_new).astype(BF16)
        acc_ref[g] = jnp.exp2(m_old - m_new) * acc_ref[g] + _dot(vsel_ref[0, g, :, pl.ds(k0, NSA_TK)], p_s)
        m_ref[g] = m_new

    def sweep(first_tile, n_tiles, last_masked=False):
        items = []
        for t in range(n_tiles):
            k0 = pl.multiple_of((first_tile + t) * NSA_TK, NSA_TK)
            masked = last_masked and t == n_tiles - 1
            items += [(functools.partial(sel_score, g, k0), functools.partial(sel_consume, g, k0, masked))
                      for g in groups]
        run_pipelined(items)

    n_full = t0 // NSA_TK

    def quad_body(kq, carry):
        sweep(4 * kq, 4)
        return carry

    lax.fori_loop(0, n_full // 4, quad_body, 0)
    done = (n_full // 4) * 4

    @pl.when(n_full % 4 >= 2)
    def _():
        sweep(done, 2)

    done = (n_full // 2) * 2

    @pl.when(n_full % 2 == 1)
    def _():
        sweep(done, 2, last_masked=True)

    @pl.when(n_full % 2 == 0)
    def _():
        sweep(done, 1, last_masked=True)

    for g in groups:
        o_cmp, o_win, o_slc = part_ref[0, g], part_ref[1, g], normalise(acc_ref[g])
        for r in range(HEADS_PER_GROUP):
            hd = HEADS_PER_GROUP * g + r
            sl = slice(r * tq, (r + 1) * tq)
            mixed_t = (gate_ref[0, 3 * hd:3 * hd + 1, :] * o_cmp[:, sl] + gate_ref[0, 3 * hd + 1:3 * hd + 2, :] * o_slc[:, sl]
                       + gate_ref[0, 3 * hd + 2:3 * hd + 3, :] * o_win[:, sl])
            o_ref[0, :, hd * HEAD_DIM:(hd + 1) * HEAD_DIM] = mixed_t.T


def _nsa_call(q, gates_t, cmp, cmp_t, ksel, vsel_t, kwin, vwin_t, mcs_t):
    b, s, _ = q.shape
    tq = NSA_TQ
    g2 = NSA_KV_GROUPS
    n_chunk = s // CMP_STRIDE
    n_slc = s // SLC_BLOCK
    rows = HEADS_PER_GROUP * tq
    tok = lambda bi, i: (bi, i, 0)
    per_batch = lambda bi, i: (bi, 0, 0, 0)
    return pl.pallas_call(
        functools.partial(_nsa_kernel, seq=s),
        grid=(b, s // tq),
        in_specs=[
            pl.BlockSpec((1, tq, NSA_WIDTH), tok),
            pl.BlockSpec((1, GATE_ROWS, tq), lambda bi, i: (bi, 0, i)),
            _resident((1, 2 * g2, n_chunk, HEAD_DIM), per_batch),
            _resident((1, 2 * g2, HEAD_DIM, n_chunk), per_batch),
            _resident((1, g2, s, HEAD_DIM + n_slc), per_batch),
            _resident((1, g2, V_ROWS, s), per_batch),
            _resident((1, g2, s, HEAD_DIM), per_batch),
            _resident((1, g2, V_ROWS, s), per_batch),
            _resident((n_slc, n_chunk), lambda bi, i: (0, 0)),
        ],
        out_specs=pl.BlockSpec((1, tq, NSA_WIDTH), tok),
        out_shape=jax.ShapeDtypeStruct((b, s, NSA_WIDTH), F32),
        scratch_shapes=[
            pltpu.VMEM((g2, rows, HEAD_DIM + n_slc), BF16),
            pltpu.VMEM((g2, 1, rows), F32),
            pltpu.VMEM((g2, V_ROWS, rows), F32),
            pltpu.VMEM((2, g2, HEAD_DIM, rows), F32),
        ],
        compiler_params=pltpu.CompilerParams(
            dimension_semantics=("arbitrary", "arbitrary"), vmem_limit_bytes=VMEM_LIMIT),
        name="nsa_attention",
    )(q, gates_t, cmp, cmp_t, ksel, vsel_t, kwin, vwin_t, mcs_t)


def _memkv_kernel(mem_ref, gain_ref, w_ref, o_ref):
    mem_n = _rmsnorm(mem_ref[0], gain_ref[...]).astype(BF16)
    o_ref[0, 0] = _dot(mem_n, w_ref[0]).astype(BF16)


def _memkv_call(mem, gain, w_mem):
    depth = w_mem.shape[0]
    b, m, d = mem.shape
    return pl.pallas_call(
        _memkv_kernel,
        grid=(depth, b),
        in_specs=[
            pl.BlockSpec((1, m, d), lambda l, bi: (bi, 0, 0)),
            pl.BlockSpec((1, d), lambda l, bi: (0, 0)),
            pl.BlockSpec((1, d, 2 * MEM_WIDTH), lambda l, bi: (l, 0, 0)),
        ],
        out_specs=pl.BlockSpec((1, 1, m, 2 * MEM_WIDTH), lambda l, bi: (l, bi, 0, 0)),
        out_shape=jax.ShapeDtypeStruct((depth, b, m, 2 * MEM_WIDTH), BF16),
        compiler_params=pltpu.CompilerParams(
            dimension_semantics=("arbitrary", "arbitrary"), vmem_limit_bytes=VMEM_LIMIT),
        name="mem_kv",
    )(mem, gain, w_mem)


_F_AUVZ = 0
_F_BZ = 3 * A_WIDTH
_F_CQZ = _F_BZ + NSA_WIDTH
_F_MERGE = _F_CQZ + 2 * MEM_WIDTH


def _fused_kernel(x_ref, ob_ref, gain_ref, w_ref, lng_ref, lnb_ref, wsp_ref, bsp_ref, mkv_ref,
                  wa_ref, wb_ref, wc_ref, wo_ref, fgain_ref, o_ref, *, final):
    tm, d = x_ref.shape[1], x_ref.shape[2]
    x = x_ref[0]
    h = _rmsnorm(x, gain_ref[...]).astype(BF16)

    def proj(c0, width):
        return _dot(h, w_ref[:, c0:c0 + width])

    def merge_gate(k):
        return jax.nn.sigmoid(proj(_F_MERGE + k * d, d))

    uvz = proj(_F_AUVZ, 3 * A_WIDTH)
    u, v, z = uvz[:, 0:A_WIDTH], uvz[:, A_WIDTH:2 * A_WIDTH], uvz[:, 2 * A_WIDTH:]
    vc = v - jnp.mean(v, axis=-1, keepdims=True)
    vn = (vc * lax.rsqrt(jnp.mean(vc * vc, axis=-1, keepdims=True) + EPS) * lng_ref[...] + lnb_ref[...]).astype(BF16)
    gdim = A_WIDTH // A_GROUPS
    mixed = jnp.concatenate([
        jnp.concatenate([
            _dot(wsp_ref[gi], vn[c * CHUNK:(c + 1) * CHUNK, gi * gdim:(gi + 1) * gdim]) + bsp_ref[gi]
            for gi in range(A_GROUPS)], axis=1)
        for c in range(tm // CHUNK)], axis=0)
    o_a = (u * mixed * _silu(z)).astype(BF16)
    acc = merge_gate(0) * _dot(o_a, wa_ref[...])

    o_b = (ob_ref[0] * _silu(proj(_F_BZ, NSA_WIDTH))).astype(BF16)
    acc = acc + merge_gate(1) * _dot(o_b, wb_ref[...])

    cqz = proj(_F_CQZ, 2 * MEM_WIDTH)
    heads = []
    for hd in range(MEM_HEADS):
        sl = slice(hd * HEAD_DIM, (hd + 1) * HEAD_DIM)
        s_m = _nt_dot(cqz[:, sl].astype(BF16), mkv_ref[0, 0, :, sl]) * ATT_SCALE
        e_m = jnp.exp(s_m - jnp.max(s_m, axis=1, keepdims=True))
        p_m = e_m * (1.0 / jnp.sum(e_m, axis=1, keepdims=True))
        heads.append(_dot(p_m.astype(BF16), mkv_ref[0, 0, :, MEM_WIDTH + hd * HEAD_DIM:MEM_WIDTH + (hd + 1) * HEAD_DIM]))
    o_c = (jnp.concatenate(heads, axis=1) * _silu(cqz[:, MEM_WIDTH:])).astype(BF16)
    acc = acc + merge_gate(2) * _dot(o_c, wc_ref[...])

    x_new = x + _dot(acc.astype(BF16), wo_ref[...])
    if final:
        x_new = _rmsnorm(x_new, fgain_ref[...])
    o_ref[0] = x_new


def _fused_call(x, ob, gain, w_f, lng, lnb, wsp, bsp, mkv, layer, wa, wb, wc, wo, fgain, final):
    b, s, d = x.shape
    tm = FUSE_TM
    m = mkv.shape[2]
    tok = lambda bi, i: (bi, i, 0)
    c2 = lambda bi, i: (0, 0)
    c3 = lambda bi, i: (0, 0, 0)
    return pl.pallas_call(
        functools.partial(_fused_kernel, final=final),
        grid=(b, s // tm),
        in_specs=[
            pl.BlockSpec((1, tm, d), tok),
            pl.BlockSpec((1, tm, NSA_WIDTH), tok),
            pl.BlockSpec((1, d), c2),
            _resident(w_f.shape, c2),
            pl.BlockSpec((1, A_WIDTH), c2),
            pl.BlockSpec((1, A_WIDTH), c2),
            _resident(wsp.shape, c3),
            _resident(bsp.shape, c3),
            _resident((1, 1, m, 2 * MEM_WIDTH), lambda bi, i: (layer, bi, 0, 0)),
            _resident(wa.shape, c2),
            _resident(wb.shape, c2),
            _resident(wc.shape, c2),
            _resident(wo.shape, c2),
            pl.BlockSpec((1, d), c2),
        ],
        out_specs=pl.BlockSpec((1, tm, d), tok),
        out_shape=jax.ShapeDtypeStruct((b, s, d), F32),
        compiler_params=pltpu.CompilerParams(
            dimension_semantics=("arbitrary", "arbitrary"), vmem_limit_bytes=VMEM_LIMIT),
        name="fused_mix",
    )(x, ob, gain, w_f, lng, lnb, wsp, bsp, mkv, wa, wb, wc, wo, fgain)


def _rope_lane_tables(positions):
    half = ROPE_DIM // 2
    inv_freq = ROPE_THETA ** (-jnp.arange(0, ROPE_DIM, 2, dtype=F32) / ROPE_DIM)
    ang = positions.astype(F32)[..., None] * inv_freq
    cos, sin = jnp.cos(ang), jnp.sin(ang)
    zeros = jnp.zeros(cos.shape[:-1] + (HEAD_DIM - ROPE_DIM,), F32)
    cosf = jnp.concatenate([cos, cos, zeros + 1.0], axis=-1)
    sina = jnp.concatenate([-sin, jnp.zeros_like(sin), zeros], axis=-1)
    sinb = jnp.concatenate([jnp.zeros_like(sin), sin, zeros], axis=-1)
    del half
    return cosf, sina, sinb


def _cmp_to_slc(n_chunk, n_slc):
    i = np.arange(n_chunk)[:, None] * CMP_STRIDE
    j = np.arange(n_slc)[None, :] * SLC_BLOCK
    ov = np.clip(np.minimum(i + CMP_BLOCK, j + SLC_BLOCK) - np.maximum(i, j), 0, None) / CMP_BLOCK
    ov[n_chunk - 1] = 0.0
    return jnp.asarray(ov.T, dtype=BF16)


def kernel(x, mem, positions, norm_gain, w_in, ln_v_gain, ln_v_bias, w_spatial, b_spatial, cmp_pe_k, cmp_w1_k, cmp_w2_k, cmp_pe_v, cmp_w1_v, cmp_w2_v, mem_norm_gain, w_mem_kv, w_branch_a, w_branch_b, w_branch_c, w_out, final_norm_gain):
    depth = w_in.shape[0]
    b, s, d = x.shape
    assert s % NSA_TK == 0 and s % PROJ_TM == 0 and s >= WINDOW + NSA_TQ and d == w_in.shape[1]

    cosf, sina, sinb = _rope_lane_tables(positions)
    mcs = _cmp_to_slc(s // CMP_STRIDE, s // SLC_BLOCK)

    gate_w = jnp.pad(w_in[:, :, _B_G:_B_KV], ((0, 0), (0, 0), (0, GATE_PAD - (_B_KV - _B_G))))
    w_p = jnp.concatenate([w_in[:, :, _B_Q:_B_Z], w_in[:, :, _B_KV:_C_Q], gate_w], axis=-1).astype(BF16)
    w_f = jnp.concatenate([w_in[:, :, _A_U:_B_Q], w_in[:, :, _B_Z:_B_G], w_in[:, :, _C_Q:_MERGE],
                           w_in[:, :, _MERGE:]], axis=-1).astype(BF16)
    wsp = (w_spatial * jnp.tril(jnp.ones((CHUNK, CHUNK), w_spatial.dtype))).astype(BF16)
    bsp = jnp.broadcast_to(b_spatial[..., None], b_spatial.shape + (A_WIDTH // A_GROUPS,))
    pe = jnp.stack([cmp_pe_k, cmp_pe_v], axis=1)
    w1 = jnp.stack([cmp_w1_k, cmp_w1_v], axis=1).astype(BF16)
    w2 = jnp.stack([cmp_w2_k, cmp_w2_v], axis=1).astype(BF16)
    wa, wb, wc, wo = (w.astype(BF16) for w in (w_branch_a, w_branch_b, w_branch_c, w_out))

    mkv = _memkv_call(mem, mem_norm_gain[None, :], w_mem_kv.astype(BF16))
    fgain = final_norm_gain[None, :]
    for l in range(depth):
        gain = norm_gain[l][None, :]
        q, gates_t, cmp_in, ksel, vsel_t, kwin, vwin_t = _proj_call(x, gain, w_p[l], cosf, sina, sinb)
        cmp, cmp_t = _compress_call(cmp_in, pe[l], w1[l], w2[l])
        ob = _nsa_call(q, gates_t, cmp, cmp_t, ksel, vsel_t, kwin, vwin_t, mcs)
        x = _fused_call(x, ob, gain, w_f[l], ln_v_gain[l][None, :], ln_v_bias[l][None, :], wsp[l], bsp[l],
                        mkv, l, wa[l], wb[l], wc[l], wo[l], fgain, l == depth - 1)
    return x
```

```python
import functools

import numpy as np
import jax
import jax.numpy as jnp
from jax import lax
from jax.experimental import pallas as pl
from jax.experimental.pallas import tpu as pltpu

F32 = jnp.float32
BF16 = jnp.bfloat16

HEAD_DIM = 128
A_GROUPS = 4
A_WIDTH = 512
CHUNK = 128
NSA_HEADS = 8
NSA_KV_GROUPS = 2
HEADS_PER_GROUP = NSA_HEADS // NSA_KV_GROUPS
NSA_WIDTH = NSA_HEADS * HEAD_DIM
CMP_BLOCK = 32
CMP_STRIDE = 16
SLC_BLOCK = 64
N_SELECT = 16
WINDOW = 512
MEM_HEADS = 4
MEM_WIDTH = MEM_HEADS * HEAD_DIM
ROPE_DIM = HEAD_DIM // 4
ROPE_THETA = 500000.0
EPS = 1e-6
NEG_INF = -1e30
FORCED_SCORE = 1e4
ATT_SCALE = HEAD_DIM ** -0.5
LOG2_E = 1.4426950408889634

_OFF = np.cumsum((0, A_WIDTH, A_WIDTH, A_WIDTH, NSA_WIDTH, NSA_WIDTH, NSA_HEADS * 3, 3 * 2 * NSA_KV_GROUPS * HEAD_DIM,
                  MEM_WIDTH, MEM_WIDTH))
(_A_U, _A_V, _A_Z, _B_Q, _B_Z, _B_G, _B_KV, _C_Q, _C_Z, _MERGE) = (int(v) for v in _OFF)
GATE_PAD = 128
GATE_ROWS = 32
V_ROWS = HEAD_DIM + 16

PROJ_TM = 512
NSA_TQ = 256
NSA_TK = 512
FUSE_TM = 256
VMEM_LIMIT = 56 * 1024 * 1024


def _nt_dot(a, b):
    return lax.dot_general(a, b, (((1,), (1,)), ((), ())), preferred_element_type=F32)


def _dot(a, b):
    return jnp.dot(a, b, preferred_element_type=F32)


def _rmsnorm(x, gain):
    return x * lax.rsqrt(jnp.mean(x * x, axis=-1, keepdims=True) + EPS) * gain


def _silu(x):
    return x * jax.nn.sigmoid(x)


def _resident(block_shape, index_map):
    return pl.BlockSpec(block_shape, index_map, pipeline_mode=pl.Buffered(1))


def _proj_kernel(x_ref, gain_ref, w_ref, cos_ref, sina_ref, sinb_ref,
                 q_ref, gate_ref, cmp_ref, ksel_ref, vsel_ref, kwin_ref, vwin_ref, *, n_slc):
    tm = x_ref.shape[1]
    t0 = pl.program_id(1) * tm
    h = _rmsnorm(x_ref[0], gain_ref[...]).astype(BF16)
    proj = _dot(h, w_ref[...])
    cosf, sina, sinb = cos_ref[0], sina_ref[0], sinb_ref[0]

    def rope(t):
        return t * cosf + pltpu.roll(t, HEAD_DIM - ROPE_DIM // 2, 1) * sina + pltpu.roll(t, ROPE_DIM // 2, 1) * sinb

    def col(j):
        return proj[:, j * HEAD_DIM:(j + 1) * HEAD_DIM]

    for hd in range(NSA_HEADS):
        q_ref[0, :, hd * HEAD_DIM:(hd + 1) * HEAD_DIM] = (rope(col(hd)) * (ATT_SCALE * LOG2_E)).astype(BF16)
    kv0 = NSA_HEADS
    g2 = NSA_KV_GROUPS
    row = t0 + lax.broadcasted_iota(jnp.int32, (tm, n_slc), 0)
    blk = lax.broadcasted_iota(jnp.int32, (tm, n_slc), 1)
    sel_onehot = (row // SLC_BLOCK == blk).astype(BF16)
    ones_row = (lax.broadcasted_iota(jnp.int32, (V_ROWS - HEAD_DIM, tm), 0) == 0).astype(BF16)
    for g in range(g2):
        cmp_ref[0, g] = rope(col(kv0 + g))
        cmp_ref[0, g2 + g] = col(kv0 + g2 + g)
        ksel_ref[0, g, :, 0:HEAD_DIM] = rope(col(kv0 + 2 * g2 + g)).astype(BF16)
        ksel_ref[0, g, :, HEAD_DIM:] = sel_onehot
        kwin_ref[0, g] = rope(col(kv0 + 4 * g2 + g)).astype(BF16)
        for v_ref, j in ((vsel_ref, kv0 + 3 * g2 + g), (vwin_ref, kv0 + 5 * g2 + g)):
            v_ref[0, g, 0:HEAD_DIM, :] = col(j).T.astype(BF16)
            v_ref[0, g, HEAD_DIM:, :] = ones_row
    gate_ref[0] = jax.nn.sigmoid(proj[:, (kv0 + 6 * g2) * HEAD_DIM:]).T[0:GATE_ROWS]


def _proj_call(x, gain, w_p, cosf, sina, sinb):
    b, s, d = x.shape
    tm = PROJ_TM
    n_slc = s // SLC_BLOCK
    g2 = NSA_KV_GROUPS
    wp_cols = w_p.shape[1]
    tok = lambda bi, i: (bi, i, 0)
    grp = lambda bi, i: (bi, 0, i, 0)
    grp_t = lambda bi, i: (bi, 0, 0, i)
    return pl.pallas_call(
        functools.partial(_proj_kernel, n_slc=n_slc),
        grid=(b, s // tm),
        in_specs=[
            pl.BlockSpec((1, tm, d), tok),
            pl.BlockSpec((1, d), lambda bi, i: (0, 0)),
            _resident((d, wp_cols), lambda bi, i: (0, 0)),
            pl.BlockSpec((1, tm, HEAD_DIM), tok),
            pl.BlockSpec((1, tm, HEAD_DIM), tok),
            pl.BlockSpec((1, tm, HEAD_DIM), tok),
        ],
        out_specs=[
            pl.BlockSpec((1, tm, NSA_WIDTH), tok),
            pl.BlockSpec((1, GATE_ROWS, tm), lambda bi, i: (bi, 0, i)),
            pl.BlockSpec((1, 2 * g2, tm, HEAD_DIM), grp),
            pl.BlockSpec((1, g2, tm, HEAD_DIM + n_slc), grp),
            pl.BlockSpec((1, g2, V_ROWS, tm), grp_t),
            pl.BlockSpec((1, g2, tm, HEAD_DIM), grp),
            pl.BlockSpec((1, g2, V_ROWS, tm), grp_t),
        ],
        out_shape=[
            jax.ShapeDtypeStruct((b, s, NSA_WIDTH), BF16),
            jax.ShapeDtypeStruct((b, GATE_ROWS, s), F32),
            jax.ShapeDtypeStruct((b, 2 * g2, s, HEAD_DIM), F32),
            jax.ShapeDtypeStruct((b, g2, s, HEAD_DIM + n_slc), BF16),
            jax.ShapeDtypeStruct((b, g2, V_ROWS, s), BF16),
            jax.ShapeDtypeStruct((b, g2, s, HEAD_DIM), BF16),
            jax.ShapeDtypeStruct((b, g2, V_ROWS, s), BF16),
        ],
        compiler_params=pltpu.CompilerParams(
            dimension_semantics=("arbitrary", "arbitrary"), vmem_limit_bytes=VMEM_LIMIT),
        name="nsa_proj",
    )(x, gain, w_p, cosf, sina, sinb)


def _compress_kernel(x_ref, pe_ref, w1_ref, w2_ref, o_ref, ot_ref):
    n_chunk = o_ref.shape[2]
    first = jnp.zeros((n_chunk, HEAD_DIM), F32)
    second = jnp.zeros((n_chunk, HEAD_DIM), F32)
    for c in range(CMP_STRIDE):
        rows = x_ref[0, 0, pl.ds(c, n_chunk, stride=CMP_STRIDE), :]
        first = first + _dot((rows + pe_ref[0, c:c + 1, :]).astype(BF16), w1_ref[0, c])
        second = second + _dot((rows + pe_ref[0, CMP_STRIDE + c:CMP_STRIDE + c + 1, :]).astype(BF16),
                               w1_ref[0, CMP_STRIDE + c])
    hidden = first + pltpu.roll(second, n_chunk - 1, 0)
    out = _dot(_silu(hidden).astype(BF16), w2_ref[0])
    o_ref[0, 0] = out.astype(BF16)
    ot_ref[0, 0] = out.T.astype(BF16)


def _compress_call(cmp_in, pe, w1, w2):
    b, n4, s, dh = cmp_in.shape
    g2 = NSA_KV_GROUPS
    n_chunk = s // CMP_STRIDE
    return pl.pallas_call(
        _compress_kernel,
        grid=(b, n4),
        in_specs=[
            pl.BlockSpec((1, 1, s, dh), lambda bi, j: (bi, j, 0, 0)),
            pl.BlockSpec((1, CMP_BLOCK, dh), lambda bi, j: (j // g2, 0, 0)),
            pl.BlockSpec((1, CMP_BLOCK, dh, dh), lambda bi, j: (j // g2, 0, 0, 0)),
            pl.BlockSpec((1, dh, dh), lambda bi, j: (j // g2, 0, 0)),
        ],
        out_specs=[pl.BlockSpec((1, 1, n_chunk, dh), lambda bi, j: (bi, j, 0, 0)),
                   pl.BlockSpec((1, 1, dh, n_chunk), lambda bi, j: (bi, j, 0, 0))],
        out_shape=[jax.ShapeDtypeStruct((b, n4, n_chunk, dh), BF16),
                   jax.ShapeDtypeStruct((b, n4, dh, n_chunk), BF16)],
        compiler_params=pltpu.CompilerParams(
            dimension_semantics=("arbitrary", "arbitrary"), vmem_limit_bytes=VMEM_LIMIT),
        name="nsa_compress",
    )(cmp_in, pe, w1, w2)


def _nsa_kernel(q_ref, gate_ref, cmp_ref, cmpt_ref, ksel_ref, vsel_ref, kwin_ref, vwin_ref, mcs_ref, o_ref,
                lhs_ref, m_ref, acc_ref, part_ref, *, seq):
    tq = q_ref.shape[1]
    rows = HEADS_PER_GROUP * tq
    n_chunk = seq // CMP_STRIDE
    n_cmp = n_chunk - 1
    n_slc = seq // SLC_BLOCK
    groups = range(NSA_KV_GROUPS)
    i = pl.program_id(1)
    t0 = i * tq
    tpos = t0 + lax.broadcasted_iota(jnp.int32, (1, rows), 1) % tq

    def normalise(acc_t):
        return acc_t[0:HEAD_DIM] * (1.0 / acc_t[HEAD_DIM:HEAD_DIM + 1])

    qgs = [jnp.concatenate(
        [q_ref[0, :, (HEADS_PER_GROUP * g + r) * HEAD_DIM:(HEADS_PER_GROUP * g + r + 1) * HEAD_DIM]
         for r in range(HEADS_PER_GROUP)], axis=0) for g in groups]

    nrow = lax.broadcasted_iota(jnp.int32, (n_chunk, 1), 0)
    cmask = (nrow * CMP_STRIDE + (CMP_BLOCK - 1) <= tpos) & (nrow < n_cmp)
    sees_block = tpos >= CMP_BLOCK - 1
    imps = [None] * NSA_KV_GROUPS

    def run_pipelined(items, lag=2):
        scores = []
        for k in range(len(items) + lag):
            if k < len(items):
                scores.append(items[k][0]())
            if k >= lag:
                items[k - lag][1](scores[k - lag])
                scores[k - lag] = None

    def cmp_consume(g, s_c):
        s_c = jnp.where(cmask, s_c, NEG_INF)
        e_c = jnp.exp2(s_c - jnp.max(s_c, axis=0, keepdims=True))
        p_c = e_c * jnp.where(sees_block, 1.0 / jnp.sum(e_c, axis=0, keepdims=True), 0.0)
        part_ref[0, g] = _dot(cmpt_ref[0, NSA_KV_GROUPS + g], p_c.astype(BF16))
        p_sum = p_c[:, 0:tq]
        for r in range(1, HEADS_PER_GROUP):
            p_sum = p_sum + p_c[:, r * tq:(r + 1) * tq]
        p_hi = p_sum.astype(BF16)
        rest = p_sum - p_hi.astype(F32)
        p_mid = rest.astype(BF16)
        p_lo = (rest - p_mid.astype(F32)).astype(BF16)
        mcs_t = mcs_ref[...]
        imps[g] = _dot(mcs_t, p_hi) + _dot(mcs_t, p_mid) + _dot(mcs_t, p_lo)

    wlen = WINDOW + tq
    w0 = pl.multiple_of(jnp.maximum(t0 - WINDOW, 0), tq)
    rel = tpos - (w0 + lax.broadcasted_iota(jnp.int32, (wlen, 1), 0))
    wmask = lax.bitcast_convert_type(rel, jnp.uint32) < WINDOW

    def win_consume(g, s_w):
        s_w = jnp.where(wmask, s_w, NEG_INF)
        p_w = jnp.exp2(s_w - jnp.max(s_w, axis=0, keepdims=True)).astype(BF16)
        part_ref[1, g] = normalise(_dot(vwin_ref[0, g, :, pl.ds(w0, wlen)], p_w))

    run_pipelined(
        [(functools.partial(_nt_dot, cmp_ref[0, g], qgs[g]), functools.partial(cmp_consume, g)) for g in groups]
        + [(functools.partial(_nt_dot, kwin_ref[0, g, pl.ds(w0, wlen), :], qgs[g]), functools.partial(win_consume, g))
           for g in groups])

    jrow = lax.broadcasted_iota(jnp.int32, (n_slc, tq), 0)
    cur = (t0 + lax.broadcasted_iota(jnp.int32, (n_slc, tq), 1)) // SLC_BLOCK
    forced = (jrow == 0) | (jrow == cur) | (jrow == cur - 1)
    for g in groups:
        cand = jnp.where(forced, -jnp.inf, jnp.where(jrow <= cur, imps[g], -1.0))
        sel_t = forced
        for _ in range(min(N_SELECT, n_slc) - 3):
            best = jnp.max(cand, axis=0, keepdims=True)
            first = jnp.min(jnp.where(cand == best, jrow, n_slc), axis=0, keepdims=True)
            hit = jrow == first
            sel_t = sel_t | hit
            cand = jnp.where(hit, -jnp.inf, cand)
        sel_bias = jnp.where(sel_t, 0.0, NEG_INF).T.astype(BF16)
        lhs_ref[g, :, 0:HEAD_DIM] = qgs[g]
        for r in range(HEADS_PER_GROUP):
            lhs_ref[g, r * tq:(r + 1) * tq, HEAD_DIM:] = sel_bias
        m_ref[g] = jnp.full((1, rows), NEG_INF, F32)
        acc_ref[g] = jnp.zeros((V_ROWS, rows), F32)

    def sel_score(g, k0):
        return _nt_dot(ksel_ref[0, g, pl.ds(k0, NSA_TK), :], lhs_ref[g])

    def sel_consume(g, k0, masked, s_s):
        if masked:
            kpos = k0 + lax.broadcasted_iota(jnp.int32, (NSA_TK, 1), 0)
            s_s = jnp.where(kpos <= tpos, s_s, NEG_INF)
        m_old = m_ref[g]
        m_new = jnp.maximum(m_old, jnp.max(s_s, axis=0, keepdims=True))
        p_s = jnp.exp2(s_s - m_new).astype(BF16)
        acc_ref[g] = jnp.exp2(m_old - m_new) * acc_ref[g] + _dot(vsel_ref[0, g, :, pl.ds(k0, NSA_TK)], p_s)
        m_ref[g] = m_new

    def sweep(first_tile, n_tiles, last_masked=False):
        items = []
        for t in range(n_tiles):
            k0 = pl.multiple_of((first_tile + t) * NSA_TK, NSA_TK)
            masked = last_masked and t == n_tiles - 1
            items += [(functools.partial(sel_score, g, k0), functools.partial(sel_consume, g, k0, masked))
                      for g in groups]
        run_pipelined(items)

    n_full = t0 // NSA_TK

    def quad_body(kq, carry):
        sweep(4 * kq, 4)
        return carry

    lax.fori_loop(0, n_full // 4, quad_body, 0)
    done = (n_full // 4) * 4

    @pl.when(n_full % 4 >= 2)
    def _():
        sweep(done, 2)

    done = (n_full // 2) * 2

    @pl.when(n_full % 2 == 1)
    def _():
        sweep(done, 2, last_masked=True)

    @pl.when(n_full % 2 == 0)
    def _():
        sweep(done, 1, last_masked=True)

    for g in groups:
        o_cmp, o_win, o_slc = part_ref[0, g], part_ref[1, g], normalise(acc_ref[g])
        for r in range(HEADS_PER_GROUP):
            hd = HEADS_PER_GROUP * g + r
            sl = slice(r * tq, (r + 1) * tq)
            mixed_t = (gate_ref[0, 3 * hd:3 * hd + 1, :] * o_cmp[:, sl] + gate_ref[0, 3 * hd + 1:3 * hd + 2, :] * o_slc[:, sl]
                       + gate_ref[0, 3 * hd + 2:3 * hd + 3, :] * o_win[:, sl])
            o_ref[0, :, hd * HEAD_DIM:(hd + 1) * HEAD_DIM] = mixed_t.T


def _nsa_call(q, gates_t, cmp, cmp_t, ksel, vsel_t, kwin, vwin_t, mcs_t):
    b, s, _ = q.shape
    tq = NSA_TQ
    g2 = NSA_KV_GROUPS
    n_chunk = s // CMP_STRIDE
    n_slc = s // SLC_BLOCK
    rows = HEADS_PER_GROUP * tq
    tok = lambda bi, i: (bi, i, 0)
    per_batch = lambda bi, i: (bi, 0, 0, 0)
    return pl.pallas_call(
        functools.partial(_nsa_kernel, seq=s),
        grid=(b, s // tq),
        in_specs=[
            pl.BlockSpec((1, tq, NSA_WIDTH), tok),
            pl.BlockSpec((1, GATE_ROWS, tq), lambda bi, i: (bi, 0, i)),
            _resident((1, 2 * g2, n_chunk, HEAD_DIM), per_batch),
            _resident((1, 2 * g2, HEAD_DIM, n_chunk), per_batch),
            _resident((1, g2, s, HEAD_DIM + n_slc), per_batch),
            _resident((1, g2, V_ROWS, s), per_batch),
            _resident((1, g2, s, HEAD_DIM), per_batch),
            _resident((1, g2, V_ROWS, s), per_batch),
            _resident((n_slc, n_chunk), lambda bi, i: (0, 0)),
        ],
        out_specs=pl.BlockSpec((1, tq, NSA_WIDTH), tok),
        out_shape=jax.ShapeDtypeStruct((b, s, NSA_WIDTH), F32),
        scratch_shapes=[
            pltpu.VMEM((g2, rows, HEAD_DIM + n_slc), BF16),
            pltpu.VMEM((g2, 1, rows), F32),
            pltpu.VMEM((g2, V_ROWS, rows), F32),
            pltpu.VMEM((2, g2, HEAD_DIM, rows), F32),
        ],
        compiler_params=pltpu.CompilerParams(
            dimension_semantics=("arbitrary", "arbitrary"), vmem_limit_bytes=VMEM_LIMIT),
        name="nsa_attention",
    )(q, gates_t, cmp, cmp_t, ksel, vsel_t, kwin, vwin_t, mcs_t)


def _memkv_kernel(mem_ref, gain_ref, w_ref, o_ref):
    mem_n = _rmsnorm(mem_ref[0], gain_ref[...]).astype(BF16)
    o_ref[0, 0] = _dot(mem_n, w_ref[0]).astype(BF16)


def _memkv_call(mem, gain, w_mem):
    depth = w_mem.shape[0]
    b, m, d = mem.shape
    return pl.pallas_call(
        _memkv_kernel,
        grid=(depth, b),
        in_specs=[
            pl.BlockSpec((1, m, d), lambda l, bi: (bi, 0, 0)),
            pl.BlockSpec((1, d), lambda l, bi: (0, 0)),
            pl.BlockSpec((1, d, 2 * MEM_WIDTH), lambda l, bi: (l, 0, 0)),
        ],
        out_specs=pl.BlockSpec((1, 1, m, 2 * MEM_WIDTH), lambda l, bi: (l, bi, 0, 0)),
        out_shape=jax.ShapeDtypeStruct((depth, b, m, 2 * MEM_WIDTH), BF16),
        compiler_params=pltpu.CompilerParams(
            dimension_semantics=("arbitrary", "arbitrary"), vmem_limit_bytes=VMEM_LIMIT),
        name="mem_kv",
    )(mem, gain, w_mem)


_F_AUVZ = 0
_F_BZ = 3 * A_WIDTH
_F_CQZ = _F_BZ + NSA_WIDTH
_F_MERGE = _F_CQZ + 2 * MEM_WIDTH


def _fused_kernel(x_ref, ob_ref, gain_ref, w_ref, lng_ref, lnb_ref, wsp_ref, bsp_ref, mkv_ref,
                  wa_ref, wb_ref, wc_ref, wo_ref, fgain_ref, o_ref, *, final):
    tm, d = x_ref.shape[1], x_ref.shape[2]
    x = x_ref[0]
    h = _rmsnorm(x, gain_ref[...]).astype(BF16)

    def proj(c0, width):
        return _dot(h, w_ref[:, c0:c0 + width])

    def merge_gate(k):
        return jax.nn.sigmoid(proj(_F_MERGE + k * d, d))

    uvz = proj(_F_AUVZ, 3 * A_WIDTH)
    u, v, z = uvz[:, 0:A_WIDTH], uvz[:, A_WIDTH:2 * A_WIDTH], uvz[:, 2 * A_WIDTH:]
    vc = v - jnp.mean(v, axis=-1, keepdims=True)
    vn = (vc * lax.rsqrt(jnp.mean(vc * vc, axis=-1, keepdims=True) + EPS) * lng_ref[...] + lnb_ref[...]).astype(BF16)
    gdim = A_WIDTH // A_GROUPS
    mixed = jnp.concatenate([
        jnp.concatenate([
            _dot(wsp_ref[gi], vn[c * CHUNK:(c + 1) * CHUNK, gi * gdim:(gi + 1) * gdim]) + bsp_ref[gi]
            for gi in range(A_GROUPS)], axis=1)
        for c in range(tm // CHUNK)], axis=0)
    o_a = (u * mixed * _silu(z)).astype(BF16)
    acc = merge_gate(0) * _dot(o_a, wa_ref[...])

    o_b = (ob_ref[0] * _silu(proj(_F_BZ, NSA_WIDTH))).astype(BF16)
    acc = acc + merge_gate(1) * _dot(o_b, wb_ref[...])

    cqz = proj(_F_CQZ, 2 * MEM_WIDTH)
    heads = []
    for hd in range(MEM_HEADS):
        sl = slice(hd * HEAD_DIM, (hd + 1) * HEAD_DIM)
        s_m = _nt_dot(cqz[:, sl].astype(BF16), mkv_ref[0, 0, :, sl]) * ATT_SCALE
        e_m = jnp.exp(s_m - jnp.max(s_m, axis=1, keepdims=True))
        p_m = e_m * (1.0 / jnp.sum(e_m, axis=1, keepdims=True))
        heads.append(_dot(p_m.astype(BF16), mkv_ref[0, 0, :, MEM_WIDTH + hd * HEAD_DIM:MEM_WIDTH + (hd + 1) * HEAD_DIM]))
    o_c = (jnp.concatenate(heads, axis=1) * _silu(cqz[:, MEM_WIDTH:])).astype(BF16)
    acc = acc + merge_gate(2) * _dot(o_c, wc_ref[...])

    x_new = x + _dot(acc.astype(BF16), wo_ref[...])
    if final:
        x_new = _rmsnorm(x_new, fgain_ref[...])
    o_ref[0] = x_new


def _fused_call(x, ob, gain, w_f, lng, lnb, wsp, bsp, mkv, layer, wa, wb, wc, wo, fgain, final):
    b, s, d = x.shape
    tm = FUSE_TM
    m = mkv.shape[2]
    tok = lambda bi, i: (bi, i, 0)
    c2 = lambda bi, i: (0, 0)
    c3 = lambda bi, i: (0, 0, 0)
    return pl.pallas_call(
        functools.partial(_fused_kernel, final=final),
        grid=(b, s // tm),
        in_specs=[
            pl.BlockSpec((1, tm, d), tok),
            pl.BlockSpec((1, tm, NSA_WIDTH), tok),
            pl.BlockSpec((1, d), c2),
            _resident(w_f.shape, c2),
            pl.BlockSpec((1, A_WIDTH), c2),
            pl.BlockSpec((1, A_WIDTH), c2),
            _resident(wsp.shape, c3),
            _resident(bsp.shape, c3),
            _resident((1, 1, m, 2 * MEM_WIDTH), lambda bi, i: (layer, bi, 0, 0)),
            _resident(wa.shape, c2),
            _resident(wb.shape, c2),
            _resident(wc.shape, c2),
            _resident(wo.shape, c2),
            pl.BlockSpec((1, d), c2),
        ],
        out_specs=pl.BlockSpec((1, tm, d), tok),
        out_shape=jax.ShapeDtypeStruct((b, s, d), F32),
        compiler_params=pltpu.CompilerParams(
            dimension_semantics=("arbitrary", "arbitrary"), vmem_limit_bytes=VMEM_LIMIT),
        name="fused_mix",
    )(x, ob, gain, w_f, lng, lnb, wsp, bsp, mkv, wa, wb, wc, wo, fgain)


def _rope_lane_tables(positions):
    half = ROPE_DIM // 2
    inv_freq = ROPE_THETA ** (-jnp.arange(0, ROPE_DIM, 2, dtype=F32) / ROPE_DIM)
    ang = positions.astype(F32)[..., None] * inv_freq
    cos, sin = jnp.cos(ang), jnp.sin(ang)
    zeros = jnp.zeros(cos.shape[:-1] + (HEAD_DIM - ROPE_DIM,), F32)
    cosf = jnp.concatenate([cos, cos, zeros + 1.0], axis=-1)
    sina = jnp.concatenate([-sin, jnp.zeros_like(sin), zeros], axis=-1)
    sinb = jnp.concatenate([jnp.zeros_like(sin), sin, zeros], axis=-1)
    del half
    return cosf, sina, sinb


def _cmp_to_slc(n_chunk, n_slc):
    i = np.arange(n_chunk)[:, None] * CMP_STRIDE
    j = np.arange(n_slc)[None, :] * SLC_BLOCK
    ov = np.clip(np.minimum(i + CMP_BLOCK, j + SLC_BLOCK) - np.maximum(i, j), 0, None) / CMP_BLOCK
    ov[n_chunk - 1] = 0.0
    return jnp.asarray(ov.T, dtype=BF16)


def kernel(x, mem, positions, norm_gain, w_in, ln_v_gain, ln_v_bias, w_spatial, b_spatial, cmp_pe_k, cmp_w1_k, cmp_w2_k, cmp_pe_v, cmp_w1_v, cmp_w2_v, mem_norm_gain, w_mem_kv, w_branch_a, w_branch_b, w_branch_c, w_out, final_norm_gain):
    depth = w_in.shape[0]
    b, s, d = x.shape
    assert s % NSA_TK == 0 and s % PROJ_TM == 0 and s >= WINDOW + NSA_TQ and d == w_in.shape[1]

    cosf, sina, sinb = _rope_lane_tables(positions)
    mcs = _cmp_to_slc(s // CMP_STRIDE, s // SLC_BLOCK)

    gate_w = jnp.pad(w_in[:, :, _B_G:_B_KV], ((0, 0), (0, 0), (0, GATE_PAD - (_B_KV - _B_G))))
    w_p = jnp.concatenate([w_in[:, :, _B_Q:_B_Z], w_in[:, :, _B_KV:_C_Q], gate_w], axis=-1).astype(BF16)
    w_f = jnp.concatenate([w_in[:, :, _A_U:_B_Q], w_in[:, :, _B_Z:_B_G], w_in[:, :, _C_Q:_MERGE],
                           w_in[:, :, _MERGE:]], axis=-1).astype(BF16)
    wsp = (w_spatial * jnp.tril(jnp.ones((CHUNK, CHUNK), w_spatial.dtype))).astype(BF16)
    bsp = jnp.broadcast_to(b_spatial[..., None], b_spatial.shape + (A_WIDTH // A_GROUPS,))
    pe = jnp.stack([cmp_pe_k, cmp_pe_v], axis=1)
    w1 = jnp.stack([cmp_w1_k, cmp_w1_v], axis=1).astype(BF16)
    w2 = jnp.stack([cmp_w2_k, cmp_w2_v], axis=1).astype(BF16)
    wa, wb, wc, wo = (w.astype(BF16) for w in (w_branch_a, w_branch_b, w_branch_c, w_out))

    mkv = _memkv_call(mem, mem_norm_gain[None, :], w_mem_kv.astype(BF16))
    fgain = final_norm_gain[None, :]
    for l in range(depth):
        gain = norm_gain[l][None, :]
        q, gates_t, cmp_in, ksel, vsel_t, kwin, vwin_t = _proj_call(x, gain, w_p[l], cosf, sina, sinb)
        cmp, cmp_t = _compress_call(cmp_in, pe[l], w1[l], w2[l])
        ob = _nsa_call(q, gates_t, cmp, cmp_t, ksel, vsel_t, kwin, vwin_t, mcs)
        x = _fused_call(x, ob, gain, w_f[l], ln_v_gain[l][None, :], ln_v_bias[l][None, :], wsp[l], bsp[l],
                        mkv, l, wa[l], wb[l], wc[l], wo[l], fgain, l == depth - 1)
    return x
```

```python
import functools

import numpy as np
import jax
import jax.numpy as jnp
from jax import lax
from jax.experimental import pallas as pl
from jax.experimental.pallas import tpu as pltpu

F32 = jnp.float32
BF16 = jnp.bfloat16

HEAD_DIM = 128
A_GROUPS = 4
A_WIDTH = 512
CHUNK = 128
NSA_HEADS = 8
NSA_KV_GROUPS = 2
HEADS_PER_GROUP = NSA_HEADS // NSA_KV_GROUPS
NSA_WIDTH = NSA_HEADS * HEAD_DIM
CMP_BLOCK = 32
CMP_STRIDE = 16
SLC_BLOCK = 64
N_SELECT = 16
WINDOW = 512
MEM_HEADS = 4
MEM_WIDTH = MEM_HEADS * HEAD_DIM
ROPE_DIM = HEAD_DIM // 4
ROPE_THETA = 500000.0
EPS = 1e-6
NEG_INF = -1e30
FORCED_SCORE = 1e4
ATT_SCALE = HEAD_DIM ** -0.5
LOG2_E = 1.4426950408889634

_OFF = np.cumsum((0, A_WIDTH, A_WIDTH, A_WIDTH, NSA_WIDTH, NSA_WIDTH, NSA_HEADS * 3, 3 * 2 * NSA_KV_GROUPS * HEAD_DIM,
                  MEM_WIDTH, MEM_WIDTH))
(_A_U, _A_V, _A_Z, _B_Q, _B_Z, _B_G, _B_KV, _C_Q, _C_Z, _MERGE) = (int(v) for v in _OFF)
GATE_PAD = 128
GATE_ROWS = 32
V_ROWS = HEAD_DIM + 16

PROJ_TM = 512
NSA_TQ = 256
NSA_TK = 512
FUSE_TM = 256
VMEM_LIMIT = 56 * 1024 * 1024


def _nt_dot(a, b):
    return lax.dot_general(a, b, (((1,), (1,)), ((), ())), preferred_element_type=F32)


def _dot(a, b):
    return jnp.dot(a, b, preferred_element_type=F32)


def _rmsnorm(x, gain):
    return x * lax.rsqrt(jnp.mean(x * x, axis=-1, keepdims=True) + EPS) * gain


def _silu(x):
    return x * jax.nn.sigmoid(x)


def _resident(block_shape, index_map):
    return pl.BlockSpec(block_shape, index_map, pipeline_mode=pl.Buffered(1))


def _proj_kernel(x_ref, gain_ref, w_ref, cos_ref, sina_ref, sinb_ref,
                 q_ref, gate_ref, cmp_ref, ksel_ref, vsel_ref, kwin_ref, vwin_ref, *, n_slc):
    tm = x_ref.shape[1]
    t0 = pl.program_id(1) * tm
    h = _rmsnorm(x_ref[0], gain_ref[...]).astype(BF16)
    proj = _dot(h, w_ref[...])
    cosf, sina, sinb = cos_ref[0], sina_ref[0], sinb_ref[0]

    def rope(t):
        return t * cosf + pltpu.roll(t, HEAD_DIM - ROPE_DIM // 2, 1) * sina + pltpu.roll(t, ROPE_DIM // 2, 1) * sinb

    def col(j):
        return proj[:, j * HEAD_DIM:(j + 1) * HEAD_DIM]

    for hd in range(NSA_HEADS):
        q_ref[0, :, hd * HEAD_DIM:(hd + 1) * HEAD_DIM] = (rope(col(hd)) * (ATT_SCALE * LOG2_E)).astype(BF16)
    kv0 = NSA_HEADS
    g2 = NSA_KV_GROUPS
    row = t0 + lax.broadcasted_iota(jnp.int32, (tm, n_slc), 0)
    blk = lax.broadcasted_iota(jnp.int32, (tm, n_slc), 1)
    sel_onehot = (row // SLC_BLOCK == blk).astype(BF16)
    ones_row = (lax.broadcasted_iota(jnp.int32, (V_ROWS - HEAD_DIM, tm), 0) == 0).astype(BF16)
    for g in range(g2):
        cmp_ref[0, g] = rope(col(kv0 + g))
        cmp_ref[0, g2 + g] = col(kv0 + g2 + g)
        ksel_ref[0, g, :, 0:HEAD_DIM] = rope(col(kv0 + 2 * g2 + g)).astype(BF16)
        ksel_ref[0, g, :, HEAD_DIM:] = sel_onehot
        kwin_ref[0, g] = rope(col(kv0 + 4 * g2 + g)).astype(BF16)
        for v_ref, j in ((vsel_ref, kv0 + 3 * g2 + g), (vwin_ref, kv0 + 5 * g2 + g)):
            v_ref[0, g, 0:HEAD_DIM, :] = col(j).T.astype(BF16)
            v_ref[0, g, HEAD_DIM:, :] = ones_row
    gate_ref[0] = jax.nn.sigmoid(proj[:, (kv0 + 6 * g2) * HEAD_DIM:]).T[0:GATE_ROWS]


def _proj_call(x, gain, w_p, cosf, sina, sinb):
    b, s, d = x.shape
    tm = PROJ_TM
    n_slc = s // SLC_BLOCK
    g2 = NSA_KV_GROUPS
    wp_cols = w_p.shape[1]
    tok = lambda bi, i: (bi, i, 0)
    grp = lambda bi, i: (bi, 0, i, 0)
    grp_t = lambda bi, i: (bi, 0, 0, i)
    return pl.pallas_call(
        functools.partial(_proj_kernel, n_slc=n_slc),
        grid=(b, s // tm),
        in_specs=[
            pl.BlockSpec((1, tm, d), tok),
            pl.BlockSpec((1, d), lambda bi, i: (0, 0)),
            _resident((d, wp_cols), lambda bi, i: (0, 0)),
            pl.BlockSpec((1, tm, HEAD_DIM), tok),
            pl.BlockSpec((1, tm, HEAD_DIM), tok),
            pl.BlockSpec((1, tm, HEAD_DIM), tok),
        ],
        out_specs=[
            pl.BlockSpec((1, tm, NSA_WIDTH), tok),
            pl.BlockSpec((1, GATE_ROWS, tm), lambda bi, i: (bi, 0, i)),
            pl.BlockSpec((1, 2 * g2, tm, HEAD_DIM), grp),
            pl.BlockSpec((1, g2, tm, HEAD_DIM + n_slc), grp),
            pl.BlockSpec((1, g2, V_ROWS, tm), grp_t),
            pl.BlockSpec((1, g2, tm, HEAD_DIM), grp),
            pl.BlockSpec((1, g2, V_ROWS, tm), grp_t),
        ],
        out_shape=[
            jax.ShapeDtypeStruct((b, s, NSA_WIDTH), BF16),
            jax.ShapeDtypeStruct((b, GATE_ROWS, s), F32),
            jax.ShapeDtypeStruct((b, 2 * g2, s, HEAD_DIM), F32),
            jax.ShapeDtypeStruct((b, g2, s, HEAD_DIM + n_slc), BF16),
            jax.ShapeDtypeStruct((b, g2, V_ROWS, s), BF16),
            jax.ShapeDtypeStruct((b, g2, s, HEAD_DIM), BF16),
            jax.ShapeDtypeStruct((b, g2, V_ROWS, s), BF16),
        ],
        compiler_params=pltpu.CompilerParams(
            dimension_semantics=("arbitrary", "arbitrary"), vmem_limit_bytes=VMEM_LIMIT),
        name="nsa_proj",
    )(x, gain, w_p, cosf, sina, sinb)


def _compress_kernel(x_ref, pe_ref, w1_ref, w2_ref, o_ref, ot_ref):
    n_chunk = o_ref.shape[2]
    first = jnp.zeros((n_chunk, HEAD_DIM), F32)
    second = jnp.zeros((n_chunk, HEAD_DIM), F32)
    for c in range(CMP_STRIDE):
        rows = x_ref[0, 0, pl.ds(c, n_chunk, stride=CMP_STRIDE), :]
        first = first + _dot((rows + pe_ref[0, c:c + 1, :]).astype(BF16), w1_ref[0, c])
        second = second + _dot((rows + pe_ref[0, CMP_STRIDE + c:CMP_STRIDE + c + 1, :]).astype(BF16),
                               w1_ref[0, CMP_STRIDE + c])
    hidden = first + pltpu.roll(second, n_chunk - 1, 0)
    out = _dot(_silu(hidden).astype(BF16), w2_ref[0])
    o_ref[0, 0] = out.astype(BF16)
    ot_ref[0, 0] = out.T.astype(BF16)


def _compress_call(cmp_in, pe, w1, w2):
    b, n4, s, dh = cmp_in.shape
    g2 = NSA_KV_GROUPS
    n_chunk = s // CMP_STRIDE
    return pl.pallas_call(
        _compress_kernel,
        grid=(b, n4),
        in_specs=[
            pl.BlockSpec((1, 1, s, dh), lambda bi, j: (bi, j, 0, 0)),
            pl.BlockSpec((1, CMP_BLOCK, dh), lambda bi, j: (j // g2, 0, 0)),
            pl.BlockSpec((1, CMP_BLOCK, dh, dh), lambda bi, j: (j // g2, 0, 0, 0)),
            pl.BlockSpec((1, dh, dh), lambda bi, j: (j // g2, 0, 0)),
        ],
        out_specs=[pl.BlockSpec((1, 1, n_chunk, dh), lambda bi, j: (bi, j, 0, 0)),
                   pl.BlockSpec((1, 1, dh, n_chunk), lambda bi, j: (bi, j, 0, 0))],
        out_shape=[jax.ShapeDtypeStruct((b, n4, n_chunk, dh), BF16),
                   jax.ShapeDtypeStruct((b, n4, dh, n_chunk), BF16)],
        compiler_params=pltpu.CompilerParams(
            dimension_semantics=("arbitrary", "arbitrary"), vmem_limit_bytes=VMEM_LIMIT),
        name="nsa_compress",
    )(cmp_in, pe, w1, w2)


def _nsa_kernel(q_ref, gate_ref, cmp_ref, cmpt_ref, ksel_ref, vsel_ref, kwin_ref, vwin_ref, mcs_ref, o_ref,
                lhs_ref, m_ref, acc_ref, part_ref, *, seq):
    tq = q_ref.shape[1]
    rows = HEADS_PER_GROUP * tq
    n_chunk = seq // CMP_STRIDE
    n_cmp = n_chunk - 1
    n_slc = seq // SLC_BLOCK
    groups = range(NSA_KV_GROUPS)
    i = pl.program_id(1)
    t0 = i * tq
    tok = t0 + lax.broadcasted_iota(jnp.int32, (1, tq), 1)
    tpos = jnp.concatenate([tok] * HEADS_PER_GROUP, axis=1)

    def mask_heads(s, mask):
        return jnp.concatenate(
            [jnp.where(mask, s[:, r * tq:(r + 1) * tq], NEG_INF) for r in range(HEADS_PER_GROUP)], axis=1)

    def gate_row(g, branch):
        return jnp.concatenate(
            [gate_ref[0, 3 * (HEADS_PER_GROUP * g + r) + branch:3 * (HEADS_PER_GROUP * g + r) + branch + 1, :]
             for r in range(HEADS_PER_GROUP)], axis=1)

    def normalise(acc_t, gate):
        return acc_t[0:HEAD_DIM] * (gate / acc_t[HEAD_DIM:HEAD_DIM + 1])

    qgs = [jnp.concatenate(
        [q_ref[0, :, (HEADS_PER_GROUP * g + r) * HEAD_DIM:(HEADS_PER_GROUP * g + r + 1) * HEAD_DIM]
         for r in range(HEADS_PER_GROUP)], axis=0) for g in groups]

    nrow = lax.broadcasted_iota(jnp.int32, (n_chunk, 1), 0)
    cmask = nrow <= jnp.minimum((tok - (CMP_BLOCK - 1)) // CMP_STRIDE, n_cmp - 1)
    sees_block = tpos >= CMP_BLOCK - 1
    imps = [None] * NSA_KV_GROUPS

    def run_pipelined(items, lag=2):
        scores = []
        for k in range(len(items) + lag):
            if k < len(items):
                scores.append(items[k][0]())
            if k >= lag:
                items[k - lag][1](scores[k - lag])
                scores[k - lag] = None

    def cmp_consume(g, s_c):
        s_c = mask_heads(s_c, cmask)
        e_c = jnp.exp2(s_c - jnp.max(s_c, axis=0, keepdims=True))
        p_c = e_c * jnp.where(sees_block, 1.0 / jnp.sum(e_c, axis=0, keepdims=True), 0.0)
        part_ref[0, g] = _dot(cmpt_ref[0, NSA_KV_GROUPS + g], p_c.astype(BF16)) * gate_row(g, 0)
        p_sum = p_c[:, 0:tq]
        for r in range(1, HEADS_PER_GROUP):
            p_sum = p_sum + p_c[:, r * tq:(r + 1) * tq]
        p_hi = p_sum.astype(BF16)
        rest = p_sum - p_hi.astype(F32)
        p_mid = rest.astype(BF16)
        p_lo = (rest - p_mid.astype(F32)).astype(BF16)
        mcs_t = mcs_ref[...]
        imps[g] = _dot(mcs_t, p_hi) + _dot(mcs_t, p_mid) + _dot(mcs_t, p_lo)

    wlen = WINDOW + tq
    w0 = pl.multiple_of(jnp.maximum(t0 - WINDOW, 0), tq)
    rel = tok - (w0 + lax.broadcasted_iota(jnp.int32, (wlen, 1), 0))
    wmask = lax.bitcast_convert_type(rel, jnp.uint32) < WINDOW

    def win_consume(g, s_w):
        s_w = mask_heads(s_w, wmask)
        p_w = jnp.exp2(s_w - jnp.max(s_w, axis=0, keepdims=True)).astype(BF16)
        part_ref[1, g] = normalise(_dot(vwin_ref[0, g, :, pl.ds(w0, wlen)], p_w), gate_row(g, 2))

    run_pipelined(
        [(functools.partial(_nt_dot, cmp_ref[0, g], qgs[g]), functools.partial(cmp_consume, g)) for g in groups]
        + [(functools.partial(_nt_dot, kwin_ref[0, g, pl.ds(w0, wlen), :], qgs[g]), functools.partial(win_consume, g))
           for g in groups])

    jrow = lax.broadcasted_iota(jnp.int32, (n_slc, tq), 0)
    cur = (t0 + lax.broadcasted_iota(jnp.int32, (n_slc, tq), 1)) // SLC_BLOCK
    forced = (jrow == 0) | (jrow == cur) | (jrow == cur - 1)
    for g in groups:
        cand = jnp.where(forced, -jnp.inf, jnp.where(jrow <= cur, imps[g], -1.0))
        sel_t = forced
        for _ in range(min(N_SELECT, n_slc) - 3):
            best = jnp.max(cand, axis=0, keepdims=True)
            first = jnp.min(jnp.where(cand == best, jrow, n_slc), axis=0, keepdims=True)
            hit = jrow == first
            sel_t = sel_t | hit
            cand = jnp.where(hit, -jnp.inf, cand)
        sel_bias = jnp.where(sel_t, 0.0, NEG_INF).T.astype(BF16)
        lhs_ref[g, :, 0:HEAD_DIM] = qgs[g]
        for r in range(HEADS_PER_GROUP):
            lhs_ref[g, r * tq:(r + 1) * tq, HEAD_DIM:] = sel_bias
        m_ref[g] = jnp.full((1, rows), NEG_INF, F32)
        acc_ref[g] = jnp.zeros((V_ROWS, rows), F32)

    def sel_score(g, k0):
        return _nt_dot(ksel_ref[0, g, pl.ds(k0, NSA_TK), :], lhs_ref[g])

    def sel_consume(g, k0, masked, s_s):
        if masked:
            kpos = k0 + lax.broadcasted_iota(jnp.int32, (NSA_TK, 1), 0)
            s_s = mask_heads(s_s, kpos <= tok)
        m_old = m_ref[g]
        m_new = jnp.maximum(m_old, jnp.max(s_s, axis=0, keepdims=True))
        p_s = jnp.exp2(s_s - m_new).astype(BF16)
        acc_ref[g] = jnp.exp2(m_old - m_new) * acc_ref[g] + _dot(vsel_ref[0, g, :, pl.ds(k0, NSA_TK)], p_s)
        m_ref[g] = m_new

    def sweep(first_tile, n_tiles, last_masked=False):
        items = []
        for t in range(n_tiles):
            k0 = pl.multiple_of((first_tile + t) * NSA_TK, NSA_TK)
            masked = last_masked and t == n_tiles - 1
            items += [(functools.partial(sel_score, g, k0), functools.partial(sel_consume, g, k0, masked))
                      for g in groups]
        run_pipelined(items)

    n_full = t0 // NSA_TK

    def quad_body(kq, carry):
        sweep(4 * kq, 4)
        return carry

    lax.fori_loop(0, n_full // 4, quad_body, 0)
    done = (n_full // 4) * 4

    @pl.when(n_full % 4 >= 2)
    def _():
        sweep(done, 2)

    done = (n_full // 2) * 2

    @pl.when(n_full % 2 == 1)
    def _():
        sweep(done, 2, last_masked=True)

    @pl.when(n_full % 2 == 0)
    def _():
        sweep(done, 1, last_masked=True)

    for g in groups:
        mixed_t = (part_ref[0, g] + normalise(acc_ref[g], gate_row(g, 1))) + part_ref[1, g]
        for r in range(HEADS_PER_GROUP):
            hd = HEADS_PER_GROUP * g + r
            o_ref[0, :, hd * HEAD_DIM:(hd + 1) * HEAD_DIM] = mixed_t[:, r * tq:(r + 1) * tq].T


def _nsa_call(q, gates_t, cmp, cmp_t, ksel, vsel_t, kwin, vwin_t, mcs_t):
    b, s, _ = q.shape
    tq = NSA_TQ
    g2 = NSA_KV_GROUPS
    n_chunk = s // CMP_STRIDE
    n_slc = s // SLC_BLOCK
    rows = HEADS_PER_GROUP * tq
    tok = lambda bi, i: (bi, i, 0)
    per_batch = lambda bi, i: (bi, 0, 0, 0)
    return pl.pallas_call(
        functools.partial(_nsa_kernel, seq=s),
        grid=(b, s // tq),
        in_specs=[
            pl.BlockSpec((1, tq, NSA_WIDTH), tok),
            pl.BlockSpec((1, GATE_ROWS, tq), lambda bi, i: (bi, 0, i)),
            _resident((1, 2 * g2, n_chunk, HEAD_DIM), per_batch),
            _resident((1, 2 * g2, HEAD_DIM, n_chunk), per_batch),
            _resident((1, g2, s, HEAD_DIM + n_slc), per_batch),
            _resident((1, g2, V_ROWS, s), per_batch),
            _resident((1, g2, s, HEAD_DIM), per_batch),
            _resident((1, g2, V_ROWS, s), per_batch),
            _resident((n_slc, n_chunk), lambda bi, i: (0, 0)),
        ],
        out_specs=pl.BlockSpec((1, tq, NSA_WIDTH), tok),
        out_shape=jax.ShapeDtypeStruct((b, s, NSA_WIDTH), F32),
        scratch_shapes=[
            pltpu.VMEM((g2, rows, HEAD_DIM + n_slc), BF16),
            pltpu.VMEM((g2, 1, rows), F32),
            pltpu.VMEM((g2, V_ROWS, rows), F32),
            pltpu.VMEM((2, g2, HEAD_DIM, rows), F32),
        ],
        compiler_params=pltpu.CompilerParams(
            dimension_semantics=("arbitrary", "arbitrary"), vmem_limit_bytes=VMEM_LIMIT),
        name="nsa_attention",
    )(q, gates_t, cmp, cmp_t, ksel, vsel_t, kwin, vwin_t, mcs_t)


def _memkv_kernel(mem_ref, gain_ref, w_ref, o_ref):
    mem_n = _rmsnorm(mem_ref[0], gain_ref[...]).astype(BF16)
    o_ref[0, 0] = _dot(mem_n, w_ref[0]).astype(BF16)


def _memkv_call(mem, gain, w_mem):
    depth = w_mem.shape[0]
    b, m, d = mem.shape
    return pl.pallas_call(
        _memkv_kernel,
        grid=(depth, b),
        in_specs=[
            pl.BlockSpec((1, m, d), lambda l, bi: (bi, 0, 0)),
            pl.BlockSpec((1, d), lambda l, bi: (0, 0)),
            pl.BlockSpec((1, d, 2 * MEM_WIDTH), lambda l, bi: (l, 0, 0)),
        ],
        out_specs=pl.BlockSpec((1, 1, m, 2 * MEM_WIDTH), lambda l, bi: (l, bi, 0, 0)),
        out_shape=jax.ShapeDtypeStruct((depth, b, m, 2 * MEM_WIDTH), BF16),
        compiler_params=pltpu.CompilerParams(
            dimension_semantics=("arbitrary", "arbitrary"), vmem_limit_bytes=VMEM_LIMIT),
        name="mem_kv",
    )(mem, gain, w_mem)


_F_AUVZ = 0
_F_BZ = 3 * A_WIDTH
_F_CQZ = _F_BZ + NSA_WIDTH
_F_MERGE = _F_CQZ + 2 * MEM_WIDTH


def _fused_kernel(x_ref, ob_ref, gain_ref, w_ref, lng_ref, lnb_ref, wsp_ref, bsp_ref, mkv_ref,
                  wa_ref, wb_ref, wc_ref, wo_ref, fgain_ref, o_ref, *, final):
    tm, d = x_ref.shape[1], x_ref.shape[2]
    x = x_ref[0]
    h = _rmsnorm(x, gain_ref[...]).astype(BF16)

    def proj(c0, width):
        return _dot(h, w_ref[:, c0:c0 + width])

    uvz = proj(_F_AUVZ, 3 * A_WIDTH)
    cqz = proj(_F_CQZ, 2 * MEM_WIDTH)
    zb = proj(_F_BZ, NSA_WIDTH)
    merge = [proj(_F_MERGE + k * d, d) for k in range(3)]

    u, v, z = uvz[:, 0:A_WIDTH], uvz[:, A_WIDTH:2 * A_WIDTH], uvz[:, 2 * A_WIDTH:]
    vc = v - jnp.mean(v, axis=-1, keepdims=True)
    vn = (vc * lax.rsqrt(jnp.mean(vc * vc, axis=-1, keepdims=True) + EPS) * lng_ref[...] + lnb_ref[...]).astype(BF16)
    gdim = A_WIDTH // A_GROUPS
    mixed = jnp.concatenate([
        jnp.concatenate([
            _dot(wsp_ref[gi], vn[c * CHUNK:(c + 1) * CHUNK, gi * gdim:(gi + 1) * gdim]) + bsp_ref[gi]
            for gi in range(A_GROUPS)], axis=1)
        for c in range(tm // CHUNK)], axis=0)

    scores = [_nt_dot(cqz[:, hd * HEAD_DIM:(hd + 1) * HEAD_DIM].astype(BF16),
                      mkv_ref[0, 0, :, hd * HEAD_DIM:(hd + 1) * HEAD_DIM]) * ATT_SCALE for hd in range(MEM_HEADS)]

    o_a = (u * mixed * _silu(z)).astype(BF16)
    acc = jax.nn.sigmoid(merge[0]) * _dot(o_a, wa_ref[...])

    o_b = (ob_ref[0] * _silu(zb)).astype(BF16)
    acc = acc + jax.nn.sigmoid(merge[1]) * _dot(o_b, wb_ref[...])

    heads = []
    for hd in range(MEM_HEADS):
        e_m = jnp.exp(scores[hd] - jnp.max(scores[hd], axis=1, keepdims=True))
        p_m = e_m * (1.0 / jnp.sum(e_m, axis=1, keepdims=True))
        heads.append(_dot(p_m.astype(BF16), mkv_ref[0, 0, :, MEM_WIDTH + hd * HEAD_DIM:MEM_WIDTH + (hd + 1) * HEAD_DIM]))
    o_c = (jnp.concatenate(heads, axis=1) * _silu(cqz[:, MEM_WIDTH:])).astype(BF16)
    acc = acc + jax.nn.sigmoid(merge[2]) * _dot(o_c, wc_ref[...])

    x_new = x + _dot(acc.astype(BF16), wo_ref[...])
    if final:
        x_new = _rmsnorm(x_new, fgain_ref[...])
    o_ref[0] = x_new


def _fused_call(x, ob, gain, w_f, lng, lnb, wsp, bsp, mkv, layer, wa, wb, wc, wo, fgain, final):
    b, s, d = x.shape
    tm = FUSE_TM
    m = mkv.shape[2]
    tok = lambda bi, i: (bi, i, 0)
    c2 = lambda bi, i: (0, 0)
    c3 = lambda bi, i: (0, 0, 0)
    return pl.pallas_call(
        functools.partial(_fused_kernel, final=final),
        grid=(b, s // tm),
        in_specs=[
            pl.BlockSpec((1, tm, d), tok),
            pl.BlockSpec((1, tm, NSA_WIDTH), tok),
            pl.BlockSpec((1, d), c2),
            _resident(w_f.shape, c2),
            pl.BlockSpec((1, A_WIDTH), c2),
            pl.BlockSpec((1, A_WIDTH), c2),
            _resident(wsp.shape, c3),
            _resident(bsp.shape, c3),
            _resident((1, 1, m, 2 * MEM_WIDTH), lambda bi, i: (layer, bi, 0, 0)),
            _resident(wa.shape, c2),
            _resident(wb.shape, c2),
            _resident(wc.shape, c2),
            _resident(wo.shape, c2),
            pl.BlockSpec((1, d), c2),
        ],
        out_specs=pl.BlockSpec((1, tm, d), tok),
        out_shape=jax.ShapeDtypeStruct((b, s, d), F32),
        compiler_params=pltpu.CompilerParams(
            dimension_semantics=("arbitrary", "arbitrary"), vmem_limit_bytes=VMEM_LIMIT),
        name="fused_mix",
    )(x, ob, gain, w_f, lng, lnb, wsp, bsp, mkv, wa, wb, wc, wo, fgain)


def _rope_lane_tables(positions):
    half = ROPE_DIM // 2
    inv_freq = ROPE_THETA ** (-jnp.arange(0, ROPE_DIM, 2, dtype=F32) / ROPE_DIM)
    ang = positions.astype(F32)[..., None] * inv_freq
    cos, sin = jnp.cos(ang), jnp.sin(ang)
    zeros = jnp.zeros(cos.shape[:-1] + (HEAD_DIM - ROPE_DIM,), F32)
    cosf = jnp.concatenate([cos, cos, zeros + 1.0], axis=-1)
    sina = jnp.concatenate([-sin, jnp.zeros_like(sin), zeros], axis=-1)
    sinb = jnp.concatenate([jnp.zeros_like(sin), sin, zeros], axis=-1)
    del half
    return cosf, sina, sinb


def _cmp_to_slc(n_chunk, n_slc):
    i = np.arange(n_chunk)[:, None] * CMP_STRIDE
    j = np.arange(n_slc)[None, :] * SLC_BLOCK
    ov = np.clip(np.minimum(i + CMP_BLOCK, j + SLC_BLOCK) - np.maximum(i, j), 0, None) / CMP_BLOCK
    ov[n_chunk - 1] = 0.0
    return jnp.asarray(ov.T, dtype=BF16)


def kernel(x, mem, positions, norm_gain, w_in, ln_v_gain, ln_v_bias, w_spatial, b_spatial, cmp_pe_k, cmp_w1_k, cmp_w2_k, cmp_pe_v, cmp_w1_v, cmp_w2_v, mem_norm_gain, w_mem_kv, w_branch_a, w_branch_b, w_branch_c, w_out, final_norm_gain):
    depth = w_in.shape[0]
    b, s, d = x.shape
    assert s % NSA_TK == 0 and s % PROJ_TM == 0 and s >= WINDOW + NSA_TQ and d == w_in.shape[1]

    cosf, sina, sinb = _rope_lane_tables(positions)
    mcs = _cmp_to_slc(s // CMP_STRIDE, s // SLC_BLOCK)

    gate_w = jnp.pad(w_in[:, :, _B_G:_B_KV], ((0, 0), (0, 0), (0, GATE_PAD - (_B_KV - _B_G))))
    w_p = jnp.concatenate([w_in[:, :, _B_Q:_B_Z], w_in[:, :, _B_KV:_C_Q], gate_w], axis=-1).astype(BF16)
    w_f = jnp.concatenate([w_in[:, :, _A_U:_B_Q], w_in[:, :, _B_Z:_B_G], w_in[:, :, _C_Q:_MERGE],
                           w_in[:, :, _MERGE:]], axis=-1).astype(BF16)
    wsp = (w_spatial * jnp.tril(jnp.ones((CHUNK, CHUNK), w_spatial.dtype))).astype(BF16)
    bsp = jnp.broadcast_to(b_spatial[..., None], b_spatial.shape + (A_WIDTH // A_GROUPS,))
    pe = jnp.stack([cmp_pe_k, cmp_pe_v], axis=1)
    w1 = jnp.stack([cmp_w1_k, cmp_w1_v], axis=1).astype(BF16)
    w2 = jnp.stack([cmp_w2_k, cmp_w2_v], axis=1).astype(BF16)
    wa, wb, wc, wo = (w.astype(BF16) for w in (w_branch_a, w_branch_b, w_branch_c, w_out))

    mkv = _memkv_call(mem, mem_norm_gain[None, :], w_mem_kv.astype(BF16))
    fgain = final_norm_gain[None, :]
    for l in range(depth):
        gain = norm_gain[l][None, :]
        q, gates_t, cmp_in, ksel, vsel_t, kwin, vwin_t = _proj_call(x, gain, w_p[l], cosf, sina, sinb)
        cmp, cmp_t = _compress_call(cmp_in, pe[l], w1[l], w2[l])
        ob = _nsa_call(q, gates_t, cmp, cmp_t, ksel, vsel_t, kwin, vwin_t, mcs)
        x = _fused_call(x, ob, gain, w_f[l], ln_v_gain[l][None, :], ln_v_bias[l][None, :], wsp[l], bsp[l],
                        mkv, l, wa[l], wb[l], wc[l], wo[l], fgain, l == depth - 1)
    return x
```

```python
import functools

import numpy as np
import jax
import jax.numpy as jnp
from jax import lax
from jax.experimental import pallas as pl
from jax.experimental.pallas import tpu as pltpu

F32 = jnp.float32
BF16 = jnp.bfloat16

HEAD_DIM = 128
A_GROUPS = 4
A_WIDTH = 512
CHUNK = 128
NSA_HEADS = 8
NSA_KV_GROUPS = 2
HEADS_PER_GROUP = NSA_HEADS // NSA_KV_GROUPS
NSA_WIDTH = NSA_HEADS * HEAD_DIM
CMP_BLOCK = 32
CMP_STRIDE = 16
SLC_BLOCK = 64
N_SELECT = 16
WINDOW = 512
MEM_HEADS = 4
MEM_WIDTH = MEM_HEADS * HEAD_DIM
ROPE_DIM = HEAD_DIM // 4
ROPE_THETA = 500000.0
EPS = 1e-6
NEG_INF = -1e30
FORCED_SCORE = 1e4
ATT_SCALE = HEAD_DIM ** -0.5
LOG2_E = 1.4426950408889634

_OFF = np.cumsum((0, A_WIDTH, A_WIDTH, A_WIDTH, NSA_WIDTH, NSA_WIDTH, NSA_HEADS * 3, 3 * 2 * NSA_KV_GROUPS * HEAD_DIM,
                  MEM_WIDTH, MEM_WIDTH))
(_A_U, _A_V, _A_Z, _B_Q, _B_Z, _B_G, _B_KV, _C_Q, _C_Z, _MERGE) = (int(v) for v in _OFF)
GATE_PAD = 128
GATE_ROWS = 32
V_ROWS = HEAD_DIM + 16

PROJ_TM = 512
NSA_TQ = 256
NSA_KC = 256
FUSE_TM = 256
VMEM_LIMIT = 56 * 1024 * 1024


def _nt_dot(a, b):
    return lax.dot_general(a, b, (((1,), (1,)), ((), ())), preferred_element_type=F32)


def _dot(a, b):
    return jnp.dot(a, b, preferred_element_type=F32)


def _rmsnorm(x, gain):
    return x * lax.rsqrt(jnp.mean(x * x, axis=-1, keepdims=True) + EPS) * gain


def _silu(x):
    return x * jax.nn.sigmoid(x)


def _resident(block_shape, index_map):
    return pl.BlockSpec(block_shape, index_map, pipeline_mode=pl.Buffered(1))


def _proj_kernel(x_ref, gain_ref, w_ref, cos_ref, sina_ref, sinb_ref,
                 q_ref, gate_ref, cmp_ref, ksel_ref, vsel_ref, kwin_ref, vwin_ref, *, n_slc):
    tm = x_ref.shape[1]
    t0 = pl.program_id(1) * tm
    h = _rmsnorm(x_ref[0], gain_ref[...]).astype(BF16)
    proj = _dot(h, w_ref[...])
    cosf, sina, sinb = cos_ref[0], sina_ref[0], sinb_ref[0]

    def rope(t):
        return t * cosf + pltpu.roll(t, HEAD_DIM - ROPE_DIM // 2, 1) * sina + pltpu.roll(t, ROPE_DIM // 2, 1) * sinb

    def col(j):
        return proj[:, j * HEAD_DIM:(j + 1) * HEAD_DIM]

    for hd in range(NSA_HEADS):
        q_ref[0, hd] = (rope(col(hd)) * (ATT_SCALE * LOG2_E)).T.astype(BF16)
    kv0 = NSA_HEADS
    g2 = NSA_KV_GROUPS
    row = t0 + lax.broadcasted_iota(jnp.int32, (tm, n_slc), 0)
    blk = lax.broadcasted_iota(jnp.int32, (tm, n_slc), 1)
    sel_onehot = (row // SLC_BLOCK == blk).astype(BF16)
    ones_row = (lax.broadcasted_iota(jnp.int32, (V_ROWS - HEAD_DIM, tm), 0) == 0).astype(BF16)
    for g in range(g2):
        cmp_ref[0, g] = rope(col(kv0 + g))
        cmp_ref[0, g2 + g] = col(kv0 + g2 + g)
        ksel_ref[0, g, :, 0:HEAD_DIM] = rope(col(kv0 + 2 * g2 + g)).astype(BF16)
        ksel_ref[0, g, :, HEAD_DIM:] = sel_onehot
        kwin_ref[0, g] = rope(col(kv0 + 4 * g2 + g)).astype(BF16)
        for v_ref, j in ((vsel_ref, kv0 + 3 * g2 + g), (vwin_ref, kv0 + 5 * g2 + g)):
            v_ref[0, g, 0:HEAD_DIM, :] = col(j).T.astype(BF16)
            v_ref[0, g, HEAD_DIM:, :] = ones_row
    gate_ref[0] = jax.nn.sigmoid(proj[:, (kv0 + 6 * g2) * HEAD_DIM:]).T[0:GATE_ROWS]


def _proj_call(x, gain, w_p, cosf, sina, sinb):
    b, s, d = x.shape
    tm = PROJ_TM
    n_slc = s // SLC_BLOCK
    g2 = NSA_KV_GROUPS
    wp_cols = w_p.shape[1]
    tok = lambda bi, i: (bi, i, 0)
    grp = lambda bi, i: (bi, 0, i, 0)
    grp_t = lambda bi, i: (bi, 0, 0, i)
    return pl.pallas_call(
        functools.partial(_proj_kernel, n_slc=n_slc),
        grid=(b, s // tm),
        in_specs=[
            pl.BlockSpec((1, tm, d), tok),
            pl.BlockSpec((1, d), lambda bi, i: (0, 0)),
            _resident((d, wp_cols), lambda bi, i: (0, 0)),
            pl.BlockSpec((1, tm, HEAD_DIM), tok),
            pl.BlockSpec((1, tm, HEAD_DIM), tok),
            pl.BlockSpec((1, tm, HEAD_DIM), tok),
        ],
        out_specs=[
            pl.BlockSpec((1, NSA_HEADS, HEAD_DIM, tm), grp_t),
            pl.BlockSpec((1, GATE_ROWS, tm), lambda bi, i: (bi, 0, i)),
            pl.BlockSpec((1, 2 * g2, tm, HEAD_DIM), grp),
            pl.BlockSpec((1, g2, tm, HEAD_DIM + n_slc), grp),
            pl.BlockSpec((1, g2, V_ROWS, tm), grp_t),
            pl.BlockSpec((1, g2, tm, HEAD_DIM), grp),
            pl.BlockSpec((1, g2, V_ROWS, tm), grp_t),
        ],
        out_shape=[
            jax.ShapeDtypeStruct((b, NSA_HEADS, HEAD_DIM, s), BF16),
            jax.ShapeDtypeStruct((b, GATE_ROWS, s), F32),
            jax.ShapeDtypeStruct((b, 2 * g2, s, HEAD_DIM), F32),
            jax.ShapeDtypeStruct((b, g2, s, HEAD_DIM + n_slc), BF16),
            jax.ShapeDtypeStruct((b, g2, V_ROWS, s), BF16),
            jax.ShapeDtypeStruct((b, g2, s, HEAD_DIM), BF16),
            jax.ShapeDtypeStruct((b, g2, V_ROWS, s), BF16),
        ],
        compiler_params=pltpu.CompilerParams(
            dimension_semantics=("arbitrary", "arbitrary"), vmem_limit_bytes=VMEM_LIMIT),
        name="nsa_proj",
    )(x, gain, w_p, cosf, sina, sinb)


def _compress_kernel(x_ref, pe_ref, w1_ref, w2_ref, o_ref, ot_ref):
    n_chunk = o_ref.shape[2]
    first = jnp.zeros((n_chunk, HEAD_DIM), F32)
    second = jnp.zeros((n_chunk, HEAD_DIM), F32)
    for c in range(CMP_STRIDE):
        rows = x_ref[0, 0, pl.ds(c, n_chunk, stride=CMP_STRIDE), :]
        first = first + _dot((rows + pe_ref[0, c:c + 1, :]).astype(BF16), w1_ref[0, c])
        second = second + _dot((rows + pe_ref[0, CMP_STRIDE + c:CMP_STRIDE + c + 1, :]).astype(BF16),
                               w1_ref[0, CMP_STRIDE + c])
    hidden = first + pltpu.roll(second, n_chunk - 1, 0)
    out = _dot(_silu(hidden).astype(BF16), w2_ref[0])
    o_ref[0, 0] = out.astype(BF16)
    ot_ref[0, 0] = out.T.astype(BF16)


def _compress_call(cmp_in, pe, w1, w2):
    b, n4, s, dh = cmp_in.shape
    g2 = NSA_KV_GROUPS
    n_chunk = s // CMP_STRIDE
    return pl.pallas_call(
        _compress_kernel,
        grid=(b, n4),
        in_specs=[
            pl.BlockSpec((1, 1, s, dh), lambda bi, j: (bi, j, 0, 0)),
            pl.BlockSpec((1, CMP_BLOCK, dh), lambda bi, j: (j // g2, 0, 0)),
            pl.BlockSpec((1, CMP_BLOCK, dh, dh), lambda bi, j: (j // g2, 0, 0, 0)),
            pl.BlockSpec((1, dh, dh), lambda bi, j: (j // g2, 0, 0)),
        ],
        out_specs=[pl.BlockSpec((1, 1, n_chunk, dh), lambda bi, j: (bi, j, 0, 0)),
                   pl.BlockSpec((1, 1, dh, n_chunk), lambda bi, j: (bi, j, 0, 0))],
        out_shape=[jax.ShapeDtypeStruct((b, n4, n_chunk, dh), BF16),
                   jax.ShapeDtypeStruct((b, n4, dh, n_chunk), BF16)],
        compiler_params=pltpu.CompilerParams(
            dimension_semantics=("arbitrary", "arbitrary"), vmem_limit_bytes=VMEM_LIMIT),
        name="nsa_compress",
    )(cmp_in, pe, w1, w2)


def _nsa_kernel(q_ref, gate_ref, cmp_ref, cmpt_ref, ksel_ref, vsel_ref, kwin_ref, vwin_ref, mcs_ref, o_ref,
                lhs_ref, m_ref, acc_ref, part_ref, *, seq):
    tq = q_ref.shape[3]
    assert tq == NSA_KC
    rows = HEADS_PER_GROUP * tq
    half = rows // 2
    n_chunk = seq // CMP_STRIDE
    n_cmp = n_chunk - 1
    n_slc = seq // SLC_BLOCK
    groups = range(NSA_KV_GROUPS)
    i = pl.program_id(1)
    t0 = i * tq
    tok = t0 + lax.broadcasted_iota(jnp.int32, (1, tq), 1)
    tpos = jnp.concatenate([tok] * HEADS_PER_GROUP, axis=1)

    def mask_heads(s, mask):
        return jnp.concatenate(
            [jnp.where(mask, s[:, r * tq:(r + 1) * tq], NEG_INF) for r in range(s.shape[1] // tq)], axis=1)

    def gate_row(g, branch):
        return jnp.concatenate(
            [gate_ref[0, 3 * (HEADS_PER_GROUP * g + r) + branch:3 * (HEADS_PER_GROUP * g + r) + branch + 1, :]
             for r in range(HEADS_PER_GROUP)], axis=1)

    def normalise(acc_t, gate):
        return acc_t[0:HEAD_DIM] * (gate / acc_t[HEAD_DIM:HEAD_DIM + 1])

    qgs = [jnp.concatenate([q_ref[0, HEADS_PER_GROUP * g + r] for r in range(HEADS_PER_GROUP)], axis=1)
           for g in groups]

    nrow = lax.broadcasted_iota(jnp.int32, (n_chunk, 1), 0)
    cmask = nrow <= jnp.minimum((tok - (CMP_BLOCK - 1)) // CMP_STRIDE, n_cmp - 1)
    sees_block = tpos >= CMP_BLOCK - 1
    imps = [None] * NSA_KV_GROUPS

    def run_pipelined(items, lag=2):
        scores = []
        for k in range(len(items) + lag):
            if k < len(items):
                scores.append(items[k][0]())
            if k >= lag:
                items[k - lag][1](scores[k - lag])
                scores[k - lag] = None

    def cmp_consume(g, s_c):
        s_c = mask_heads(s_c, cmask)
        e_c = jnp.exp2(s_c - jnp.max(s_c, axis=0, keepdims=True))
        p_c = e_c * jnp.where(sees_block, 1.0 / jnp.sum(e_c, axis=0, keepdims=True), 0.0)
        part_ref[0, g] = _dot(cmpt_ref[0, NSA_KV_GROUPS + g], p_c.astype(BF16)) * gate_row(g, 0)
        p_sum = p_c[:, 0:tq]
        for r in range(1, HEADS_PER_GROUP):
            p_sum = p_sum + p_c[:, r * tq:(r + 1) * tq]
        p_hi = p_sum.astype(BF16)
        rest = p_sum - p_hi.astype(F32)
        p_mid = rest.astype(BF16)
        p_lo = (rest - p_mid.astype(F32)).astype(BF16)
        mcs_t = mcs_ref[...]
        imps[g] = _dot(mcs_t, p_hi) + _dot(mcs_t, p_mid) + _dot(mcs_t, p_lo)

    wlen = WINDOW + tq
    w0 = pl.multiple_of(jnp.maximum(t0 - WINDOW, 0), tq)
    rel = tok - (w0 + lax.broadcasted_iota(jnp.int32, (wlen, 1), 0))
    wmask = lax.bitcast_convert_type(rel, jnp.uint32) < WINDOW

    def win_consume(g, s_w):
        s_w = mask_heads(s_w, wmask)
        p_w = jnp.exp2(s_w - jnp.max(s_w, axis=0, keepdims=True)).astype(BF16)
        part_ref[1, g] = normalise(_dot(vwin_ref[0, g, :, pl.ds(w0, wlen)], p_w), gate_row(g, 2))

    run_pipelined(
        [(functools.partial(_dot, cmp_ref[0, g], qgs[g]), functools.partial(cmp_consume, g)) for g in groups]
        + [(functools.partial(_dot, kwin_ref[0, g, pl.ds(w0, wlen), :], qgs[g]), functools.partial(win_consume, g))
           for g in groups])

    jrow = lax.broadcasted_iota(jnp.int32, (n_slc, tq), 0)
    cur = (t0 + lax.broadcasted_iota(jnp.int32, (n_slc, tq), 1)) // SLC_BLOCK
    forced = (jrow == 0) | (jrow == cur) | (jrow == cur - 1)
    for g in groups:
        cand = jnp.where(forced, -jnp.inf, jnp.where(jrow <= cur, imps[g], -1.0))
        sel_t = forced
        for _ in range(min(N_SELECT, n_slc) - 3):
            best = jnp.max(cand, axis=0, keepdims=True)
            first = jnp.min(jnp.where(cand == best, jrow, n_slc), axis=0, keepdims=True)
            hit = jrow == first
            sel_t = sel_t | hit
            cand = jnp.where(hit, -jnp.inf, cand)
        sel_bias = jnp.where(sel_t, 0.0, NEG_INF).astype(BF16)
        lhs_ref[g, 0:HEAD_DIM, :] = qgs[g]
        lhs_ref[g, HEAD_DIM:, :] = jnp.concatenate([sel_bias] * HEADS_PER_GROUP, axis=1)
        m_ref[g] = jnp.full((1, rows), NEG_INF, F32)
        acc_ref[g] = jnp.zeros((V_ROWS, rows), F32)

    def sel_score(g, hf, k0):
        return _dot(ksel_ref[0, g, pl.ds(k0, NSA_KC), :], lhs_ref[g, :, hf * half:(hf + 1) * half])

    def sel_consume(g, hf, k0, masked, s_s):
        lanes = slice(hf * half, (hf + 1) * half)
        if masked:
            kpos = k0 + lax.broadcasted_iota(jnp.int32, (NSA_KC, 1), 0)
            s_s = mask_heads(s_s, kpos <= tok)
        m_old = m_ref[g, :, lanes]
        m_new = jnp.maximum(m_old, jnp.max(s_s, axis=0, keepdims=True))
        p_s = jnp.exp2(s_s - m_new).astype(BF16)
        acc_ref[g, :, lanes] = (jnp.exp2(m_old - m_new) * acc_ref[g, :, lanes]
                                + _dot(vsel_ref[0, g, :, pl.ds(k0, NSA_KC)], p_s))
        m_ref[g, :, lanes] = m_new

    def sweep(first_chunk, n_chunks, last_masked=False):
        items = []
        for c in range(n_chunks):
            k0 = pl.multiple_of((first_chunk + c) * NSA_KC, NSA_KC)
            masked = last_masked and c == n_chunks - 1
            items += [(functools.partial(sel_score, g, hf, k0), functools.partial(sel_consume, g, hf, k0, masked))
                      for g in groups for hf in range(2)]
        run_pipelined(items)

    def octet_body(ko, carry):
        sweep(8 * ko, 8)
        return carry

    lax.fori_loop(0, i // 8, octet_body, 0)
    done = (i // 8) * 8

    @pl.when(i % 8 >= 4)
    def _():
        sweep(done, 4)

    done = (i // 4) * 4

    @pl.when(i % 4 >= 2)
    def _():
        sweep(done, 2)

    done = (i // 2) * 2

    @pl.when(i % 2 == 1)
    def _():
        sweep(done, 2, last_masked=True)

    @pl.when(i % 2 == 0)
    def _():
        sweep(done, 1, last_masked=True)

    for g in groups:
        mixed_t = (part_ref[0, g] + normalise(acc_ref[g], gate_row(g, 1))) + part_ref[1, g]
        for r in range(HEADS_PER_GROUP):
            hd = HEADS_PER_GROUP * g + r
            o_ref[0, :, hd * HEAD_DIM:(hd + 1) * HEAD_DIM] = mixed_t[:, r * tq:(r + 1) * tq].T


def _nsa_call(q, gates_t, cmp, cmp_t, ksel, vsel_t, kwin, vwin_t, mcs_t):
    b, _, _, s = q.shape
    tq = NSA_TQ
    g2 = NSA_KV_GROUPS
    n_chunk = s // CMP_STRIDE
    n_slc = s // SLC_BLOCK
    rows = HEADS_PER_GROUP * tq
    tok = lambda bi, i: (bi, i, 0)
    per_batch = lambda bi, i: (bi, 0, 0, 0)
    return pl.pallas_call(
        functools.partial(_nsa_kernel, seq=s),
        grid=(b, s // tq),
        in_specs=[
            pl.BlockSpec((1, NSA_HEADS, HEAD_DIM, tq), lambda bi, i: (bi, 0, 0, i)),
            pl.BlockSpec((1, GATE_ROWS, tq), lambda bi, i: (bi, 0, i)),
            _resident((1, 2 * g2, n_chunk, HEAD_DIM), per_batch),
            _resident((1, 2 * g2, HEAD_DIM, n_chunk), per_batch),
            _resident((1, g2, s, HEAD_DIM + n_slc), per_batch),
            _resident((1, g2, V_ROWS, s), per_batch),
            _resident((1, g2, s, HEAD_DIM), per_batch),
            _resident((1, g2, V_ROWS, s), per_batch),
            _resident((n_slc, n_chunk), lambda bi, i: (0, 0)),
        ],
        out_specs=pl.BlockSpec((1, tq, NSA_WIDTH), tok),
        out_shape=jax.ShapeDtypeStruct((b, s, NSA_WIDTH), F32),
        scratch_shapes=[
            pltpu.VMEM((g2, HEAD_DIM + n_slc, rows), BF16),
            pltpu.VMEM((g2, 1, rows), F32),
            pltpu.VMEM((g2, V_ROWS, rows), F32),
            pltpu.VMEM((2, g2, HEAD_DIM, rows), F32),
        ],
        compiler_params=pltpu.CompilerParams(
            dimension_semantics=("arbitrary", "arbitrary"), vmem_limit_bytes=VMEM_LIMIT),
        name="nsa_attention",
    )(q, gates_t, cmp, cmp_t, ksel, vsel_t, kwin, vwin_t, mcs_t)


def _memkv_kernel(mem_ref, gain_ref, w_ref, o_ref):
    mem_n = _rmsnorm(mem_ref[0], gain_ref[...]).astype(BF16)
    o_ref[0, 0] = _dot(mem_n, w_ref[0]).astype(BF16)


def _memkv_call(mem, gain, w_mem):
    depth = w_mem.shape[0]
    b, m, d = mem.shape
    return pl.pallas_call(
        _memkv_kernel,
        grid=(depth, b),
        in_specs=[
            pl.BlockSpec((1, m, d), lambda l, bi: (bi, 0, 0)),
            pl.BlockSpec((1, d), lambda l, bi: (0, 0)),
            pl.BlockSpec((1, d, 2 * MEM_WIDTH), lambda l, bi: (l, 0, 0)),
        ],
        out_specs=pl.BlockSpec((1, 1, m, 2 * MEM_WIDTH), lambda l, bi: (l, bi, 0, 0)),
        out_shape=jax.ShapeDtypeStruct((depth, b, m, 2 * MEM_WIDTH), BF16),
        compiler_params=pltpu.CompilerParams(
            dimension_semantics=("arbitrary", "arbitrary"), vmem_limit_bytes=VMEM_LIMIT),
        name="mem_kv",
    )(mem, gain, w_mem)


_F_AUVZ = 0
_F_BZ = 3 * A_WIDTH
_F_CQZ = _F_BZ + NSA_WIDTH
_F_MERGE = _F_CQZ + 2 * MEM_WIDTH


def _fused_kernel(x_ref, ob_ref, gain_ref, w_ref, lng_ref, lnb_ref, wsp_ref, bsp_ref, mkv_ref,
                  wa_ref, wb_ref, wc_ref, wo_ref, fgain_ref, o_ref, *, final):
    tm, d = x_ref.shape[1], x_ref.shape[2]
    x = x_ref[0]
    h = _rmsnorm(x, gain_ref[...]).astype(BF16)

    def proj(c0, width):
        return _dot(h, w_ref[:, c0:c0 + width])

    uvz = proj(_F_AUVZ, 3 * A_WIDTH)
    cqz = proj(_F_CQZ, 2 * MEM_WIDTH)
    zb = proj(_F_BZ, NSA_WIDTH)
    merge = [proj(_F_MERGE + k * d, d) for k in range(3)]

    u, v, z = uvz[:, 0:A_WIDTH], uvz[:, A_WIDTH:2 * A_WIDTH], uvz[:, 2 * A_WIDTH:]
    vc = v - jnp.mean(v, axis=-1, keepdims=True)
    vn = (vc * lax.rsqrt(jnp.mean(vc * vc, axis=-1, keepdims=True) + EPS) * lng_ref[...] + lnb_ref[...]).astype(BF16)
    gdim = A_WIDTH // A_GROUPS
    mixed = jnp.concatenate([
        jnp.concatenate([
            _dot(wsp_ref[gi], vn[c * CHUNK:(c + 1) * CHUNK, gi * gdim:(gi + 1) * gdim]) + bsp_ref[gi]
            for gi in range(A_GROUPS)], axis=1)
        for c in range(tm // CHUNK)], axis=0)

    scores = [_nt_dot(cqz[:, hd * HEAD_DIM:(hd + 1) * HEAD_DIM].astype(BF16),
                      mkv_ref[0, 0, :, hd * HEAD_DIM:(hd + 1) * HEAD_DIM]) * ATT_SCALE for hd in range(MEM_HEADS)]

    o_a = (u * mixed * _silu(z)).astype(BF16)
    acc = jax.nn.sigmoid(merge[0]) * _dot(o_a, wa_ref[...])

    o_b = (ob_ref[0] * _silu(zb)).astype(BF16)
    acc = acc + jax.nn.sigmoid(merge[1]) * _dot(o_b, wb_ref[...])

    heads = []
    for hd in range(MEM_HEADS):
        e_m = jnp.exp(scores[hd] - jnp.max(scores[hd], axis=1, keepdims=True))
        p_m = e_m * (1.0 / jnp.sum(e_m, axis=1, keepdims=True))
        heads.append(_dot(p_m.astype(BF16), mkv_ref[0, 0, :, MEM_WIDTH + hd * HEAD_DIM:MEM_WIDTH + (hd + 1) * HEAD_DIM]))
    o_c = (jnp.concatenate(heads, axis=1) * _silu(cqz[:, MEM_WIDTH:])).astype(BF16)
    acc = acc + jax.nn.sigmoid(merge[2]) * _dot(o_c, wc_ref[...])

    x_new = x + _dot(acc.astype(BF16), wo_ref[...])
    if final:
        x_new = _rmsnorm(x_new, fgain_ref[...])
    o_ref[0] = x_new


def _fused_call(x, ob, gain, w_f, lng, lnb, wsp, bsp, mkv, layer, wa, wb, wc, wo, fgain, final):
    b, s, d = x.shape
    tm = FUSE_TM
    m = mkv.shape[2]
    tok = lambda bi, i: (bi, i, 0)
    c2 = lambda bi, i: (0, 0)
    c3 = lambda bi, i: (0, 0, 0)
    return pl.pallas_call(
        functools.partial(_fused_kernel, final=final),
        grid=(b, s // tm),
        in_specs=[
            pl.BlockSpec((1, tm, d), tok),
            pl.BlockSpec((1, tm, NSA_WIDTH), tok),
            pl.BlockSpec((1, d), c2),
            _resident(w_f.shape, c2),
            pl.BlockSpec((1, A_WIDTH), c2),
            pl.BlockSpec((1, A_WIDTH), c2),
            _resident(wsp.shape, c3),
            _resident(bsp.shape, c3),
            _resident((1, 1, m, 2 * MEM_WIDTH), lambda bi, i: (layer, bi, 0, 0)),
            _resident(wa.shape, c2),
            _resident(wb.shape, c2),
            _resident(wc.shape, c2),
            _resident(wo.shape, c2),
            pl.BlockSpec((1, d), c2),
        ],
        out_specs=pl.BlockSpec((1, tm, d), tok),
        out_shape=jax.ShapeDtypeStruct((b, s, d), F32),
        compiler_params=pltpu.CompilerParams(
            dimension_semantics=("arbitrary", "arbitrary"), vmem_limit_bytes=VMEM_LIMIT),
        name="fused_mix",
    )(x, ob, gain, w_f, lng, lnb, wsp, bsp, mkv, wa, wb, wc, wo, fgain)


def _rope_lane_tables(positions):
    half = ROPE_DIM // 2
    inv_freq = ROPE_THETA ** (-jnp.arange(0, ROPE_DIM, 2, dtype=F32) / ROPE_DIM)
    ang = positions.astype(F32)[..., None] * inv_freq
    cos, sin = jnp.cos(ang), jnp.sin(ang)
    zeros = jnp.zeros(cos.shape[:-1] + (HEAD_DIM - ROPE_DIM,), F32)
    cosf = jnp.concatenate([cos, cos, zeros + 1.0], axis=-1)
    sina = jnp.concatenate([-sin, jnp.zeros_like(sin), zeros], axis=-1)
    sinb = jnp.concatenate([jnp.zeros_like(sin), sin, zeros], axis=-1)
    del half
    return cosf, sina, sinb


def _cmp_to_slc(n_chunk, n_slc):
    i = np.arange(n_chunk)[:, None] * CMP_STRIDE
    j = np.arange(n_slc)[None, :] * SLC_BLOCK
    ov = np.clip(np.minimum(i + CMP_BLOCK, j + SLC_BLOCK) - np.maximum(i, j), 0, None) / CMP_BLOCK
    ov[n_chunk - 1] = 0.0
    return jnp.asarray(ov.T, dtype=BF16)


def kernel(x, mem, positions, norm_gain, w_in, ln_v_gain, ln_v_bias, w_spatial, b_spatial, cmp_pe_k, cmp_w1_k, cmp_w2_k, cmp_pe_v, cmp_w1_v, cmp_w2_v, mem_norm_gain, w_mem_kv, w_branch_a, w_branch_b, w_branch_c, w_out, final_norm_gain):
    depth = w_in.shape[0]
    b, s, d = x.shape
    assert s % NSA_TQ == 0 and s % PROJ_TM == 0 and s >= WINDOW + NSA_TQ and d == w_in.shape[1]

    cosf, sina, sinb = _rope_lane_tables(positions)
    mcs = _cmp_to_slc(s // CMP_STRIDE, s // SLC_BLOCK)

    gate_w = jnp.pad(w_in[:, :, _B_G:_B_KV], ((0, 0), (0, 0), (0, GATE_PAD - (_B_KV - _B_G))))
    w_p = jnp.concatenate([w_in[:, :, _B_Q:_B_Z], w_in[:, :, _B_KV:_C_Q], gate_w], axis=-1).astype(BF16)
    w_f = jnp.concatenate([w_in[:, :, _A_U:_B_Q], w_in[:, :, _B_Z:_B_G], w_in[:, :, _C_Q:_MERGE],
                           w_in[:, :, _MERGE:]], axis=-1).astype(BF16)
    wsp = (w_spatial * jnp.tril(jnp.ones((CHUNK, CHUNK), w_spatial.dtype))).astype(BF16)
    bsp = jnp.broadcast_to(b_spatial[..., None], b_spatial.shape + (A_WIDTH // A_GROUPS,))
    pe = jnp.stack([cmp_pe_k, cmp_pe_v], axis=1)
    w1 = jnp.stack([cmp_w1_k, cmp_w1_v], axis=1).astype(BF16)
    w2 = jnp.stack([cmp_w2_k, cmp_w2_v], axis=1).astype(BF16)
    wa, wb, wc, wo = (w.astype(BF16) for w in (w_branch_a, w_branch_b, w_branch_c, w_out))

    mkv = _memkv_call(mem, mem_norm_gain[None, :], w_mem_kv.astype(BF16))
    fgain = final_norm_gain[None, :]
    for l in range(depth):
        gain = norm_gain[l][None, :]
        q, gates_t, cmp_in, ksel, vsel_t, kwin, vwin_t = _proj_call(x, gain, w_p[l], cosf, sina, sinb)
        cmp, cmp_t = _compress_call(cmp_in, pe[l], w1[l], w2[l])
        ob = _nsa_call(q, gates_t, cmp, cmp_t, ksel, vsel_t, kwin, vwin_t, mcs)
        x = _fused_call(x, ob, gain, w_f[l], ln_v_gain[l][None, :], ln_v_bias[l][None, :], wsp[l], bsp[l],
                        mkv, l, wa[l], wb[l], wc[l], wo[l], fgain, l == depth - 1)
    return x
```

```python
import functools

import numpy as np
import jax
import jax.numpy as jnp
from jax import lax
from jax.experimental import pallas as pl
from jax.experimental.pallas import tpu as pltpu

F32 = jnp.float32
BF16 = jnp.bfloat16

HEAD_DIM = 128
A_GROUPS = 4
A_WIDTH = 512
CHUNK = 128
NSA_HEADS = 8
NSA_KV_GROUPS = 2
HEADS_PER_GROUP = NSA_HEADS // NSA_KV_GROUPS
NSA_WIDTH = NSA_HEADS * HEAD_DIM
CMP_BLOCK = 32
CMP_STRIDE = 16
SLC_BLOCK = 64
N_SELECT = 16
WINDOW = 512
MEM_HEADS = 4
MEM_WIDTH = MEM_HEADS * HEAD_DIM
ROPE_DIM = HEAD_DIM // 4
ROPE_THETA = 500000.0
EPS = 1e-6
NEG_INF = -1e30
FORCED_SCORE = 1e4
ATT_SCALE = HEAD_DIM ** -0.5
LOG2_E = 1.4426950408889634

_OFF = np.cumsum((0, A_WIDTH, A_WIDTH, A_WIDTH, NSA_WIDTH, NSA_WIDTH, NSA_HEADS * 3, 3 * 2 * NSA_KV_GROUPS * HEAD_DIM,
                  MEM_WIDTH, MEM_WIDTH))
(_A_U, _A_V, _A_Z, _B_Q, _B_Z, _B_G, _B_KV, _C_Q, _C_Z, _MERGE) = (int(v) for v in _OFF)
GATE_PAD = 128
GATE_ROWS = 32
V_ROWS = HEAD_DIM + 16

PROJ_TM = 512
NSA_TQ = 256
NSA_KC = 256
FUSE_TM = 256
VMEM_LIMIT = 56 * 1024 * 1024


def _nt_dot(a, b):
    return lax.dot_general(a, b, (((1,), (1,)), ((), ())), preferred_element_type=F32)


def _dot(a, b):
    return jnp.dot(a, b, preferred_element_type=F32)


def _rmsnorm(x, gain):
    return x * lax.rsqrt(jnp.mean(x * x, axis=-1, keepdims=True) + EPS) * gain


def _silu(x):
    return x * jax.nn.sigmoid(x)


def _resident(block_shape, index_map):
    return pl.BlockSpec(block_shape, index_map, pipeline_mode=pl.Buffered(1))


def _proj_kernel(x_ref, gain_ref, w_ref, cos_ref, sina_ref, sinb_ref,
                 q_ref, gate_ref, cmp_ref, ksel_ref, vsel_ref, kwin_ref, vwin_ref, *, n_slc):
    tm = x_ref.shape[1]
    t0 = pl.program_id(1) * tm
    h = _rmsnorm(x_ref[0], gain_ref[...]).astype(BF16)
    proj = _dot(h, w_ref[...])
    cosf, sina, sinb = cos_ref[0], sina_ref[0], sinb_ref[0]

    def rope(t):
        return t * cosf + pltpu.roll(t, HEAD_DIM - ROPE_DIM // 2, 1) * sina + pltpu.roll(t, ROPE_DIM // 2, 1) * sinb

    def col(j):
        return proj[:, j * HEAD_DIM:(j + 1) * HEAD_DIM]

    for hd in range(NSA_HEADS):
        q_ref[0, hd] = (rope(col(hd)) * (ATT_SCALE * LOG2_E)).T.astype(BF16)
    kv0 = NSA_HEADS
    g2 = NSA_KV_GROUPS
    row = t0 + lax.broadcasted_iota(jnp.int32, (tm, n_slc), 0)
    blk = lax.broadcasted_iota(jnp.int32, (tm, n_slc), 1)
    sel_onehot = (row // SLC_BLOCK == blk).astype(BF16)
    ones_row = (lax.broadcasted_iota(jnp.int32, (V_ROWS - HEAD_DIM, tm), 0) == 0).astype(BF16)
    for g in range(g2):
        cmp_ref[0, g] = rope(col(kv0 + g))
        cmp_ref[0, g2 + g] = col(kv0 + g2 + g)
        ksel_ref[0, g, :, 0:HEAD_DIM] = rope(col(kv0 + 2 * g2 + g)).astype(BF16)
        ksel_ref[0, g, :, HEAD_DIM:] = sel_onehot
        kwin_ref[0, g] = rope(col(kv0 + 4 * g2 + g)).astype(BF16)
        for v_ref, j in ((vsel_ref, kv0 + 3 * g2 + g), (vwin_ref, kv0 + 5 * g2 + g)):
            v_ref[0, g, 0:HEAD_DIM, :] = col(j).T.astype(BF16)
            v_ref[0, g, HEAD_DIM:, :] = ones_row
    gate_ref[0] = jax.nn.sigmoid(proj[:, (kv0 + 6 * g2) * HEAD_DIM:]).T[0:GATE_ROWS]


def _proj_call(x, gain, w_p, cosf, sina, sinb):
    b, s, d = x.shape
    tm = PROJ_TM
    n_slc = s // SLC_BLOCK
    g2 = NSA_KV_GROUPS
    wp_cols = w_p.shape[1]
    tok = lambda bi, i: (bi, i, 0)
    grp = lambda bi, i: (bi, 0, i, 0)
    grp_t = lambda bi, i: (bi, 0, 0, i)
    return pl.pallas_call(
        functools.partial(_proj_kernel, n_slc=n_slc),
        grid=(b, s // tm),
        in_specs=[
            pl.BlockSpec((1, tm, d), tok),
            pl.BlockSpec((1, d), lambda bi, i: (0, 0)),
            _resident((d, wp_cols), lambda bi, i: (0, 0)),
            pl.BlockSpec((1, tm, HEAD_DIM), tok),
            pl.BlockSpec((1, tm, HEAD_DIM), tok),
            pl.BlockSpec((1, tm, HEAD_DIM), tok),
        ],
        out_specs=[
            pl.BlockSpec((1, NSA_HEADS, HEAD_DIM, tm), grp_t),
            pl.BlockSpec((1, GATE_ROWS, tm), lambda bi, i: (bi, 0, i)),
            pl.BlockSpec((1, 2 * g2, tm, HEAD_DIM), grp),
            pl.BlockSpec((1, g2, tm, HEAD_DIM + n_slc), grp),
            pl.BlockSpec((1, g2, V_ROWS, tm), grp_t),
            pl.BlockSpec((1, g2, tm, HEAD_DIM), grp),
            pl.BlockSpec((1, g2, V_ROWS, tm), grp_t),
        ],
        out_shape=[
            jax.ShapeDtypeStruct((b, NSA_HEADS, HEAD_DIM, s), BF16),
            jax.ShapeDtypeStruct((b, GATE_ROWS, s), F32),
            jax.ShapeDtypeStruct((b, 2 * g2, s, HEAD_DIM), F32),
            jax.ShapeDtypeStruct((b, g2, s, HEAD_DIM + n_slc), BF16),
            jax.ShapeDtypeStruct((b, g2, V_ROWS, s), BF16),
            jax.ShapeDtypeStruct((b, g2, s, HEAD_DIM), BF16),
            jax.ShapeDtypeStruct((b, g2, V_ROWS, s), BF16),
        ],
        compiler_params=pltpu.CompilerParams(
            dimension_semantics=("arbitrary", "arbitrary"), vmem_limit_bytes=VMEM_LIMIT),
        name="nsa_proj",
    )(x, gain, w_p, cosf, sina, sinb)


def _compress_kernel(x_ref, pe_ref, w1_ref, w2_ref, o_ref, ot_ref):
    n_chunk = o_ref.shape[2]
    first = jnp.zeros((n_chunk, HEAD_DIM), F32)
    second = jnp.zeros((n_chunk, HEAD_DIM), F32)
    for c in range(CMP_STRIDE):
        rows = x_ref[0, 0, pl.ds(c, n_chunk, stride=CMP_STRIDE), :]
        first = first + _dot((rows + pe_ref[0, c:c + 1, :]).astype(BF16), w1_ref[0, c])
        second = second + _dot((rows + pe_ref[0, CMP_STRIDE + c:CMP_STRIDE + c + 1, :]).astype(BF16),
                               w1_ref[0, CMP_STRIDE + c])
    hidden = first + pltpu.roll(second, n_chunk - 1, 0)
    out = _dot(_silu(hidden).astype(BF16), w2_ref[0])
    o_ref[0, 0] = out.astype(BF16)
    ot_ref[0, 0] = out.T.astype(BF16)


def _compress_call(cmp_in, pe, w1, w2):
    b, n4, s, dh = cmp_in.shape
    g2 = NSA_KV_GROUPS
    n_chunk = s // CMP_STRIDE
    return pl.pallas_call(
        _compress_kernel,
        grid=(b, n4),
        in_specs=[
            pl.BlockSpec((1, 1, s, dh), lambda bi, j: (bi, j, 0, 0)),
            pl.BlockSpec((1, CMP_BLOCK, dh), lambda bi, j: (j // g2, 0, 0)),
            pl.BlockSpec((1, CMP_BLOCK, dh, dh), lambda bi, j: (j // g2, 0, 0, 0)),
            pl.BlockSpec((1, dh, dh), lambda bi, j: (j // g2, 0, 0)),
        ],
        out_specs=[pl.BlockSpec((1, 1, n_chunk, dh), lambda bi, j: (bi, j, 0, 0)),
                   pl.BlockSpec((1, 1, dh, n_chunk), lambda bi, j: (bi, j, 0, 0))],
        out_shape=[jax.ShapeDtypeStruct((b, n4, n_chunk, dh), BF16),
                   jax.ShapeDtypeStruct((b, n4, dh, n_chunk), BF16)],
        compiler_params=pltpu.CompilerParams(
            dimension_semantics=("arbitrary", "arbitrary"), vmem_limit_bytes=VMEM_LIMIT),
        name="nsa_compress",
    )(cmp_in, pe, w1, w2)


def _nsa_kernel(q_ref, gate_ref, cmp_ref, cmpt_ref, ksel_ref, vsel_ref, kwin_ref, vwin_ref, mcs_ref, o_ref,
                lhs_ref, m_ref, acc_ref, part_ref, *, seq):
    tq = q_ref.shape[3]
    assert tq == NSA_KC
    rows = HEADS_PER_GROUP * tq
    half = rows // 2
    n_chunk = seq // CMP_STRIDE
    n_cmp = n_chunk - 1
    n_slc = seq // SLC_BLOCK
    groups = range(NSA_KV_GROUPS)
    i = pl.program_id(1)
    t0 = i * tq
    tok = t0 + lax.broadcasted_iota(jnp.int32, (1, tq), 1)
    tpos = jnp.concatenate([tok] * HEADS_PER_GROUP, axis=1)

    def mask_heads(s, mask):
        return jnp.concatenate(
            [jnp.where(mask, s[:, r * tq:(r + 1) * tq], NEG_INF) for r in range(s.shape[1] // tq)], axis=1)

    def gate_row(g, branch):
        return jnp.concatenate(
            [gate_ref[0, 3 * (HEADS_PER_GROUP * g + r) + branch:3 * (HEADS_PER_GROUP * g + r) + branch + 1, :]
             for r in range(HEADS_PER_GROUP)], axis=1)

    def normalise(acc_t, gate):
        return acc_t[0:HEAD_DIM] * (gate / acc_t[HEAD_DIM:HEAD_DIM + 1])

    qgs = [jnp.concatenate([q_ref[0, HEADS_PER_GROUP * g + r] for r in range(HEADS_PER_GROUP)], axis=1)
           for g in groups]

    nrow = lax.broadcasted_iota(jnp.int32, (n_chunk, 1), 0)
    cmask = nrow <= jnp.minimum((tok - (CMP_BLOCK - 1)) // CMP_STRIDE, n_cmp - 1)
    sees_block = tpos >= CMP_BLOCK - 1
    imps = [None] * NSA_KV_GROUPS

    def run_pipelined(items):
        scores = [score_fn() for score_fn, _ in items]
        for k, (_, consume_fn) in enumerate(items):
            consume_fn(scores[k])
            scores[k] = None

    def cmp_consume(g, s_c):
        s_c = mask_heads(s_c, cmask)
        e_c = jnp.exp2(s_c - jnp.max(s_c, axis=0, keepdims=True))
        p_c = e_c * jnp.where(sees_block, 1.0 / jnp.sum(e_c, axis=0, keepdims=True), 0.0)
        part_ref[0, g] = _dot(cmpt_ref[0, NSA_KV_GROUPS + g], p_c.astype(BF16)) * gate_row(g, 0)
        p_sum = p_c[:, 0:tq]
        for r in range(1, HEADS_PER_GROUP):
            p_sum = p_sum + p_c[:, r * tq:(r + 1) * tq]
        p_hi = p_sum.astype(BF16)
        rest = p_sum - p_hi.astype(F32)
        p_mid = rest.astype(BF16)
        p_lo = (rest - p_mid.astype(F32)).astype(BF16)
        mcs_t = mcs_ref[...]
        imps[g] = _dot(mcs_t, p_hi) + _dot(mcs_t, p_mid) + _dot(mcs_t, p_lo)

    wlen = WINDOW + tq
    w0 = pl.multiple_of(jnp.maximum(t0 - WINDOW, 0), tq)
    rel = tok - (w0 + lax.broadcasted_iota(jnp.int32, (wlen, 1), 0))
    wmask = lax.bitcast_convert_type(rel, jnp.uint32) < WINDOW

    def win_consume(g, s_w):
        s_w = mask_heads(s_w, wmask)
        p_w = jnp.exp2(s_w - jnp.max(s_w, axis=0, keepdims=True)).astype(BF16)
        part_ref[1, g] = normalise(_dot(vwin_ref[0, g, :, pl.ds(w0, wlen)], p_w), gate_row(g, 2))

    run_pipelined(
        [(functools.partial(_dot, cmp_ref[0, g], qgs[g]), functools.partial(cmp_consume, g)) for g in groups]
        + [(functools.partial(_dot, kwin_ref[0, g, pl.ds(w0, wlen), :], qgs[g]), functools.partial(win_consume, g))
           for g in groups])

    jrow = lax.broadcasted_iota(jnp.int32, (n_slc, tq), 0)
    cur = (t0 + lax.broadcasted_iota(jnp.int32, (n_slc, tq), 1)) // SLC_BLOCK
    forced = (jrow == 0) | (jrow == cur) | (jrow == cur - 1)
    for g in groups:
        cand = jnp.where(forced, -jnp.inf, jnp.where(jrow <= cur, imps[g], -1.0))
        sel_t = forced
        for _ in range(min(N_SELECT, n_slc) - 3):
            best = jnp.max(cand, axis=0, keepdims=True)
            first = jnp.min(jnp.where(cand == best, jrow, n_slc), axis=0, keepdims=True)
            hit = jrow == first
            sel_t = sel_t | hit
            cand = jnp.where(hit, -jnp.inf, cand)
        sel_bias = jnp.where(sel_t, 0.0, NEG_INF).astype(BF16)
        lhs_ref[g, 0:HEAD_DIM, :] = qgs[g]
        lhs_ref[g, HEAD_DIM:, :] = jnp.concatenate([sel_bias] * HEADS_PER_GROUP, axis=1)
        m_ref[g] = jnp.full((1, rows), NEG_INF, F32)
        acc_ref[g] = jnp.zeros((V_ROWS, rows), F32)

    def sel_score(g, hf, k0, width):
        return _dot(ksel_ref[0, g, pl.ds(k0, width), :], lhs_ref[g, :, hf * half:(hf + 1) * half])

    def sel_consume(g, hf, k0, width, masked, s_s):
        lanes = slice(hf * half, (hf + 1) * half)
        if masked:
            kpos = k0 + lax.broadcasted_iota(jnp.int32, (width, 1), 0)
            s_s = mask_heads(s_s, kpos <= tok)
        m_old = m_ref[g, :, lanes]
        m_new = jnp.maximum(m_old, jnp.max(s_s, axis=0, keepdims=True))
        p_s = jnp.exp2(s_s - m_new).astype(BF16)
        acc_ref[g, :, lanes] = (jnp.exp2(m_old - m_new) * acc_ref[g, :, lanes]
                                + _dot(vsel_ref[0, g, :, pl.ds(k0, width)], p_s))
        m_ref[g, :, lanes] = m_new

    def sweep(first_chunk, n_chunks, last_masked=False):
        k0 = pl.multiple_of(first_chunk * NSA_KC, NSA_KC)
        width = n_chunks * NSA_KC
        run_pipelined([(functools.partial(sel_score, g, hf, k0, width),
                        functools.partial(sel_consume, g, hf, k0, width, last_masked))
                       for g in groups for hf in range(2)])

    def octet_body(ko, carry):
        sweep(8 * ko, 8)
        return carry

    lax.fori_loop(0, i // 8, octet_body, 0)
    done = (i // 8) * 8

    @pl.when(i % 8 >= 4)
    def _():
        sweep(done, 4)

    done = (i // 4) * 4

    @pl.when(i % 4 >= 2)
    def _():
        sweep(done, 2)

    done = (i // 2) * 2

    @pl.when(i % 2 == 1)
    def _():
        sweep(done, 2, last_masked=True)

    @pl.when(i % 2 == 0)
    def _():
        sweep(done, 1, last_masked=True)

    for g in groups:
        mixed_t = (part_ref[0, g] + normalise(acc_ref[g], gate_row(g, 1))) + part_ref[1, g]
        for r in range(HEADS_PER_GROUP):
            hd = HEADS_PER_GROUP * g + r
            o_ref[0, :, hd * HEAD_DIM:(hd + 1) * HEAD_DIM] = mixed_t[:, r * tq:(r + 1) * tq].T


def _nsa_call(q, gates_t, cmp, cmp_t, ksel, vsel_t, kwin, vwin_t, mcs_t):
    b, _, _, s = q.shape
    tq = NSA_TQ
    g2 = NSA_KV_GROUPS
    n_chunk = s // CMP_STRIDE
    n_slc = s // SLC_BLOCK
    rows = HEADS_PER_GROUP * tq
    tok = lambda bi, i: (bi, i, 0)
    per_batch = lambda bi, i: (bi, 0, 0, 0)
    return pl.pallas_call(
        functools.partial(_nsa_kernel, seq=s),
        grid=(b, s // tq),
        in_specs=[
            pl.BlockSpec((1, NSA_HEADS, HEAD_DIM, tq), lambda bi, i: (bi, 0, 0, i)),
            pl.BlockSpec((1, GATE_ROWS, tq), lambda bi, i: (bi, 0, i)),
            _resident((1, 2 * g2, n_chunk, HEAD_DIM), per_batch),
            _resident((1, 2 * g2, HEAD_DIM, n_chunk), per_batch),
            _resident((1, g2, s, HEAD_DIM + n_slc), per_batch),
            _resident((1, g2, V_ROWS, s), per_batch),
            _resident((1, g2, s, HEAD_DIM), per_batch),
            _resident((1, g2, V_ROWS, s), per_batch),
            _resident((n_slc, n_chunk), lambda bi, i: (0, 0)),
        ],
        out_specs=pl.BlockSpec((1, tq, NSA_WIDTH), tok),
        out_shape=jax.ShapeDtypeStruct((b, s, NSA_WIDTH), F32),
        scratch_shapes=[
            pltpu.VMEM((g2, HEAD_DIM + n_slc, rows), BF16),
            pltpu.VMEM((g2, 1, rows), F32),
            pltpu.VMEM((g2, V_ROWS, rows), F32),
            pltpu.VMEM((2, g2, HEAD_DIM, rows), F32),
        ],
        compiler_params=pltpu.CompilerParams(
            dimension_semantics=("arbitrary", "arbitrary"), vmem_limit_bytes=VMEM_LIMIT),
        name="nsa_attention",
    )(q, gates_t, cmp, cmp_t, ksel, vsel_t, kwin, vwin_t, mcs_t)


def _memkv_kernel(mem_ref, gain_ref, w_ref, o_ref):
    mem_n = _rmsnorm(mem_ref[0], gain_ref[...]).astype(BF16)
    o_ref[0, 0] = _dot(mem_n, w_ref[0]).astype(BF16)


def _memkv_call(mem, gain, w_mem):
    depth = w_mem.shape[0]
    b, m, d = mem.shape
    return pl.pallas_call(
        _memkv_kernel,
        grid=(depth, b),
        in_specs=[
            pl.BlockSpec((1, m, d), lambda l, bi: (bi, 0, 0)),
            pl.BlockSpec((1, d), lambda l, bi: (0, 0)),
            pl.BlockSpec((1, d, 2 * MEM_WIDTH), lambda l, bi: (l, 0, 0)),
        ],
        out_specs=pl.BlockSpec((1, 1, m, 2 * MEM_WIDTH), lambda l, bi: (l, bi, 0, 0)),
        out_shape=jax.ShapeDtypeStruct((depth, b, m, 2 * MEM_WIDTH), BF16),
        compiler_params=pltpu.CompilerParams(
            dimension_semantics=("arbitrary", "arbitrary"), vmem_limit_bytes=VMEM_LIMIT),
        name="mem_kv",
    )(mem, gain, w_mem)


_F_AUVZ = 0
_F_BZ = 3 * A_WIDTH
_F_CQZ = _F_BZ + NSA_WIDTH
_F_MERGE = _F_CQZ + 2 * MEM_WIDTH


def _fused_kernel(x_ref, ob_ref, gain_ref, w_ref, lng_ref, lnb_ref, wsp_ref, bsp_ref, mkv_ref,
                  wa_ref, wb_ref, wc_ref, wo_ref, fgain_ref, o_ref, *, final):
    tm, d = x_ref.shape[1], x_ref.shape[2]
    x = x_ref[0]
    h = _rmsnorm(x, gain_ref[...]).astype(BF16)

    def proj(c0, width):
        return _dot(h, w_ref[:, c0:c0 + width])

    uvz = proj(_F_AUVZ, 3 * A_WIDTH)
    cqz = proj(_F_CQZ, 2 * MEM_WIDTH)
    zb = proj(_F_BZ, NSA_WIDTH)
    merge = [proj(_F_MERGE + k * d, d) for k in range(3)]

    u, v, z = uvz[:, 0:A_WIDTH], uvz[:, A_WIDTH:2 * A_WIDTH], uvz[:, 2 * A_WIDTH:]
    vc = v - jnp.mean(v, axis=-1, keepdims=True)
    vn = (vc * lax.rsqrt(jnp.mean(vc * vc, axis=-1, keepdims=True) + EPS) * lng_ref[...] + lnb_ref[...]).astype(BF16)
    gdim = A_WIDTH // A_GROUPS
    mixed = jnp.concatenate([
        jnp.concatenate([
            _dot(wsp_ref[gi], vn[c * CHUNK:(c + 1) * CHUNK, gi * gdim:(gi + 1) * gdim]) + bsp_ref[gi]
            for gi in range(A_GROUPS)], axis=1)
        for c in range(tm // CHUNK)], axis=0)

    scores = [_nt_dot(cqz[:, hd * HEAD_DIM:(hd + 1) * HEAD_DIM].astype(BF16),
                      mkv_ref[0, 0, :, hd * HEAD_DIM:(hd + 1) * HEAD_DIM]) * ATT_SCALE for hd in range(MEM_HEADS)]

    o_a = (u * mixed * _silu(z)).astype(BF16)
    acc = jax.nn.sigmoid(merge[0]) * _dot(o_a, wa_ref[...])

    o_b = (ob_ref[0] * _silu(zb)).astype(BF16)
    acc = acc + jax.nn.sigmoid(merge[1]) * _dot(o_b, wb_ref[...])

    heads = []
    for hd in range(MEM_HEADS):
        e_m = jnp.exp(scores[hd] - jnp.max(scores[hd], axis=1, keepdims=True))
        p_m = e_m * (1.0 / jnp.sum(e_m, axis=1, keepdims=True))
        heads.append(_dot(p_m.astype(BF16), mkv_ref[0, 0, :, MEM_WIDTH + hd * HEAD_DIM:MEM_WIDTH + (hd + 1) * HEAD_DIM]))
    o_c = (jnp.concatenate(heads, axis=1) * _silu(cqz[:, MEM_WIDTH:])).astype(BF16)
    acc = acc + jax.nn.sigmoid(merge[2]) * _dot(o_c, wc_ref[...])

    x_new = x + _dot(acc.astype(BF16), wo_ref[...])
    if final:
        x_new = _rmsnorm(x_new, fgain_ref[...])
    o_ref[0] = x_new


def _fused_call(x, ob, gain, w_f, lng, lnb, wsp, bsp, mkv, layer, wa, wb, wc, wo, fgain, final):
    b, s, d = x.shape
    tm = FUSE_TM
    m = mkv.shape[2]
    tok = lambda bi, i: (bi, i, 0)
    c2 = lambda bi, i: (0, 0)
    c3 = lambda bi, i: (0, 0, 0)
    return pl.pallas_call(
        functools.partial(_fused_kernel, final=final),
        grid=(b, s // tm),
        in_specs=[
            pl.BlockSpec((1, tm, d), tok),
            pl.BlockSpec((1, tm, NSA_WIDTH), tok),
            pl.BlockSpec((1, d), c2),
            _resident(w_f.shape, c2),
            pl.BlockSpec((1, A_WIDTH), c2),
            pl.BlockSpec((1, A_WIDTH), c2),
            _resident(wsp.shape, c3),
            _resident(bsp.shape, c3),
            _resident((1, 1, m, 2 * MEM_WIDTH), lambda bi, i: (layer, bi, 0, 0)),
            _resident(wa.shape, c2),
            _resident(wb.shape, c2),
            _resident(wc.shape, c2),
            _resident(wo.shape, c2),
            pl.BlockSpec((1, d), c2),
        ],
        out_specs=pl.BlockSpec((1, tm, d), tok),
        out_shape=jax.ShapeDtypeStruct((b, s, d), F32),
        compiler_params=pltpu.CompilerParams(
            dimension_semantics=("arbitrary", "arbitrary"), vmem_limit_bytes=VMEM_LIMIT),
        name="fused_mix",
    )(x, ob, gain, w_f, lng, lnb, wsp, bsp, mkv, wa, wb, wc, wo, fgain)


def _rope_lane_tables(positions):
    half = ROPE_DIM // 2
    inv_freq = ROPE_THETA ** (-jnp.arange(0, ROPE_DIM, 2, dtype=F32) / ROPE_DIM)
    ang = positions.astype(F32)[..., None] * inv_freq
    cos, sin = jnp.cos(ang), jnp.sin(ang)
    zeros = jnp.zeros(cos.shape[:-1] + (HEAD_DIM - ROPE_DIM,), F32)
    cosf = jnp.concatenate([cos, cos, zeros + 1.0], axis=-1)
    sina = jnp.concatenate([-sin, jnp.zeros_like(sin), zeros], axis=-1)
    sinb = jnp.concatenate([jnp.zeros_like(sin), sin, zeros], axis=-1)
    del half
    return cosf, sina, sinb


def _cmp_to_slc(n_chunk, n_slc):
    i = np.arange(n_chunk)[:, None] * CMP_STRIDE
    j = np.arange(n_slc)[None, :] * SLC_BLOCK
    ov = np.clip(np.minimum(i + CMP_BLOCK, j + SLC_BLOCK) - np.maximum(i, j), 0, None) / CMP_BLOCK
    ov[n_chunk - 1] = 0.0
    return jnp.asarray(ov.T, dtype=BF16)


def kernel(x, mem, positions, norm_gain, w_in, ln_v_gain, ln_v_bias, w_spatial, b_spatial, cmp_pe_k, cmp_w1_k, cmp_w2_k, cmp_pe_v, cmp_w1_v, cmp_w2_v, mem_norm_gain, w_mem_kv, w_branch_a, w_branch_b, w_branch_c, w_out, final_norm_gain):
    depth = w_in.shape[0]
    b, s, d = x.shape
    assert s % NSA_TQ == 0 and s % PROJ_TM == 0 and s >= WINDOW + NSA_TQ and d == w_in.shape[1]

    cosf, sina, sinb = _rope_lane_tables(positions)
    mcs = _cmp_to_slc(s // CMP_STRIDE, s // SLC_BLOCK)

    gate_w = jnp.pad(w_in[:, :, _B_G:_B_KV], ((0, 0), (0, 0), (0, GATE_PAD - (_B_KV - _B_G))))
    w_p = jnp.concatenate([w_in[:, :, _B_Q:_B_Z], w_in[:, :, _B_KV:_C_Q], gate_w], axis=-1).astype(BF16)
    w_f = jnp.concatenate([w_in[:, :, _A_U:_B_Q], w_in[:, :, _B_Z:_B_G], w_in[:, :, _C_Q:_MERGE],
                           w_in[:, :, _MERGE:]], axis=-1).astype(BF16)
    wsp = (w_spatial * jnp.tril(jnp.ones((CHUNK, CHUNK), w_spatial.dtype))).astype(BF16)
    bsp = jnp.broadcast_to(b_spatial[..., None], b_spatial.shape + (A_WIDTH // A_GROUPS,))
    pe = jnp.stack([cmp_pe_k, cmp_pe_v], axis=1)
    w1 = jnp.stack([cmp_w1_k, cmp_w1_v], axis=1).astype(BF16)
    w2 = jnp.stack([cmp_w2_k, cmp_w2_v], axis=1).astype(BF16)
    wa, wb, wc, wo = (w.astype(BF16) for w in (w_branch_a, w_branch_b, w_branch_c, w_out))

    mkv = _memkv_call(mem, mem_norm_gain[None, :], w_mem_kv.astype(BF16))
    fgain = final_norm_gain[None, :]
    for l in range(depth):
        gain = norm_gain[l][None, :]
        q, gates_t, cmp_in, ksel, vsel_t, kwin, vwin_t = _proj_call(x, gain, w_p[l], cosf, sina, sinb)
        cmp, cmp_t = _compress_call(cmp_in, pe[l], w1[l], w2[l])
        ob = _nsa_call(q, gates_t, cmp, cmp_t, ksel, vsel_t, kwin, vwin_t, mcs)
        x = _fused_call(x, ob, gain, w_f[l], ln_v_gain[l][None, :], ln_v_bias[l][None, :], wsp[l], bsp[l],
                        mkv, l, wa[l], wb[l], wc[l], wo[l], fgain, l == depth - 1)
    return x
```

```python
import functools

import numpy as np
import jax
import jax.numpy as jnp
from jax import lax
from jax.experimental import pallas as pl
from jax.experimental.pallas import tpu as pltpu

F32 = jnp.float32
BF16 = jnp.bfloat16

HEAD_DIM = 128
A_GROUPS = 4
A_WIDTH = 512
CHUNK = 128
NSA_HEADS = 8
NSA_KV_GROUPS = 2
HEADS_PER_GROUP = NSA_HEADS // NSA_KV_GROUPS
NSA_WIDTH = NSA_HEADS * HEAD_DIM
CMP_BLOCK = 32
CMP_STRIDE = 16
SLC_BLOCK = 64
N_SELECT = 16
WINDOW = 512
MEM_HEADS = 4
MEM_WIDTH = MEM_HEADS * HEAD_DIM
ROPE_DIM = HEAD_DIM // 4
ROPE_THETA = 500000.0
EPS = 1e-6
NEG_INF = -1e30
FORCED_SCORE = 1e4
ATT_SCALE = HEAD_DIM ** -0.5
LOG2_E = 1.4426950408889634

_OFF = np.cumsum((0, A_WIDTH, A_WIDTH, A_WIDTH, NSA_WIDTH, NSA_WIDTH, NSA_HEADS * 3, 3 * 2 * NSA_KV_GROUPS * HEAD_DIM,
                  MEM_WIDTH, MEM_WIDTH))
(_A_U, _A_V, _A_Z, _B_Q, _B_Z, _B_G, _B_KV, _C_Q, _C_Z, _MERGE) = (int(v) for v in _OFF)
GATE_PAD = 128
GATE_ROWS = 32
V_ROWS = HEAD_DIM + 16

PROJ_TM = 512
NSA_TQ = 256
NSA_KC = 256
FUSE_TM = 256
VMEM_LIMIT = 56 * 1024 * 1024


def _nt_dot(a, b):
    return lax.dot_general(a, b, (((1,), (1,)), ((), ())), preferred_element_type=F32)


def _dot(a, b):
    return jnp.dot(a, b, preferred_element_type=F32)


def _rmsnorm(x, gain):
    return x * lax.rsqrt(jnp.mean(x * x, axis=-1, keepdims=True) + EPS) * gain


def _silu(x):
    return x * jax.nn.sigmoid(x)


def _resident(block_shape, index_map):
    return pl.BlockSpec(block_shape, index_map, pipeline_mode=pl.Buffered(1))


def _proj_kernel(x_ref, gain_ref, w_ref, cos_ref, sina_ref, sinb_ref,
                 q_ref, gate_ref, cmp_ref, ksel_ref, vsel_ref, kwin_ref, vwin_ref, *, n_slc):
    tm = x_ref.shape[1]
    t0 = pl.program_id(1) * tm
    h = _rmsnorm(x_ref[0], gain_ref[...]).astype(BF16)
    proj = _dot(h, w_ref[...])
    cosf, sina, sinb = cos_ref[0], sina_ref[0], sinb_ref[0]

    def rope(t):
        return t * cosf + pltpu.roll(t, HEAD_DIM - ROPE_DIM // 2, 1) * sina + pltpu.roll(t, ROPE_DIM // 2, 1) * sinb

    def col(j):
        return proj[:, j * HEAD_DIM:(j + 1) * HEAD_DIM]

    for hd in range(NSA_HEADS):
        q_ref[0, hd] = (rope(col(hd)) * (ATT_SCALE * LOG2_E)).T.astype(BF16)
    kv0 = NSA_HEADS
    g2 = NSA_KV_GROUPS
    row = t0 + lax.broadcasted_iota(jnp.int32, (tm, n_slc), 0)
    blk = lax.broadcasted_iota(jnp.int32, (tm, n_slc), 1)
    sel_onehot = (row // SLC_BLOCK == blk).astype(BF16)
    ones_row = (lax.broadcasted_iota(jnp.int32, (V_ROWS - HEAD_DIM, tm), 0) == 0).astype(BF16)
    for g in range(g2):
        cmp_ref[0, g] = rope(col(kv0 + g))
        cmp_ref[0, g2 + g] = col(kv0 + g2 + g)
        ksel_ref[0, g, :, 0:HEAD_DIM] = rope(col(kv0 + 2 * g2 + g)).astype(BF16)
        ksel_ref[0, g, :, HEAD_DIM:] = sel_onehot
        kwin_ref[0, g] = rope(col(kv0 + 4 * g2 + g)).astype(BF16)
        for v_ref, j in ((vsel_ref, kv0 + 3 * g2 + g), (vwin_ref, kv0 + 5 * g2 + g)):
            v_ref[0, g, 0:HEAD_DIM, :] = col(j).T.astype(BF16)
            v_ref[0, g, HEAD_DIM:, :] = ones_row
    gate_ref[0] = jax.nn.sigmoid(proj[:, (kv0 + 6 * g2) * HEAD_DIM:]).T[0:GATE_ROWS]


def _proj_call(x, gain, w_p, layer, cosf, sina, sinb):
    b, s, d = x.shape
    tm = PROJ_TM
    n_slc = s // SLC_BLOCK
    g2 = NSA_KV_GROUPS
    wp_cols = w_p.shape[2]
    tok = lambda bi, i: (bi, i, 0)
    grp = lambda bi, i: (bi, 0, i, 0)
    grp_t = lambda bi, i: (bi, 0, 0, i)
    return pl.pallas_call(
        functools.partial(_proj_kernel, n_slc=n_slc),
        grid=(b, s // tm),
        in_specs=[
            pl.BlockSpec((1, tm, d), tok),
            pl.BlockSpec((1, d), lambda bi, i: (0, 0)),
            _resident((None, d, wp_cols), lambda bi, i: (layer, 0, 0)),
            pl.BlockSpec((1, tm, HEAD_DIM), tok),
            pl.BlockSpec((1, tm, HEAD_DIM), tok),
            pl.BlockSpec((1, tm, HEAD_DIM), tok),
        ],
        out_specs=[
            pl.BlockSpec((1, NSA_HEADS, HEAD_DIM, tm), grp_t),
            pl.BlockSpec((1, GATE_ROWS, tm), lambda bi, i: (bi, 0, i)),
            pl.BlockSpec((1, 2 * g2, tm, HEAD_DIM), grp),
            pl.BlockSpec((1, g2, tm, HEAD_DIM + n_slc), grp),
            pl.BlockSpec((1, g2, V_ROWS, tm), grp_t),
            pl.BlockSpec((1, g2, tm, HEAD_DIM), grp),
            pl.BlockSpec((1, g2, V_ROWS, tm), grp_t),
        ],
        out_shape=[
            jax.ShapeDtypeStruct((b, NSA_HEADS, HEAD_DIM, s), BF16),
            jax.ShapeDtypeStruct((b, GATE_ROWS, s), F32),
            jax.ShapeDtypeStruct((b, 2 * g2, s, HEAD_DIM), F32),
            jax.ShapeDtypeStruct((b, g2, s, HEAD_DIM + n_slc), BF16),
            jax.ShapeDtypeStruct((b, g2, V_ROWS, s), BF16),
            jax.ShapeDtypeStruct((b, g2, s, HEAD_DIM), BF16),
            jax.ShapeDtypeStruct((b, g2, V_ROWS, s), BF16),
        ],
        compiler_params=pltpu.CompilerParams(
            dimension_semantics=("arbitrary", "arbitrary"), vmem_limit_bytes=VMEM_LIMIT),
        name="nsa_proj",
    )(x, gain, w_p, cosf, sina, sinb)


def _compress_kernel(x_ref, pe_ref, w1_ref, w2_ref, o_ref, ot_ref):
    n_chunk = o_ref.shape[2]
    first = jnp.zeros((n_chunk, HEAD_DIM), F32)
    second = jnp.zeros((n_chunk, HEAD_DIM), F32)
    for c in range(CMP_STRIDE):
        rows = x_ref[0, 0, pl.ds(c, n_chunk, stride=CMP_STRIDE), :]
        first = first + _dot((rows + pe_ref[0, c:c + 1, :]).astype(BF16), w1_ref[0, c])
        second = second + _dot((rows + pe_ref[0, CMP_STRIDE + c:CMP_STRIDE + c + 1, :]).astype(BF16),
                               w1_ref[0, CMP_STRIDE + c])
    hidden = first + pltpu.roll(second, n_chunk - 1, 0)
    out = _dot(_silu(hidden).astype(BF16), w2_ref[0])
    o_ref[0, 0] = out.astype(BF16)
    ot_ref[0, 0] = out.T.astype(BF16)


def _compress_call(cmp_in, pe, w1, w2):
    b, n4, s, dh = cmp_in.shape
    g2 = NSA_KV_GROUPS
    n_chunk = s // CMP_STRIDE
    return pl.pallas_call(
        _compress_kernel,
        grid=(b, n4),
        in_specs=[
            pl.BlockSpec((1, 1, s, dh), lambda bi, j: (bi, j, 0, 0)),
            pl.BlockSpec((1, CMP_BLOCK, dh), lambda bi, j: (j // g2, 0, 0)),
            pl.BlockSpec((1, CMP_BLOCK, dh, dh), lambda bi, j: (j // g2, 0, 0, 0)),
            pl.BlockSpec((1, dh, dh), lambda bi, j: (j // g2, 0, 0)),
        ],
        out_specs=[pl.BlockSpec((1, 1, n_chunk, dh), lambda bi, j: (bi, j, 0, 0)),
                   pl.BlockSpec((1, 1, dh, n_chunk), lambda bi, j: (bi, j, 0, 0))],
        out_shape=[jax.ShapeDtypeStruct((b, n4, n_chunk, dh), BF16),
                   jax.ShapeDtypeStruct((b, n4, dh, n_chunk), BF16)],
        compiler_params=pltpu.CompilerParams(
            dimension_semantics=("arbitrary", "arbitrary"), vmem_limit_bytes=VMEM_LIMIT),
        name="nsa_compress",
    )(cmp_in, pe, w1, w2)


def _nsa_kernel(q_ref, gate_ref, cmp_ref, cmpt_ref, ksel_ref, vsel_ref, kwin_ref, vwin_ref, mcs_ref, o_ref,
                lhs_ref, m_ref, acc_ref, part_ref, *, seq):
    tq = q_ref.shape[3]
    assert tq == NSA_KC
    rows = HEADS_PER_GROUP * tq
    half = rows // 2
    n_chunk = seq // CMP_STRIDE
    n_cmp = n_chunk - 1
    n_slc = seq // SLC_BLOCK
    groups = range(NSA_KV_GROUPS)
    i = pl.program_id(1)
    t0 = i * tq
    tok = t0 + lax.broadcasted_iota(jnp.int32, (1, tq), 1)
    tpos = jnp.concatenate([tok] * HEADS_PER_GROUP, axis=1)

    def mask_heads(s, mask):
        return jnp.concatenate(
            [jnp.where(mask, s[:, r * tq:(r + 1) * tq], NEG_INF) for r in range(s.shape[1] // tq)], axis=1)

    def gate_row(g, branch):
        return jnp.concatenate(
            [gate_ref[0, 3 * (HEADS_PER_GROUP * g + r) + branch:3 * (HEADS_PER_GROUP * g + r) + branch + 1, :]
             for r in range(HEADS_PER_GROUP)], axis=1)

    def normalise(acc_t, gate):
        return acc_t[0:HEAD_DIM] * (gate / acc_t[HEAD_DIM:HEAD_DIM + 1])

    qgs = [jnp.concatenate([q_ref[0, HEADS_PER_GROUP * g + r] for r in range(HEADS_PER_GROUP)], axis=1)
           for g in groups]

    nrow = lax.broadcasted_iota(jnp.int32, (n_chunk, 1), 0)
    cmask = nrow <= jnp.minimum((tok - (CMP_BLOCK - 1)) // CMP_STRIDE, n_cmp - 1)
    sees_block = tpos >= CMP_BLOCK - 1
    imps = [None] * NSA_KV_GROUPS

    def run_pipelined(items):
        scores = [score_fn() for score_fn, _ in items]
        for k, (_, consume_fn) in enumerate(items):
            consume_fn(scores[k])
            scores[k] = None

    def cmp_consume(g, s_c):
        s_c = mask_heads(s_c, cmask)
        e_c = jnp.exp2(s_c - jnp.max(s_c, axis=0, keepdims=True))
        p_c = e_c * jnp.where(sees_block, 1.0 / jnp.sum(e_c, axis=0, keepdims=True), 0.0)
        part_ref[0, g] = _dot(cmpt_ref[0, NSA_KV_GROUPS + g], p_c.astype(BF16)) * gate_row(g, 0)
        p_sum = p_c[:, 0:tq]
        for r in range(1, HEADS_PER_GROUP):
            p_sum = p_sum + p_c[:, r * tq:(r + 1) * tq]
        p_hi = p_sum.astype(BF16)
        p_lo = (p_sum - p_hi.astype(F32)).astype(BF16)
        mcs_t = mcs_ref[...]
        imps[g] = _dot(mcs_t, p_hi) + _dot(mcs_t, p_lo)

    wlen = WINDOW + tq
    w0 = pl.multiple_of(jnp.maximum(t0 - WINDOW, 0), tq)
    rel = tok - (w0 + lax.broadcasted_iota(jnp.int32, (wlen, 1), 0))
    wmask = lax.bitcast_convert_type(rel, jnp.uint32) < WINDOW

    def win_consume(g, s_w):
        s_w = mask_heads(s_w, wmask)
        p_w = jnp.exp2(s_w - jnp.max(s_w, axis=0, keepdims=True)).astype(BF16)
        part_ref[1, g] = normalise(_dot(vwin_ref[0, g, :, pl.ds(w0, wlen)], p_w), gate_row(g, 2))

    run_pipelined(
        [(functools.partial(_dot, cmp_ref[0, g], qgs[g]), functools.partial(cmp_consume, g)) for g in groups]
        + [(functools.partial(_dot, kwin_ref[0, g, pl.ds(w0, wlen), :], qgs[g]), functools.partial(win_consume, g))
           for g in groups])

    jrow = lax.broadcasted_iota(jnp.int32, (n_slc, tq), 0)
    cur = (t0 + lax.broadcasted_iota(jnp.int32, (n_slc, tq), 1)) // SLC_BLOCK
    forced = (jrow == 0) | (jrow == cur) | (jrow == cur - 1)
    for g in groups:
        cand = jnp.where(forced, -jnp.inf, jnp.where(jrow <= cur, imps[g], -1.0))
        for _ in range(min(N_SELECT, n_slc) - 3):
            best = jnp.max(cand, axis=0, keepdims=True)
            first = jnp.min(jnp.where(cand == best, jrow, n_slc), axis=0, keepdims=True)
            cand = jnp.where(jrow == first, -jnp.inf, cand)
        sel_bias = jnp.where(cand == -jnp.inf, 0.0, NEG_INF).astype(BF16)
        lhs_ref[g, 0:HEAD_DIM, :] = qgs[g]
        lhs_ref[g, HEAD_DIM:, :] = jnp.concatenate([sel_bias] * HEADS_PER_GROUP, axis=1)
        m_ref[g] = jnp.full((1, rows), NEG_INF, F32)
        acc_ref[g] = jnp.zeros((V_ROWS, rows), F32)

    def sel_score(g, hf, k0, width):
        return _dot(ksel_ref[0, g, pl.ds(k0, width), :], lhs_ref[g, :, hf * half:(hf + 1) * half])

    def sel_consume(g, hf, k0, width, masked, s_s):
        lanes = slice(hf * half, (hf + 1) * half)
        if masked:
            kpos = k0 + lax.broadcasted_iota(jnp.int32, (width, 1), 0)
            s_s = mask_heads(s_s, kpos <= tok)
        m_old = m_ref[g, :, lanes]
        m_new = jnp.maximum(m_old, jnp.max(s_s, axis=0, keepdims=True))
        p_s = jnp.exp2(s_s - m_new).astype(BF16)
        acc_ref[g, :, lanes] = (jnp.exp2(m_old - m_new) * acc_ref[g, :, lanes]
                                + _dot(vsel_ref[0, g, :, pl.ds(k0, width)], p_s))
        m_ref[g, :, lanes] = m_new

    def sweep(first_chunk, n_chunks, last_masked=False):
        k0 = pl.multiple_of(first_chunk * NSA_KC, NSA_KC)
        width = n_chunks * NSA_KC
        run_pipelined([(functools.partial(sel_score, g, hf, k0, width),
                        functools.partial(sel_consume, g, hf, k0, width, last_masked))
                       for g in groups for hf in range(2)])

    def octet_body(ko, carry):
        sweep(8 * ko, 8)
        return carry

    lax.fori_loop(0, i // 8, octet_body, 0)
    done = (i // 8) * 8

    @pl.when(i % 8 >= 4)
    def _():
        sweep(done, 4)

    done = (i // 4) * 4

    @pl.when(i % 4 >= 2)
    def _():
        sweep(done, 2)

    done = (i // 2) * 2

    @pl.when(i % 2 == 1)
    def _():
        sweep(done, 2, last_masked=True)

    @pl.when(i % 2 == 0)
    def _():
        sweep(done, 1, last_masked=True)

    for g in groups:
        mixed_t = (part_ref[0, g] + normalise(acc_ref[g], gate_row(g, 1))) + part_ref[1, g]
        for r in range(HEADS_PER_GROUP):
            hd = HEADS_PER_GROUP * g + r
            o_ref[0, :, hd * HEAD_DIM:(hd + 1) * HEAD_DIM] = mixed_t[:, r * tq:(r + 1) * tq].T


def _nsa_call(q, gates_t, cmp, cmp_t, ksel, vsel_t, kwin, vwin_t, mcs_t):
    b, _, _, s = q.shape
    tq = NSA_TQ
    g2 = NSA_KV_GROUPS
    n_chunk = s // CMP_STRIDE
    n_slc = s // SLC_BLOCK
    rows = HEADS_PER_GROUP * tq
    tok = lambda bi, i: (bi, i, 0)
    per_batch = lambda bi, i: (bi, 0, 0, 0)
    return pl.pallas_call(
        functools.partial(_nsa_kernel, seq=s),
        grid=(b, s // tq),
        in_specs=[
            pl.BlockSpec((1, NSA_HEADS, HEAD_DIM, tq), lambda bi, i: (bi, 0, 0, i)),
            pl.BlockSpec((1, GATE_ROWS, tq), lambda bi, i: (bi, 0, i)),
            _resident((1, 2 * g2, n_chunk, HEAD_DIM), per_batch),
            _resident((1, 2 * g2, HEAD_DIM, n_chunk), per_batch),
            _resident((1, g2, s, HEAD_DIM + n_slc), per_batch),
            _resident((1, g2, V_ROWS, s), per_batch),
            _resident((1, g2, s, HEAD_DIM), per_batch),
            _resident((1, g2, V_ROWS, s), per_batch),
            _resident((n_slc, n_chunk), lambda bi, i: (0, 0)),
        ],
        out_specs=pl.BlockSpec((1, tq, NSA_WIDTH), tok),
        out_shape=jax.ShapeDtypeStruct((b, s, NSA_WIDTH), F32),
        scratch_shapes=[
            pltpu.VMEM((g2, HEAD_DIM + n_slc, rows), BF16),
            pltpu.VMEM((g2, 1, rows), F32),
            pltpu.VMEM((g2, V_ROWS, rows), F32),
            pltpu.VMEM((2, g2, HEAD_DIM, rows), F32),
        ],
        compiler_params=pltpu.CompilerParams(
            dimension_semantics=("arbitrary", "arbitrary"), vmem_limit_bytes=VMEM_LIMIT),
        name="nsa_attention",
    )(q, gates_t, cmp, cmp_t, ksel, vsel_t, kwin, vwin_t, mcs_t)


def _memkv_kernel(mem_ref, gain_ref, w_ref, o_ref):
    mem_n = _rmsnorm(mem_ref[0], gain_ref[...]).astype(BF16)
    o_ref[0, 0] = _dot(mem_n, w_ref[0]).astype(BF16)


def _memkv_call(mem, gain, w_mem):
    depth = w_mem.shape[0]
    b, m, d = mem.shape
    return pl.pallas_call(
        _memkv_kernel,
        grid=(depth, b),
        in_specs=[
            pl.BlockSpec((1, m, d), lambda l, bi: (bi, 0, 0)),
            pl.BlockSpec((1, d), lambda l, bi: (0, 0)),
            pl.BlockSpec((1, d, 2 * MEM_WIDTH), lambda l, bi: (l, 0, 0)),
        ],
        out_specs=pl.BlockSpec((1, 1, m, 2 * MEM_WIDTH), lambda l, bi: (l, bi, 0, 0)),
        out_shape=jax.ShapeDtypeStruct((depth, b, m, 2 * MEM_WIDTH), BF16),
        compiler_params=pltpu.CompilerParams(
            dimension_semantics=("arbitrary", "arbitrary"), vmem_limit_bytes=VMEM_LIMIT),
        name="mem_kv",
    )(mem, gain, w_mem)


_F_AUVZ = 0
_F_BZ = 3 * A_WIDTH
_F_CQZ = _F_BZ + NSA_WIDTH
_F_MERGE = _F_CQZ + 2 * MEM_WIDTH


def _fused_kernel(x_ref, ob_ref, gain_ref, w_ref, lng_ref, lnb_ref, wsp_ref, bsp_ref, mkv_ref,
                  wa_ref, wb_ref, wc_ref, wo_ref, *fgain_and_out, final):
    fgain_ref, o_ref = fgain_and_out if final else (None,) + fgain_and_out
    tm, d = x_ref.shape[1], x_ref.shape[2]
    x = x_ref[0]
    h = _rmsnorm(x, gain_ref[...]).astype(BF16)

    def proj(c0, width):
        return _dot(h, w_ref[:, c0:c0 + width])

    uvz = proj(_F_AUVZ, 3 * A_WIDTH)
    cqz = proj(_F_CQZ, 2 * MEM_WIDTH)
    zb = proj(_F_BZ, NSA_WIDTH)
    merge = [proj(_F_MERGE + k * d, d) for k in range(3)]

    u, v, z = uvz[:, 0:A_WIDTH], uvz[:, A_WIDTH:2 * A_WIDTH], uvz[:, 2 * A_WIDTH:]
    vc = v - jnp.mean(v, axis=-1, keepdims=True)
    vn = (vc * lax.rsqrt(jnp.mean(vc * vc, axis=-1, keepdims=True) + EPS) * lng_ref[...] + lnb_ref[...]).astype(BF16)
    gdim = A_WIDTH // A_GROUPS
    mixed = jnp.concatenate([
        jnp.concatenate([
            _dot(wsp_ref[gi], vn[c * CHUNK:(c + 1) * CHUNK, gi * gdim:(gi + 1) * gdim]) + bsp_ref[gi]
            for gi in range(A_GROUPS)], axis=1)
        for c in range(tm // CHUNK)], axis=0)

    scores = [_nt_dot(cqz[:, hd * HEAD_DIM:(hd + 1) * HEAD_DIM].astype(BF16),
                      mkv_ref[0, 0, :, hd * HEAD_DIM:(hd + 1) * HEAD_DIM]) * ATT_SCALE for hd in range(MEM_HEADS)]

    o_a = (u * mixed * _silu(z)).astype(BF16)
    acc = jax.nn.sigmoid(merge[0]) * _dot(o_a, wa_ref[...])

    o_b = (ob_ref[0] * _silu(zb)).astype(BF16)
    acc = acc + jax.nn.sigmoid(merge[1]) * _dot(o_b, wb_ref[...])

    heads = []
    for hd in range(MEM_HEADS):
        e_m = jnp.exp(scores[hd] - jnp.max(scores[hd], axis=1, keepdims=True))
        p_m = e_m * (1.0 / jnp.sum(e_m, axis=1, keepdims=True))
        heads.append(_dot(p_m.astype(BF16), mkv_ref[0, 0, :, MEM_WIDTH + hd * HEAD_DIM:MEM_WIDTH + (hd + 1) * HEAD_DIM]))
    o_c = (jnp.concatenate(heads, axis=1) * _silu(cqz[:, MEM_WIDTH:])).astype(BF16)
    acc = acc + jax.nn.sigmoid(merge[2]) * _dot(o_c, wc_ref[...])

    x_new = x + _dot(acc.astype(BF16), wo_ref[...])
    if final:
        x_new = _rmsnorm(x_new, fgain_ref[...])
    o_ref[0] = x_new


def _fused_call(x, ob, gain, w_f, lng, lnb, wsp, bsp, mkv, layer, wa, wb, wc, wo, fgain=None):
    b, s, d = x.shape
    tm = FUSE_TM
    m = mkv.shape[2]
    tok = lambda bi, i: (bi, i, 0)
    c2 = lambda bi, i: (0, 0)
    final = fgain is not None

    def layer_weights(w):
        return _resident((None,) + w.shape[1:], lambda bi, i: (layer,) + (0,) * (w.ndim - 1))

    return pl.pallas_call(
        functools.partial(_fused_kernel, final=final),
        grid=(b, s // tm),
        in_specs=[
            pl.BlockSpec((1, tm, d), tok),
            pl.BlockSpec((1, tm, NSA_WIDTH), tok),
            pl.BlockSpec((1, d), c2),
            layer_weights(w_f),
            pl.BlockSpec((1, A_WIDTH), c2),
            pl.BlockSpec((1, A_WIDTH), c2),
            layer_weights(wsp),
            layer_weights(bsp),
            _resident((1, 1, m, 2 * MEM_WIDTH), lambda bi, i: (layer, bi, 0, 0)),
            layer_weights(wa),
            layer_weights(wb),
            layer_weights(wc),
            layer_weights(wo),
        ] + ([pl.BlockSpec((1, d), c2)] if final else []),
        out_specs=pl.BlockSpec((1, tm, d), tok),
        out_shape=jax.ShapeDtypeStruct((b, s, d), F32),
        compiler_params=pltpu.CompilerParams(
            dimension_semantics=("arbitrary", "arbitrary"), vmem_limit_bytes=VMEM_LIMIT),
        name="fused_mix",
    )(x, ob, gain, w_f, lng, lnb, wsp, bsp, mkv, wa, wb, wc, wo, *([fgain] if final else []))


def _rope_lane_tables(positions):
    half = ROPE_DIM // 2
    inv_freq = ROPE_THETA ** (-jnp.arange(0, ROPE_DIM, 2, dtype=F32) / ROPE_DIM)
    ang = positions.astype(F32)[..., None] * inv_freq
    cos, sin = jnp.cos(ang), jnp.sin(ang)
    zeros = jnp.zeros(cos.shape[:-1] + (HEAD_DIM - ROPE_DIM,), F32)
    cosf = jnp.concatenate([cos, cos, zeros + 1.0], axis=-1)
    sina = jnp.concatenate([-sin, jnp.zeros_like(sin), zeros], axis=-1)
    sinb = jnp.concatenate([jnp.zeros_like(sin), sin, zeros], axis=-1)
    del half
    return cosf, sina, sinb


def _cmp_to_slc(n_chunk, n_slc):
    i = np.arange(n_chunk)[:, None] * CMP_STRIDE
    j = np.arange(n_slc)[None, :] * SLC_BLOCK
    ov = np.clip(np.minimum(i + CMP_BLOCK, j + SLC_BLOCK) - np.maximum(i, j), 0, None) / CMP_BLOCK
    ov[n_chunk - 1] = 0.0
    return jnp.asarray(ov.T, dtype=BF16)


def kernel(x, mem, positions, norm_gain, w_in, ln_v_gain, ln_v_bias, w_spatial, b_spatial, cmp_pe_k, cmp_w1_k, cmp_w2_k, cmp_pe_v, cmp_w1_v, cmp_w2_v, mem_norm_gain, w_mem_kv, w_branch_a, w_branch_b, w_branch_c, w_out, final_norm_gain):
    depth = w_in.shape[0]
    b, s, d = x.shape
    assert s % NSA_TQ == 0 and s % PROJ_TM == 0 and s >= WINDOW + NSA_TQ and d == w_in.shape[1]

    cosf, sina, sinb = _rope_lane_tables(positions)
    mcs = _cmp_to_slc(s // CMP_STRIDE, s // SLC_BLOCK)

    gate_w = jnp.pad(w_in[:, :, _B_G:_B_KV], ((0, 0), (0, 0), (0, GATE_PAD - (_B_KV - _B_G))))
    w_p = jnp.concatenate([w_in[:, :, _B_Q:_B_Z], w_in[:, :, _B_KV:_C_Q], gate_w], axis=-1).astype(BF16)
    w_f = jnp.concatenate([w_in[:, :, _A_U:_B_Q], w_in[:, :, _B_Z:_B_G], w_in[:, :, _C_Q:_MERGE],
                           w_in[:, :, _MERGE:]], axis=-1).astype(BF16)
    wsp = (w_spatial * jnp.tril(jnp.ones((CHUNK, CHUNK), w_spatial.dtype))).astype(BF16)
    bsp = jnp.broadcast_to(b_spatial[..., None], b_spatial.shape + (A_WIDTH // A_GROUPS,))
    pe = jnp.stack([cmp_pe_k, cmp_pe_v], axis=1)
    w1 = jnp.stack([cmp_w1_k, cmp_w1_v], axis=1).astype(BF16)
    w2 = jnp.stack([cmp_w2_k, cmp_w2_v], axis=1).astype(BF16)
    wa, wb, wc, wo = (w.astype(BF16) for w in (w_branch_a, w_branch_b, w_branch_c, w_out))

    mkv = _memkv_call(mem, mem_norm_gain[None, :], w_mem_kv.astype(BF16))
    fgain = final_norm_gain[None, :]
    for l in range(depth):
        gain = norm_gain[l][None, :]
        q, gates_t, cmp_in, ksel, vsel_t, kwin, vwin_t = _proj_call(x, gain, w_p, l, cosf, sina, sinb)
        cmp, cmp_t = _compress_call(cmp_in, pe[l], w1[l], w2[l])
        ob = _nsa_call(q, gates_t, cmp, cmp_t, ksel, vsel_t, kwin, vwin_t, mcs)
        x = _fused_call(x, ob, gain, w_f, ln_v_gain[l][None, :], ln_v_bias[l][None, :], wsp, bsp,
                        mkv, l, wa, wb, wc, wo, fgain if l == depth - 1 else None)
    return x
```

```python
import functools

import numpy as np
import jax
import jax.numpy as jnp
from jax import lax
from jax.experimental import pallas as pl
from jax.experimental.pallas import tpu as pltpu

F32 = jnp.float32
BF16 = jnp.bfloat16

HEAD_DIM = 128
A_GROUPS = 4
A_WIDTH = 512
CHUNK = 128
NSA_HEADS = 8
NSA_KV_GROUPS = 2
HEADS_PER_GROUP = NSA_HEADS // NSA_KV_GROUPS
NSA_WIDTH = NSA_HEADS * HEAD_DIM
CMP_BLOCK = 32
CMP_STRIDE = 16
SLC_BLOCK = 64
N_SELECT = 16
WINDOW = 512
MEM_HEADS = 4
MEM_WIDTH = MEM_HEADS * HEAD_DIM
ROPE_DIM = HEAD_DIM // 4
ROPE_THETA = 500000.0
EPS = 1e-6
NEG_INF = -1e30
FORCED_SCORE = 1e4
ATT_SCALE = HEAD_DIM ** -0.5
LOG2_E = 1.4426950408889634

_OFF = np.cumsum((0, A_WIDTH, A_WIDTH, A_WIDTH, NSA_WIDTH, NSA_WIDTH, NSA_HEADS * 3, 3 * 2 * NSA_KV_GROUPS * HEAD_DIM,
                  MEM_WIDTH, MEM_WIDTH))
(_A_U, _A_V, _A_Z, _B_Q, _B_Z, _B_G, _B_KV, _C_Q, _C_Z, _MERGE) = (int(v) for v in _OFF)
GATE_PAD = 128
GATE_ROWS = 32
V_ROWS = HEAD_DIM + 16

PROJ_TM = 512
PROJ_SPLIT = 2
NSA_TQ = 256
NSA_KC = 256
FUSE_TM = 512
FUSE_SPLIT = 2
VMEM_LIMIT = 56 * 1024 * 1024


def _nt_dot(a, b):
    return lax.dot_general(a, b, (((1,), (1,)), ((), ())), preferred_element_type=F32)


def _dot(a, b):
    return jnp.dot(a, b, preferred_element_type=F32)


def _rmsnorm(x, gain):
    return x * lax.rsqrt(jnp.mean(x * x, axis=-1, keepdims=True) + EPS) * gain


def _silu(x):
    return x * jax.nn.sigmoid(x)


def _resident(block_shape, index_map):
    return pl.BlockSpec(block_shape, index_map, pipeline_mode=pl.Buffered(1))


def _proj_kernel(x_ref, gain_ref, w_ref, cos_ref, sina_ref, sinb_ref,
                 q_ref, gate_ref, cmp_ref, ksel_ref, vsel_ref, kwin_ref, vwin_ref, *, n_slc):
    tm = x_ref.shape[1]
    th = tm // PROJ_SPLIT
    g2 = NSA_KV_GROUPS
    kv0 = NSA_HEADS
    rows = [slice(k * th, (k + 1) * th) for k in range(PROJ_SPLIT)]
    projs = [_dot(_rmsnorm(x_ref[0, r, :], gain_ref[...]).astype(BF16), w_ref[...]) for r in rows]
    ones_row = (lax.broadcasted_iota(jnp.int32, (V_ROWS - HEAD_DIM, th), 0) == 0).astype(BF16)
    for k, (r, proj) in enumerate(zip(rows, projs)):
        cosf, sina, sinb = cos_ref[0, r, :], sina_ref[0, r, :], sinb_ref[0, r, :]

        def rope(t):
            return (t * cosf + pltpu.roll(t, HEAD_DIM - ROPE_DIM // 2, 1) * sina
                    + pltpu.roll(t, ROPE_DIM // 2, 1) * sinb)

        def col(j):
            return proj[:, j * HEAD_DIM:(j + 1) * HEAD_DIM]

        for hd in range(NSA_HEADS):
            q_ref[0, hd, :, r] = (rope(col(hd)) * (ATT_SCALE * LOG2_E)).T.astype(BF16)
        row = pl.program_id(1) * tm + k * th + lax.broadcasted_iota(jnp.int32, (th, n_slc), 0)
        blk = lax.broadcasted_iota(jnp.int32, (th, n_slc), 1)
        sel_onehot = (row // SLC_BLOCK == blk).astype(BF16)
        for g in range(g2):
            cmp_ref[0, g, r, :] = rope(col(kv0 + g))
            cmp_ref[0, g2 + g, r, :] = col(kv0 + g2 + g)
            ksel_ref[0, g, r, 0:HEAD_DIM] = rope(col(kv0 + 2 * g2 + g)).astype(BF16)
            ksel_ref[0, g, r, HEAD_DIM:] = sel_onehot
            kwin_ref[0, g, r, :] = rope(col(kv0 + 4 * g2 + g)).astype(BF16)
            for v_ref, j in ((vsel_ref, kv0 + 3 * g2 + g), (vwin_ref, kv0 + 5 * g2 + g)):
                v_ref[0, g, 0:HEAD_DIM, r] = col(j).T.astype(BF16)
                v_ref[0, g, HEAD_DIM:, r] = ones_row
        gate_ref[0, :, r] = jax.nn.sigmoid(proj[:, (kv0 + 6 * g2) * HEAD_DIM:]).T[0:GATE_ROWS]


def _proj_call(x, gain, w_p, layer, cosf, sina, sinb):
    b, s, d = x.shape
    tm = PROJ_TM
    n_slc = s // SLC_BLOCK
    g2 = NSA_KV_GROUPS
    wp_cols = w_p.shape[2]
    tok = lambda bi, i: (bi, i, 0)
    grp = lambda bi, i: (bi, 0, i, 0)
    grp_t = lambda bi, i: (bi, 0, 0, i)
    return pl.pallas_call(
        functools.partial(_proj_kernel, n_slc=n_slc),
        grid=(b, s // tm),
        in_specs=[
            pl.BlockSpec((1, tm, d), tok),
            pl.BlockSpec((1, d), lambda bi, i: (0, 0)),
            _resident((None, d, wp_cols), lambda bi, i: (layer, 0, 0)),
            pl.BlockSpec((1, tm, HEAD_DIM), tok),
            pl.BlockSpec((1, tm, HEAD_DIM), tok),
            pl.BlockSpec((1, tm, HEAD_DIM), tok),
        ],
        out_specs=[
            pl.BlockSpec((1, NSA_HEADS, HEAD_DIM, tm), grp_t),
            pl.BlockSpec((1, GATE_ROWS, tm), lambda bi, i: (bi, 0, i)),
            pl.BlockSpec((1, 2 * g2, tm, HEAD_DIM), grp),
            pl.BlockSpec((1, g2, tm, HEAD_DIM + n_slc), grp),
            pl.BlockSpec((1, g2, V_ROWS, tm), grp_t),
            pl.BlockSpec((1, g2, tm, HEAD_DIM), grp),
            pl.BlockSpec((1, g2, V_ROWS, tm), grp_t),
        ],
        out_shape=[
            jax.ShapeDtypeStruct((b, NSA_HEADS, HEAD_DIM, s), BF16),
            jax.ShapeDtypeStruct((b, GATE_ROWS, s), F32),
            jax.ShapeDtypeStruct((b, 2 * g2, s, HEAD_DIM), F32),
            jax.ShapeDtypeStruct((b, g2, s, HEAD_DIM + n_slc), BF16),
            jax.ShapeDtypeStruct((b, g2, V_ROWS, s), BF16),
            jax.ShapeDtypeStruct((b, g2, s, HEAD_DIM), BF16),
            jax.ShapeDtypeStruct((b, g2, V_ROWS, s), BF16),
        ],
        compiler_params=pltpu.CompilerParams(
            dimension_semantics=("arbitrary", "arbitrary"), vmem_limit_bytes=VMEM_LIMIT),
        name="nsa_proj",
    )(x, gain, w_p, cosf, sina, sinb)


def _compress_kernel(x_ref, pe_ref, w1_ref, w2_ref, o_ref, ot_ref):
    n_chunk = o_ref.shape[2]
    first = jnp.zeros((n_chunk, HEAD_DIM), F32)
    second = jnp.zeros((n_chunk, HEAD_DIM), F32)
    for c in range(CMP_STRIDE):
        rows = x_ref[0, 0, pl.ds(c, n_chunk, stride=CMP_STRIDE), :]
        first = first + _dot((rows + pe_ref[0, c:c + 1, :]).astype(BF16), w1_ref[0, c])
        second = second + _dot((rows + pe_ref[0, CMP_STRIDE + c:CMP_STRIDE + c + 1, :]).astype(BF16),
                               w1_ref[0, CMP_STRIDE + c])
    hidden = first + pltpu.roll(second, n_chunk - 1, 0)
    out = _dot(_silu(hidden).astype(BF16), w2_ref[0])
    o_ref[0, 0] = out.astype(BF16)
    ot_ref[0, 0] = out.T.astype(BF16)


def _compress_call(cmp_in, pe, w1, w2):
    b, n4, s, dh = cmp_in.shape
    g2 = NSA_KV_GROUPS
    n_chunk = s // CMP_STRIDE
    return pl.pallas_call(
        _compress_kernel,
        grid=(b, n4),
        in_specs=[
            pl.BlockSpec((1, 1, s, dh), lambda bi, j: (bi, j, 0, 0)),
            pl.BlockSpec((1, CMP_BLOCK, dh), lambda bi, j: (j // g2, 0, 0)),
            pl.BlockSpec((1, CMP_BLOCK, dh, dh), lambda bi, j: (j // g2, 0, 0, 0)),
            pl.BlockSpec((1, dh, dh), lambda bi, j: (j // g2, 0, 0)),
        ],
        out_specs=[pl.BlockSpec((1, 1, n_chunk, dh), lambda bi, j: (bi, j, 0, 0)),
                   pl.BlockSpec((1, 1, dh, n_chunk), lambda bi, j: (bi, j, 0, 0))],
        out_shape=[jax.ShapeDtypeStruct((b, n4, n_chunk, dh), BF16),
                   jax.ShapeDtypeStruct((b, n4, dh, n_chunk), BF16)],
        compiler_params=pltpu.CompilerParams(
            dimension_semantics=("arbitrary", "arbitrary"), vmem_limit_bytes=VMEM_LIMIT),
        name="nsa_compress",
    )(cmp_in, pe, w1, w2)


def _nsa_kernel(q_ref, gate_ref, cmp_ref, cmpt_ref, ksel_ref, vsel_ref, kwin_ref, vwin_ref, mcs_ref, o_ref,
                lhs_ref, m_ref, acc_ref, part_ref, *, seq):
    tq = q_ref.shape[3]
    assert tq == NSA_KC
    rows = HEADS_PER_GROUP * tq
    half = rows // 2
    n_chunk = seq // CMP_STRIDE
    n_cmp = n_chunk - 1
    n_slc = seq // SLC_BLOCK
    groups = range(NSA_KV_GROUPS)
    i = pl.program_id(1)
    t0 = i * tq
    tok = t0 + lax.broadcasted_iota(jnp.int32, (1, tq), 1)
    tpos = jnp.concatenate([tok] * HEADS_PER_GROUP, axis=1)

    def mask_heads(s, mask):
        return jnp.concatenate(
            [jnp.where(mask, s[:, r * tq:(r + 1) * tq], NEG_INF) for r in range(s.shape[1] // tq)], axis=1)

    def gate_row(g, branch):
        return jnp.concatenate(
            [gate_ref[0, 3 * (HEADS_PER_GROUP * g + r) + branch:3 * (HEADS_PER_GROUP * g + r) + branch + 1, :]
             for r in range(HEADS_PER_GROUP)], axis=1)

    def normalise(acc_t, gate):
        return acc_t[0:HEAD_DIM] * (gate / acc_t[HEAD_DIM:HEAD_DIM + 1])

    qgs = [jnp.concatenate([q_ref[0, HEADS_PER_GROUP * g + r] for r in range(HEADS_PER_GROUP)], axis=1)
           for g in groups]

    nrow = lax.broadcasted_iota(jnp.int32, (n_chunk, 1), 0)
    cmask = nrow <= jnp.minimum((tok - (CMP_BLOCK - 1)) // CMP_STRIDE, n_cmp - 1)
    sees_block = tpos >= CMP_BLOCK - 1
    imps = [None] * NSA_KV_GROUPS

    def run_pipelined(items):
        scores = [score_fn() for score_fn, _ in items]
        for k, (_, consume_fn) in enumerate(items):
            consume_fn(scores[k])
            scores[k] = None

    def cmp_consume(g, s_c):
        s_c = mask_heads(s_c, cmask)
        e_c = jnp.exp2(s_c - jnp.max(s_c, axis=0, keepdims=True))
        p_c = e_c * jnp.where(sees_block, 1.0 / jnp.sum(e_c, axis=0, keepdims=True), 0.0)
        part_ref[0, g] = _dot(cmpt_ref[0, NSA_KV_GROUPS + g], p_c.astype(BF16)) * gate_row(g, 0)
        p_sum = p_c[:, 0:tq]
        for r in range(1, HEADS_PER_GROUP):
            p_sum = p_sum + p_c[:, r * tq:(r + 1) * tq]
        p_hi = p_sum.astype(BF16)
        p_lo = (p_sum - p_hi.astype(F32)).astype(BF16)
        mcs_t = mcs_ref[...]
        imps[g] = _dot(mcs_t, p_hi) + _dot(mcs_t, p_lo)

    wlen = WINDOW + tq
    w0 = pl.multiple_of(jnp.maximum(t0 - WINDOW, 0), tq)
    rel = tok - (w0 + lax.broadcasted_iota(jnp.int32, (wlen, 1), 0))
    wmask = lax.bitcast_convert_type(rel, jnp.uint32) < WINDOW

    def win_consume(g, s_w):
        s_w = mask_heads(s_w, wmask)
        p_w = jnp.exp2(s_w - jnp.max(s_w, axis=0, keepdims=True)).astype(BF16)
        part_ref[1, g] = normalise(_dot(vwin_ref[0, g, :, pl.ds(w0, wlen)], p_w), gate_row(g, 2))

    run_pipelined(
        [(functools.partial(_dot, cmp_ref[0, g], qgs[g]), functools.partial(cmp_consume, g)) for g in groups]
        + [(functools.partial(_dot, kwin_ref[0, g, pl.ds(w0, wlen), :], qgs[g]), functools.partial(win_consume, g))
           for g in groups])

    jrow = lax.broadcasted_iota(jnp.int32, (n_slc, tq), 0)
    cur = (t0 + lax.broadcasted_iota(jnp.int32, (n_slc, tq), 1)) // SLC_BLOCK
    forced = (jrow == 0) | (jrow == cur) | (jrow == cur - 1)
    for g in groups:
        cand = jnp.where(forced, -jnp.inf, jnp.where(jrow <= cur, imps[g], -1.0))
        for _ in range(min(N_SELECT, n_slc) - 3):
            best = jnp.max(cand, axis=0, keepdims=True)
            first = jnp.min(jnp.where(cand == best, jrow, n_slc), axis=0, keepdims=True)
            cand = jnp.where(jrow == first, -jnp.inf, cand)
        sel_bias = jnp.where(cand == -jnp.inf, 0.0, NEG_INF).astype(BF16)
        lhs_ref[g, 0:HEAD_DIM, :] = qgs[g]
        lhs_ref[g, HEAD_DIM:, :] = jnp.concatenate([sel_bias] * HEADS_PER_GROUP, axis=1)
        m_ref[g] = jnp.full((1, rows), NEG_INF, F32)
        acc_ref[g] = jnp.zeros((V_ROWS, rows), F32)

    def sel_score(g, hf, k0, width):
        return _dot(ksel_ref[0, g, pl.ds(k0, width), :], lhs_ref[g, :, hf * half:(hf + 1) * half])

    def sel_consume(g, hf, k0, width, masked, s_s):
        lanes = slice(hf * half, (hf + 1) * half)
        if masked:
            kpos = k0 + lax.broadcasted_iota(jnp.int32, (width, 1), 0)
            s_s = mask_heads(s_s, kpos <= tok)
        m_old = m_ref[g, :, lanes]
        m_new = jnp.maximum(m_old, jnp.max(s_s, axis=0, keepdims=True))
        p_s = jnp.exp2(s_s - m_new).astype(BF16)
        acc_ref[g, :, lanes] = (jnp.exp2(m_old - m_new) * acc_ref[g, :, lanes]
                                + _dot(vsel_ref[0, g, :, pl.ds(k0, width)], p_s))
        m_ref[g, :, lanes] = m_new

    def sweep(first_chunk, n_chunks, last_masked=False):
        k0 = pl.multiple_of(first_chunk * NSA_KC, NSA_KC)
        width = n_chunks * NSA_KC
        run_pipelined([(functools.partial(sel_score, g, hf, k0, width),
                        functools.partial(sel_consume, g, hf, k0, width, last_masked))
                       for g in groups for hf in range(2)])

    def octet_body(ko, carry):
        sweep(8 * ko, 8)
        return carry

    lax.fori_loop(0, i // 8, octet_body, 0)
    done = (i // 8) * 8

    @pl.when(i % 8 >= 4)
    def _():
        sweep(done, 4)

    done = (i // 4) * 4

    @pl.when(i % 4 >= 2)
    def _():
        sweep(done, 2)

    done = (i // 2) * 2

    @pl.when(i % 2 == 1)
    def _():
        sweep(done, 2, last_masked=True)

    @pl.when(i % 2 == 0)
    def _():
        sweep(done, 1, last_masked=True)

    for g in groups:
        mixed_t = (part_ref[0, g] + normalise(acc_ref[g], gate_row(g, 1))) + part_ref[1, g]
        for r in range(HEADS_PER_GROUP):
            hd = HEADS_PER_GROUP * g + r
            o_ref[0, :, hd * HEAD_DIM:(hd + 1) * HEAD_DIM] = mixed_t[:, r * tq:(r + 1) * tq].T


def _nsa_call(q, gates_t, cmp, cmp_t, ksel, vsel_t, kwin, vwin_t, mcs_t):
    b, _, _, s = q.shape
    tq = NSA_TQ
    g2 = NSA_KV_GROUPS
    n_chunk = s // CMP_STRIDE
    n_slc = s // SLC_BLOCK
    rows = HEADS_PER_GROUP * tq
    tok = lambda bi, i: (bi, i, 0)
    per_batch = lambda bi, i: (bi, 0, 0, 0)
    return pl.pallas_call(
        functools.partial(_nsa_kernel, seq=s),
        grid=(b, s // tq),
        in_specs=[
            pl.BlockSpec((1, NSA_HEADS, HEAD_DIM, tq), lambda bi, i: (bi, 0, 0, i)),
            pl.BlockSpec((1, GATE_ROWS, tq), lambda bi, i: (bi, 0, i)),
            _resident((1, 2 * g2, n_chunk, HEAD_DIM), per_batch),
            _resident((1, 2 * g2, HEAD_DIM, n_chunk), per_batch),
            _resident((1, g2, s, HEAD_DIM + n_slc), per_batch),
            _resident((1, g2, V_ROWS, s), per_batch),
            _resident((1, g2, s, HEAD_DIM), per_batch),
            _resident((1, g2, V_ROWS, s), per_batch),
            _resident((n_slc, n_chunk), lambda bi, i: (0, 0)),
        ],
        out_specs=pl.BlockSpec((1, tq, NSA_WIDTH), tok),
        out_shape=jax.ShapeDtypeStruct((b, s, NSA_WIDTH), F32),
        scratch_shapes=[
            pltpu.VMEM((g2, HEAD_DIM + n_slc, rows), BF16),
            pltpu.VMEM((g2, 1, rows), F32),
            pltpu.VMEM((g2, V_ROWS, rows), F32),
            pltpu.VMEM((2, g2, HEAD_DIM, rows), F32),
        ],
        compiler_params=pltpu.CompilerParams(
            dimension_semantics=("arbitrary", "arbitrary"), vmem_limit_bytes=VMEM_LIMIT),
        name="nsa_attention",
    )(q, gates_t, cmp, cmp_t, ksel, vsel_t, kwin, vwin_t, mcs_t)


def _memkv_kernel(mem_ref, gain_ref, w_ref, o_ref):
    mem_n = _rmsnorm(mem_ref[0], gain_ref[...]).astype(BF16)
    o_ref[0, 0] = _dot(mem_n, w_ref[0]).astype(BF16)


def _memkv_call(mem, gain, w_mem):
    depth = w_mem.shape[0]
    b, m, d = mem.shape
    return pl.pallas_call(
        _memkv_kernel,
        grid=(depth, b),
        in_specs=[
            pl.BlockSpec((1, m, d), lambda l, bi: (bi, 0, 0)),
            pl.BlockSpec((1, d), lambda l, bi: (0, 0)),
            pl.BlockSpec((1, d, 2 * MEM_WIDTH), lambda l, bi: (l, 0, 0)),
        ],
        out_specs=pl.BlockSpec((1, 1, m, 2 * MEM_WIDTH), lambda l, bi: (l, bi, 0, 0)),
        out_shape=jax.ShapeDtypeStruct((depth, b, m, 2 * MEM_WIDTH), BF16),
        compiler_params=pltpu.CompilerParams(
            dimension_semantics=("arbitrary", "arbitrary"), vmem_limit_bytes=VMEM_LIMIT),
        name="mem_kv",
    )(mem, gain, w_mem)


_F_AUVZ = 0
_F_BZ = 3 * A_WIDTH
_F_CQZ = _F_BZ + NSA_WIDTH
_F_MERGE = _F_CQZ + 2 * MEM_WIDTH


def _fused_kernel(x_ref, ob_ref, gain_ref, w_ref, lng_ref, lnb_ref, wsp_ref, bsp_ref, mkv_ref,
                  wa_ref, wb_ref, wc_ref, wo_ref, *fgain_and_out, final):
    fgain_ref, o_ref = fgain_and_out if final else (None,) + fgain_and_out
    tm, d = x_ref.shape[1], x_ref.shape[2]
    th = tm // FUSE_SPLIT

    def front(x):
        h = _rmsnorm(x, gain_ref[...]).astype(BF16)

        def proj(c0, width):
            return _dot(h, w_ref[:, c0:c0 + width])

        return (proj(_F_AUVZ, 3 * A_WIDTH), proj(_F_CQZ, 2 * MEM_WIDTH), proj(_F_BZ, NSA_WIDTH),
                [proj(_F_MERGE + k * d, d) for k in range(3)])

    def back(x, ob, uvz, cqz, zb, merge):
        u, v, z = uvz[:, 0:A_WIDTH], uvz[:, A_WIDTH:2 * A_WIDTH], uvz[:, 2 * A_WIDTH:]
        vc = v - jnp.mean(v, axis=-1, keepdims=True)
        vn = (vc * lax.rsqrt(jnp.mean(vc * vc, axis=-1, keepdims=True) + EPS) * lng_ref[...]
              + lnb_ref[...]).astype(BF16)
        gdim = A_WIDTH // A_GROUPS
        mixed = jnp.concatenate([
            jnp.concatenate([
                _dot(wsp_ref[gi], vn[c * CHUNK:(c + 1) * CHUNK, gi * gdim:(gi + 1) * gdim]) + bsp_ref[gi]
                for gi in range(A_GROUPS)], axis=1)
            for c in range(th // CHUNK)], axis=0)

        scores = [_nt_dot(cqz[:, hd * HEAD_DIM:(hd + 1) * HEAD_DIM].astype(BF16),
                          mkv_ref[0, 0, :, hd * HEAD_DIM:(hd + 1) * HEAD_DIM]) * ATT_SCALE
                  for hd in range(MEM_HEADS)]

        o_a = (u * mixed * _silu(z)).astype(BF16)
        acc = jax.nn.sigmoid(merge[0]) * _dot(o_a, wa_ref[...])

        o_b = (ob * _silu(zb)).astype(BF16)
        acc = acc + jax.nn.sigmoid(merge[1]) * _dot(o_b, wb_ref[...])

        heads = []
        for hd in range(MEM_HEADS):
            e_m = jnp.exp(scores[hd] - jnp.max(scores[hd], axis=1, keepdims=True))
            p_m = e_m * (1.0 / jnp.sum(e_m, axis=1, keepdims=True))
            heads.append(_dot(p_m.astype(BF16),
                              mkv_ref[0, 0, :, MEM_WIDTH + hd * HEAD_DIM:MEM_WIDTH + (hd + 1) * HEAD_DIM]))
        o_c = (jnp.concatenate(heads, axis=1) * _silu(cqz[:, MEM_WIDTH:])).astype(BF16)
        acc = acc + jax.nn.sigmoid(merge[2]) * _dot(o_c, wc_ref[...])

        x_new = x + _dot(acc.astype(BF16), wo_ref[...])
        if final:
            x_new = _rmsnorm(x_new, fgain_ref[...])
        return x_new

    rows = [slice(k * th, (k + 1) * th) for k in range(FUSE_SPLIT)]
    fronts = [front(x_ref[0, r, :]) for r in rows]
    for r, f in zip(rows, fronts):
        o_ref[0, r, :] = back(x_ref[0, r, :], ob_ref[0, r, :], *f)


def _fused_call(x, ob, gain, w_f, lng, lnb, wsp, bsp, mkv, layer, wa, wb, wc, wo, fgain=None):
    b, s, d = x.shape
    tm = FUSE_TM
    m = mkv.shape[2]
    tok = lambda bi, i: (bi, i, 0)
    c2 = lambda bi, i: (0, 0)
    final = fgain is not None

    def layer_weights(w):
        return _resident((None,) + w.shape[1:], lambda bi, i: (layer,) + (0,) * (w.ndim - 1))

    return pl.pallas_call(
        functools.partial(_fused_kernel, final=final),
        grid=(b, s // tm),
        in_specs=[
            pl.BlockSpec((1, tm, d), tok),
            pl.BlockSpec((1, tm, NSA_WIDTH), tok),
            pl.BlockSpec((1, d), c2),
            layer_weights(w_f),
            pl.BlockSpec((1, A_WIDTH), c2),
            pl.BlockSpec((1, A_WIDTH), c2),
            layer_weights(wsp),
            layer_weights(bsp),
            _resident((1, 1, m, 2 * MEM_WIDTH), lambda bi, i: (layer, bi, 0, 0)),
            layer_weights(wa),
            layer_weights(wb),
            layer_weights(wc),
            layer_weights(wo),
        ] + ([pl.BlockSpec((1, d), c2)] if final else []),
        out_specs=pl.BlockSpec((1, tm, d), tok),
        out_shape=jax.ShapeDtypeStruct((b, s, d), F32),
        compiler_params=pltpu.CompilerParams(
            dimension_semantics=("arbitrary", "arbitrary"), vmem_limit_bytes=VMEM_LIMIT),
        name="fused_mix",
    )(x, ob, gain, w_f, lng, lnb, wsp, bsp, mkv, wa, wb, wc, wo, *([fgain] if final else []))


def _rope_lane_tables(positions):
    inv_freq = ROPE_THETA ** (-jnp.arange(0, ROPE_DIM, 2, dtype=F32) / ROPE_DIM)
    ang = positions.astype(F32)[..., None] * inv_freq
    cos, sin = jnp.cos(ang), jnp.sin(ang)
    zeros = jnp.zeros(cos.shape[:-1] + (HEAD_DIM - ROPE_DIM,), F32)
    cosf = jnp.concatenate([cos, cos, zeros + 1.0], axis=-1)
    sina = jnp.concatenate([-sin, jnp.zeros_like(sin), zeros], axis=-1)
    sinb = jnp.concatenate([jnp.zeros_like(sin), sin, zeros], axis=-1)
    return cosf, sina, sinb


def _cmp_to_slc(n_chunk, n_slc):
    i = np.arange(n_chunk)[:, None] * CMP_STRIDE
    j = np.arange(n_slc)[None, :] * SLC_BLOCK
    ov = np.clip(np.minimum(i + CMP_BLOCK, j + SLC_BLOCK) - np.maximum(i, j), 0, None) / CMP_BLOCK
    ov[n_chunk - 1] = 0.0
    return jnp.asarray(ov.T, dtype=BF16)


def kernel(x, mem, positions, norm_gain, w_in, ln_v_gain, ln_v_bias, w_spatial, b_spatial, cmp_pe_k, cmp_w1_k, cmp_w2_k, cmp_pe_v, cmp_w1_v, cmp_w2_v, mem_norm_gain, w_mem_kv, w_branch_a, w_branch_b, w_branch_c, w_out, final_norm_gain):
    depth = w_in.shape[0]
    b, s, d = x.shape
    assert s % NSA_TQ == 0 and s % PROJ_TM == 0 and s >= WINDOW + NSA_TQ and d == w_in.shape[1]

    cosf, sina, sinb = _rope_lane_tables(positions)
    mcs = _cmp_to_slc(s // CMP_STRIDE, s // SLC_BLOCK)

    gate_w = jnp.pad(w_in[:, :, _B_G:_B_KV], ((0, 0), (0, 0), (0, GATE_PAD - (_B_KV - _B_G))))
    w_p = jnp.concatenate([w_in[:, :, _B_Q:_B_Z], w_in[:, :, _B_KV:_C_Q], gate_w], axis=-1).astype(BF16)
    w_f = jnp.concatenate([w_in[:, :, _A_U:_B_Q], w_in[:, :, _B_Z:_B_G], w_in[:, :, _C_Q:_MERGE],
                           w_in[:, :, _MERGE:]], axis=-1).astype(BF16)
    wsp = (w_spatial * jnp.tril(jnp.ones((CHUNK, CHUNK), w_spatial.dtype))).astype(BF16)
    bsp = jnp.broadcast_to(b_spatial[..., None], b_spatial.shape + (A_WIDTH // A_GROUPS,))
    pe = jnp.stack([cmp_pe_k, cmp_pe_v], axis=1)
    w1 = jnp.stack([cmp_w1_k, cmp_w1_v], axis=1).astype(BF16)
    w2 = jnp.stack([cmp_w2_k, cmp_w2_v], axis=1).astype(BF16)
    wa, wb, wc, wo = (w.astype(BF16) for w in (w_branch_a, w_branch_b, w_branch_c, w_out))

    mkv = _memkv_call(mem, mem_norm_gain[None, :], w_mem_kv.astype(BF16))
    fgain = final_norm_gain[None, :]
    for l in range(depth):
        gain = norm_gain[l][None, :]
        q, gates_t, cmp_in, ksel, vsel_t, kwin, vwin_t = _proj_call(x, gain, w_p, l, cosf, sina, sinb)
        cmp, cmp_t = _compress_call(cmp_in, pe[l], w1[l], w2[l])
        ob = _nsa_call(q, gates_t, cmp, cmp_t, ksel, vsel_t, kwin, vwin_t, mcs)
        x = _fused_call(x, ob, gain, w_f, ln_v_gain[l][None, :], ln_v_bias[l][None, :], wsp, bsp,
                        mkv, l, wa, wb, wc, wo, fgain if l == depth - 1 else None)
    return x
```

```python
import functools

import numpy as np
import jax
import jax.numpy as jnp
from jax import lax
from jax.experimental import pallas as pl
from jax.experimental.pallas import tpu as pltpu

F32 = jnp.float32
BF16 = jnp.bfloat16

HEAD_DIM = 128
A_GROUPS = 4
A_WIDTH = 512
CHUNK = 128
NSA_HEADS = 8
NSA_KV_GROUPS = 2
HEADS_PER_GROUP = NSA_HEADS // NSA_KV_GROUPS
NSA_WIDTH = NSA_HEADS * HEAD_DIM
CMP_BLOCK = 32
CMP_STRIDE = 16
SLC_BLOCK = 64
N_SELECT = 16
WINDOW = 512
MEM_HEADS = 4
MEM_WIDTH = MEM_HEADS * HEAD_DIM
ROPE_DIM = HEAD_DIM // 4
ROPE_THETA = 500000.0
EPS = 1e-6
NEG_INF = -1e30
FORCED_SCORE = 1e4
ATT_SCALE = HEAD_DIM ** -0.5
LOG2_E = 1.4426950408889634

_OFF = np.cumsum((0, A_WIDTH, A_WIDTH, A_WIDTH, NSA_WIDTH, NSA_WIDTH, NSA_HEADS * 3, 3 * 2 * NSA_KV_GROUPS * HEAD_DIM,
                  MEM_WIDTH, MEM_WIDTH))
(_A_U, _A_V, _A_Z, _B_Q, _B_Z, _B_G, _B_KV, _C_Q, _C_Z, _MERGE) = (int(v) for v in _OFF)
GATE_PAD = 128
GATE_ROWS = 32
V_ROWS = HEAD_DIM + 16

PROJ_TM = 512
PROJ_SPLIT = 2
NSA_TQ = 256
NSA_KC = 256
FUSE_TM = 512
FUSE_SPLIT = 2
VMEM_LIMIT = 56 * 1024 * 1024


def _nt_dot(a, b):
    return lax.dot_general(a, b, (((1,), (1,)), ((), ())), preferred_element_type=F32)


def _dot(a, b):
    return jnp.dot(a, b, preferred_element_type=F32)


def _rmsnorm(x, gain):
    return x * lax.rsqrt(jnp.mean(x * x, axis=-1, keepdims=True) + EPS) * gain


def _silu(x):
    return x * jax.nn.sigmoid(x)


def _resident(block_shape, index_map):
    return pl.BlockSpec(block_shape, index_map, pipeline_mode=pl.Buffered(1))


def _proj_kernel(x_ref, gain_ref, w_ref, cos_ref, sina_ref, sinb_ref,
                 q_ref, gate_ref, cmp_ref, ksel_ref, vsel_ref, kwin_ref, vwin_ref, *, n_slc):
    tm = x_ref.shape[1]
    th = tm // PROJ_SPLIT
    g2 = NSA_KV_GROUPS
    kv0 = NSA_HEADS
    rows = [slice(k * th, (k + 1) * th) for k in range(PROJ_SPLIT)]
    projs = [_dot(_rmsnorm(x_ref[0, r, :], gain_ref[...]).astype(BF16), w_ref[...]) for r in rows]
    ones_row = (lax.broadcasted_iota(jnp.int32, (V_ROWS - HEAD_DIM, th), 0) == 0).astype(BF16)
    for k, (r, proj) in enumerate(zip(rows, projs)):
        cosf, sina, sinb = cos_ref[0, r, :], sina_ref[0, r, :], sinb_ref[0, r, :]

        def rope(t):
            return (t * cosf + pltpu.roll(t, HEAD_DIM - ROPE_DIM // 2, 1) * sina
                    + pltpu.roll(t, ROPE_DIM // 2, 1) * sinb)

        def col(j):
            return proj[:, j * HEAD_DIM:(j + 1) * HEAD_DIM]

        for hd in range(NSA_HEADS):
            q_ref[0, hd, :, r] = (rope(col(hd)) * (ATT_SCALE * LOG2_E)).T.astype(BF16)
        row = pl.program_id(1) * tm + k * th + lax.broadcasted_iota(jnp.int32, (th, n_slc), 0)
        blk = lax.broadcasted_iota(jnp.int32, (th, n_slc), 1)
        sel_onehot = (row // SLC_BLOCK == blk).astype(BF16)
        for g in range(g2):
            cmp_ref[0, g, r, :] = rope(col(kv0 + g))
            cmp_ref[0, g2 + g, r, :] = col(kv0 + g2 + g)
            ksel_ref[0, g, r, 0:HEAD_DIM] = rope(col(kv0 + 2 * g2 + g)).astype(BF16)
            ksel_ref[0, g, r, HEAD_DIM:] = sel_onehot
            kwin_ref[0, g, r, :] = rope(col(kv0 + 4 * g2 + g)).astype(BF16)
            for v_ref, j in ((vsel_ref, kv0 + 3 * g2 + g), (vwin_ref, kv0 + 5 * g2 + g)):
                v_ref[0, g, 0:HEAD_DIM, r] = col(j).T.astype(BF16)
                v_ref[0, g, HEAD_DIM:, r] = ones_row
        gate_ref[0, :, r] = jax.nn.sigmoid(proj[:, (kv0 + 6 * g2) * HEAD_DIM:]).T[0:GATE_ROWS]


def _proj_call(x, gain, w_p, layer, cosf, sina, sinb):
    b, s, d = x.shape
    tm = PROJ_TM
    n_slc = s // SLC_BLOCK
    g2 = NSA_KV_GROUPS
    wp_cols = w_p.shape[2]
    tok = lambda bi, i: (bi, i, 0)
    grp = lambda bi, i: (bi, 0, i, 0)
    grp_t = lambda bi, i: (bi, 0, 0, i)
    return pl.pallas_call(
        functools.partial(_proj_kernel, n_slc=n_slc),
        grid=(b, s // tm),
        in_specs=[
            pl.BlockSpec((1, tm, d), tok),
            pl.BlockSpec((1, d), lambda bi, i: (0, 0)),
            _resident((None, d, wp_cols), lambda bi, i: (layer, 0, 0)),
            pl.BlockSpec((1, tm, HEAD_DIM), tok),
            pl.BlockSpec((1, tm, HEAD_DIM), tok),
            pl.BlockSpec((1, tm, HEAD_DIM), tok),
        ],
        out_specs=[
            pl.BlockSpec((1, NSA_HEADS, HEAD_DIM, tm), grp_t),
            pl.BlockSpec((1, GATE_ROWS, tm), lambda bi, i: (bi, 0, i)),
            pl.BlockSpec((1, 2 * g2, tm, HEAD_DIM), grp),
            pl.BlockSpec((1, g2, tm, HEAD_DIM + n_slc), grp),
            pl.BlockSpec((1, g2, V_ROWS, tm), grp_t),
            pl.BlockSpec((1, g2, tm, HEAD_DIM), grp),
            pl.BlockSpec((1, g2, V_ROWS, tm), grp_t),
        ],
        out_shape=[
            jax.ShapeDtypeStruct((b, NSA_HEADS, HEAD_DIM, s), BF16),
            jax.ShapeDtypeStruct((b, GATE_ROWS, s), F32),
            jax.ShapeDtypeStruct((b, 2 * g2, s, HEAD_DIM), F32),
            jax.ShapeDtypeStruct((b, g2, s, HEAD_DIM + n_slc), BF16),
            jax.ShapeDtypeStruct((b, g2, V_ROWS, s), BF16),
            jax.ShapeDtypeStruct((b, g2, s, HEAD_DIM), BF16),
            jax.ShapeDtypeStruct((b, g2, V_ROWS, s), BF16),
        ],
        compiler_params=pltpu.CompilerParams(
            dimension_semantics=("arbitrary", "arbitrary"), vmem_limit_bytes=VMEM_LIMIT),
        name="nsa_proj",
    )(x, gain, w_p, cosf, sina, sinb)


def _compress_kernel(x_ref, pe_ref, w1_ref, w2_ref, o_ref, ot_ref):
    n_chunk = o_ref.shape[2]
    first = jnp.zeros((n_chunk, HEAD_DIM), F32)
    second = jnp.zeros((n_chunk, HEAD_DIM), F32)
    for c in range(CMP_STRIDE):
        rows = x_ref[0, 0, pl.ds(c, n_chunk, stride=CMP_STRIDE), :]
        first = first + _dot((rows + pe_ref[0, c:c + 1, :]).astype(BF16), w1_ref[0, c])
        second = second + _dot((rows + pe_ref[0, CMP_STRIDE + c:CMP_STRIDE + c + 1, :]).astype(BF16),
                               w1_ref[0, CMP_STRIDE + c])
    hidden = first + pltpu.roll(second, n_chunk - 1, 0)
    out = _dot(_silu(hidden).astype(BF16), w2_ref[0])
    o_ref[0, 0] = out.astype(BF16)
    ot_ref[0, 0] = out.T.astype(BF16)


def _compress_call(cmp_in, pe, w1, w2):
    b, n4, s, dh = cmp_in.shape
    g2 = NSA_KV_GROUPS
    n_chunk = s // CMP_STRIDE
    return pl.pallas_call(
        _compress_kernel,
        grid=(b, n4),
        in_specs=[
            pl.BlockSpec((1, 1, s, dh), lambda bi, j: (bi, j, 0, 0)),
            pl.BlockSpec((1, CMP_BLOCK, dh), lambda bi, j: (j // g2, 0, 0)),
            pl.BlockSpec((1, CMP_BLOCK, dh, dh), lambda bi, j: (j // g2, 0, 0, 0)),
            pl.BlockSpec((1, dh, dh), lambda bi, j: (j // g2, 0, 0)),
        ],
        out_specs=[pl.BlockSpec((1, 1, n_chunk, dh), lambda bi, j: (bi, j, 0, 0)),
                   pl.BlockSpec((1, 1, dh, n_chunk), lambda bi, j: (bi, j, 0, 0))],
        out_shape=[jax.ShapeDtypeStruct((b, n4, n_chunk, dh), BF16),
                   jax.ShapeDtypeStruct((b, n4, dh, n_chunk), BF16)],
        compiler_params=pltpu.CompilerParams(
            dimension_semantics=("arbitrary", "arbitrary"), vmem_limit_bytes=VMEM_LIMIT),
        name="nsa_compress",
    )(cmp_in, pe, w1, w2)


def _nsa_kernel(q_ref, gate_ref, cmp_ref, cmpt_ref, ksel_ref, vsel_ref, kwin_ref, vwin_ref, mcs_ref, o_ref,
                lhs_ref, m_ref, acc_ref, part_ref, *, seq):
    tq = q_ref.shape[3]
    assert tq == NSA_KC
    rows = HEADS_PER_GROUP * tq
    half = rows // 2
    n_chunk = seq // CMP_STRIDE
    n_cmp = n_chunk - 1
    n_slc = seq // SLC_BLOCK
    groups = range(NSA_KV_GROUPS)
    i = pl.program_id(1)
    t0 = i * tq
    tok = t0 + lax.broadcasted_iota(jnp.int32, (1, tq), 1)
    tpos = jnp.concatenate([tok] * HEADS_PER_GROUP, axis=1)

    def mask_bias(mask):
        return jnp.where(mask, 0.0, NEG_INF)

    def mask_heads(s, bias):
        return jnp.concatenate([s[:, r * tq:(r + 1) * tq] + bias for r in range(s.shape[1] // tq)], axis=1)

    def gate_row(g, branch):
        return jnp.concatenate(
            [gate_ref[0, 3 * (HEADS_PER_GROUP * g + r) + branch:3 * (HEADS_PER_GROUP * g + r) + branch + 1, :]
             for r in range(HEADS_PER_GROUP)], axis=1)

    def normalise(acc_t, gate):
        return acc_t[0:HEAD_DIM] * (gate / acc_t[HEAD_DIM:HEAD_DIM + 1])

    qgs = [jnp.concatenate([q_ref[0, HEADS_PER_GROUP * g + r] for r in range(HEADS_PER_GROUP)], axis=1)
           for g in groups]

    nrow = lax.broadcasted_iota(jnp.int32, (n_chunk, 1), 0)
    cmp_bias = mask_bias(nrow <= jnp.minimum((tok - (CMP_BLOCK - 1)) // CMP_STRIDE, n_cmp - 1))
    sees_block = tpos >= CMP_BLOCK - 1
    imps = [None] * NSA_KV_GROUPS

    def run_pipelined(items):
        scores = [score_fn() for score_fn, _ in items]
        for k, (_, consume_fn) in enumerate(items):
            consume_fn(scores[k])
            scores[k] = None

    def cmp_consume(g, s_c):
        s_c = mask_heads(s_c, cmp_bias)
        e_c = jnp.exp2(s_c - jnp.max(s_c, axis=0, keepdims=True))
        p_c = e_c * jnp.where(sees_block, 1.0 / jnp.sum(e_c, axis=0, keepdims=True), 0.0)
        part_ref[0, g] = _dot(cmpt_ref[0, NSA_KV_GROUPS + g], p_c.astype(BF16)) * gate_row(g, 0)
        p_sum = p_c[:, 0:tq]
        for r in range(1, HEADS_PER_GROUP):
            p_sum = p_sum + p_c[:, r * tq:(r + 1) * tq]
        p_hi = p_sum.astype(BF16)
        p_lo = (p_sum - p_hi.astype(F32)).astype(BF16)
        mcs_t = mcs_ref[...]
        imps[g] = _dot(mcs_t, p_hi) + _dot(mcs_t, p_lo)

    wlen = WINDOW + tq
    w0 = pl.multiple_of(jnp.maximum(t0 - WINDOW, 0), tq)
    rel = tok - (w0 + lax.broadcasted_iota(jnp.int32, (wlen, 1), 0))
    win_bias = mask_bias(lax.bitcast_convert_type(rel, jnp.uint32) < WINDOW)

    def win_consume(g, s_w):
        s_w = mask_heads(s_w, win_bias)
        p_w = jnp.exp2(s_w - jnp.max(s_w, axis=0, keepdims=True)).astype(BF16)
        part_ref[1, g] = normalise(_dot(vwin_ref[0, g, :, pl.ds(w0, wlen)], p_w), gate_row(g, 2))

    run_pipelined(
        [(functools.partial(_dot, cmp_ref[0, g], qgs[g]), functools.partial(cmp_consume, g)) for g in groups]
        + [(functools.partial(_dot, kwin_ref[0, g, pl.ds(w0, wlen), :], qgs[g]), functools.partial(win_consume, g))
           for g in groups])

    jrow = lax.broadcasted_iota(jnp.int32, (n_slc, tq), 0)
    cur = (t0 + lax.broadcasted_iota(jnp.int32, (n_slc, tq), 1)) // SLC_BLOCK
    forced = (jrow == 0) | (jrow == cur) | (jrow == cur - 1)
    jrow_f = jrow.astype(F32)
    for g in groups:
        cand = jnp.where(forced, -jnp.inf, jnp.where(jrow <= cur, imps[g], -1.0))
        for _ in range(min(N_SELECT, n_slc) - 3):
            best = jnp.max(cand, axis=0, keepdims=True)
            first = jnp.min(jnp.where(cand == best, jrow_f, float(n_slc)), axis=0, keepdims=True)
            cand = jnp.where(jrow_f == first, -jnp.inf, cand)
        sel_bias = jnp.where(cand == -jnp.inf, 0.0, NEG_INF).astype(BF16)
        lhs_ref[g, 0:HEAD_DIM, :] = qgs[g]
        lhs_ref[g, HEAD_DIM:, :] = jnp.concatenate([sel_bias] * HEADS_PER_GROUP, axis=1)
        m_ref[g] = jnp.full((1, rows), NEG_INF, F32)
        acc_ref[g] = jnp.zeros((V_ROWS, rows), F32)

    def sel_score(g, hf, k0, width):
        return _dot(ksel_ref[0, g, pl.ds(k0, width), :], lhs_ref[g, :, hf * half:(hf + 1) * half])

    def sel_consume(g, hf, k0, width, causal_bias, s_s):
        lanes = slice(hf * half, (hf + 1) * half)
        if causal_bias is not None:
            s_s = mask_heads(s_s, causal_bias)
        m_old = m_ref[g, :, lanes]
        m_new = jnp.maximum(m_old, jnp.max(s_s, axis=0, keepdims=True))
        p_s = jnp.exp2(s_s - m_new).astype(BF16)
        acc_ref[g, :, lanes] = (jnp.exp2(m_old - m_new) * acc_ref[g, :, lanes]
                                + _dot(vsel_ref[0, g, :, pl.ds(k0, width)], p_s))
        m_ref[g, :, lanes] = m_new

    def sweep(first_chunk, n_chunks, last_masked=False):
        k0 = pl.multiple_of(first_chunk * NSA_KC, NSA_KC)
        width = n_chunks * NSA_KC
        causal_bias = None
        if last_masked:
            causal_bias = mask_bias(k0 + lax.broadcasted_iota(jnp.int32, (width, 1), 0) <= tok)
        run_pipelined([(functools.partial(sel_score, g, hf, k0, width),
                        functools.partial(sel_consume, g, hf, k0, width, causal_bias))
                       for g in groups for hf in range(2)])

    def octet_body(ko, carry):
        sweep(8 * ko, 8)
        return carry

    lax.fori_loop(0, i // 8, octet_body, 0)
    done = (i // 8) * 8

    @pl.when(i % 8 >= 4)
    def _():
        sweep(done, 4)

    done = (i // 4) * 4

    @pl.when(i % 4 >= 2)
    def _():
        sweep(done, 2)

    done = (i // 2) * 2

    @pl.when(i % 2 == 1)
    def _():
        sweep(done, 2, last_masked=True)

    @pl.when(i % 2 == 0)
    def _():
        sweep(done, 1, last_masked=True)

    for g in groups:
        mixed_t = (part_ref[0, g] + normalise(acc_ref[g], gate_row(g, 1))) + part_ref[1, g]
        for r in range(HEADS_PER_GROUP):
            hd = HEADS_PER_GROUP * g + r
            o_ref[0, :, hd * HEAD_DIM:(hd + 1) * HEAD_DIM] = mixed_t[:, r * tq:(r + 1) * tq].T


def _nsa_call(q, gates_t, cmp, cmp_t, ksel, vsel_t, kwin, vwin_t, mcs_t):
    b, _, _, s = q.shape
    tq = NSA_TQ
    g2 = NSA_KV_GROUPS
    n_chunk = s // CMP_STRIDE
    n_slc = s // SLC_BLOCK
    rows = HEADS_PER_GROUP * tq
    tok = lambda bi, i: (bi, i, 0)
    per_batch = lambda bi, i: (bi, 0, 0, 0)
    return pl.pallas_call(
        functools.partial(_nsa_kernel, seq=s),
        grid=(b, s // tq),
        in_specs=[
            pl.BlockSpec((1, NSA_HEADS, HEAD_DIM, tq), lambda bi, i: (bi, 0, 0, i)),
            pl.BlockSpec((1, GATE_ROWS, tq), lambda bi, i: (bi, 0, i)),
            _resident((1, 2 * g2, n_chunk, HEAD_DIM), per_batch),
            _resident((1, 2 * g2, HEAD_DIM, n_chunk), per_batch),
            _resident((1, g2, s, HEAD_DIM + n_slc), per_batch),
            _resident((1, g2, V_ROWS, s), per_batch),
            _resident((1, g2, s, HEAD_DIM), per_batch),
            _resident((1, g2, V_ROWS, s), per_batch),
            _resident((n_slc, n_chunk), lambda bi, i: (0, 0)),
        ],
        out_specs=pl.BlockSpec((1, tq, NSA_WIDTH), tok),
        out_shape=jax.ShapeDtypeStruct((b, s, NSA_WIDTH), F32),
        scratch_shapes=[
            pltpu.VMEM((g2, HEAD_DIM + n_slc, rows), BF16),
            pltpu.VMEM((g2, 1, rows), F32),
            pltpu.VMEM((g2, V_ROWS, rows), F32),
            pltpu.VMEM((2, g2, HEAD_DIM, rows), F32),
        ],
        compiler_params=pltpu.CompilerParams(
            dimension_semantics=("arbitrary", "arbitrary"), vmem_limit_bytes=VMEM_LIMIT),
        name="nsa_attention",
    )(q, gates_t, cmp, cmp_t, ksel, vsel_t, kwin, vwin_t, mcs_t)


def _memkv_kernel(mem_ref, gain_ref, w_ref, o_ref):
    mem_n = _rmsnorm(mem_ref[0], gain_ref[...]).astype(BF16)
    o_ref[0, 0] = _dot(mem_n, w_ref[0]).astype(BF16)


def _memkv_call(mem, gain, w_mem):
    depth = w_mem.shape[0]
    b, m, d = mem.shape
    return pl.pallas_call(
        _memkv_kernel,
        grid=(depth, b),
        in_specs=[
            pl.BlockSpec((1, m, d), lambda l, bi: (bi, 0, 0)),
            pl.BlockSpec((1, d), lambda l, bi: (0, 0)),
            pl.BlockSpec((1, d, 2 * MEM_WIDTH), lambda l, bi: (l, 0, 0)),
        ],
        out_specs=pl.BlockSpec((1, 1, m, 2 * MEM_WIDTH), lambda l, bi: (l, bi, 0, 0)),
        out_shape=jax.ShapeDtypeStruct((depth, b, m, 2 * MEM_WIDTH), BF16),
        compiler_params=pltpu.CompilerParams(
            dimension_semantics=("arbitrary", "arbitrary"), vmem_limit_bytes=VMEM_LIMIT),
        name="mem_kv",
    )(mem, gain, w_mem)


_F_AUVZ = 0
_F_BZ = 3 * A_WIDTH
_F_CQZ = _F_BZ + NSA_WIDTH
_F_MERGE = _F_CQZ + 2 * MEM_WIDTH


def _fused_kernel(x_ref, ob_ref, gain_ref, w_ref, lng_ref, lnb_ref, wsp_ref, bsp_ref, mkv_ref,
                  wa_ref, wb_ref, wc_ref, wo_ref, *fgain_and_out, final):
    fgain_ref, o_ref = fgain_and_out if final else (None,) + fgain_and_out
    tm, d = x_ref.shape[1], x_ref.shape[2]
    th = tm // FUSE_SPLIT

    def front(x):
        h = _rmsnorm(x, gain_ref[...]).astype(BF16)

        def proj(c0, width):
            return _dot(h, w_ref[:, c0:c0 + width])

        return (proj(_F_AUVZ, 3 * A_WIDTH), proj(_F_CQZ, 2 * MEM_WIDTH), proj(_F_BZ, NSA_WIDTH),
                [proj(_F_MERGE + k * d, d) for k in range(3)])

    def back(x, ob, uvz, cqz, zb, merge):
        u, v, z = uvz[:, 0:A_WIDTH], uvz[:, A_WIDTH:2 * A_WIDTH], uvz[:, 2 * A_WIDTH:]
        vc = v - jnp.mean(v, axis=-1, keepdims=True)
        vn = (vc * lax.rsqrt(jnp.mean(vc * vc, axis=-1, keepdims=True) + EPS) * lng_ref[...]
              + lnb_ref[...]).astype(BF16)
        gdim = A_WIDTH // A_GROUPS
        mixed = jnp.concatenate([
            jnp.concatenate([
                _dot(wsp_ref[gi], vn[c * CHUNK:(c + 1) * CHUNK, gi * gdim:(gi + 1) * gdim]) + bsp_ref[gi]
                for gi in range(A_GROUPS)], axis=1)
            for c in range(th // CHUNK)], axis=0)

        scores = [_nt_dot(cqz[:, hd * HEAD_DIM:(hd + 1) * HEAD_DIM].astype(BF16),
                          mkv_ref[0, 0, :, hd * HEAD_DIM:(hd + 1) * HEAD_DIM]) * ATT_SCALE
                  for hd in range(MEM_HEADS)]

        o_a = (u * mixed * _silu(z)).astype(BF16)
        acc = jax.nn.sigmoid(merge[0]) * _dot(o_a, wa_ref[...])

        o_b = (ob * _silu(zb)).astype(BF16)
        acc = acc + jax.nn.sigmoid(merge[1]) * _dot(o_b, wb_ref[...])

        heads = []
        for hd in range(MEM_HEADS):
            e_m = jnp.exp(scores[hd] - jnp.max(scores[hd], axis=1, keepdims=True))
            p_m = e_m * (1.0 / jnp.sum(e_m, axis=1, keepdims=True))
            heads.append(_dot(p_m.astype(BF16),
                              mkv_ref[0, 0, :, MEM_WIDTH + hd * HEAD_DIM:MEM_WIDTH + (hd + 1) * HEAD_DIM]))
        o_c = (jnp.concatenate(heads, axis=1) * _silu(cqz[:, MEM_WIDTH:])).astype(BF16)
        acc = acc + jax.nn.sigmoid(merge[2]) * _dot(o_c, wc_ref[...])

        x_new = x + _dot(acc.astype(BF16), wo_ref[...])
        if final:
            x_new = _rmsnorm(x_new, fgain_ref[...])
        return x_new

    rows = [slice(k * th, (k + 1) * th) for k in range(FUSE_SPLIT)]
    fronts = [front(x_ref[0, r, :]) for r in rows]
    for r, f in zip(rows, fronts):
        o_ref[0, r, :] = back(x_ref[0, r, :], ob_ref[0, r, :], *f)


def _fused_call(x, ob, gain, w_f, lng, lnb, wsp, bsp, mkv, layer, wa, wb, wc, wo, fgain=None):
    b, s, d = x.shape
    tm = FUSE_TM
    m = mkv.shape[2]
    tok = lambda bi, i: (bi, i, 0)
    c2 = lambda bi, i: (0, 0)
    final = fgain is not None

    def layer_weights(w):
        return _resident((None,) + w.shape[1:], lambda bi, i: (layer,) + (0,) * (w.ndim - 1))

    return pl.pallas_call(
        functools.partial(_fused_kernel, final=final),
        grid=(b, s // tm),
        in_specs=[
            pl.BlockSpec((1, tm, d), tok),
            pl.BlockSpec((1, tm, NSA_WIDTH), tok),
            pl.BlockSpec((1, d), c2),
            layer_weights(w_f),
            pl.BlockSpec((1, A_WIDTH), c2),
            pl.BlockSpec((1, A_WIDTH), c2),
            layer_weights(wsp),
            layer_weights(bsp),
            _resident((1, 1, m, 2 * MEM_WIDTH), lambda bi, i: (layer, bi, 0, 0)),
            layer_weights(wa),
            layer_weights(wb),
            layer_weights(wc),
            layer_weights(wo),
        ] + ([pl.BlockSpec((1, d), c2)] if final else []),
        out_specs=pl.BlockSpec((1, tm, d), tok),
        out_shape=jax.ShapeDtypeStruct((b, s, d), F32),
        compiler_params=pltpu.CompilerParams(
            dimension_semantics=("arbitrary", "arbitrary"), vmem_limit_bytes=VMEM_LIMIT),
        name="fused_mix",
    )(x, ob, gain, w_f, lng, lnb, wsp, bsp, mkv, wa, wb, wc, wo, *([fgain] if final else []))


def _rope_lane_tables(positions):
    inv_freq = ROPE_THETA ** (-jnp.arange(0, ROPE_DIM, 2, dtype=F32) / ROPE_DIM)
    ang = positions.astype(F32)[..., None] * inv_freq
    cos, sin = jnp.cos(ang), jnp.sin(ang)
    zeros = jnp.zeros(cos.shape[:-1] + (HEAD_DIM - ROPE_DIM,), F32)
    cosf = jnp.concatenate([cos, cos, zeros + 1.0], axis=-1)
    sina = jnp.concatenate([-sin, jnp.zeros_like(sin), zeros], axis=-1)
    sinb = jnp.concatenate([jnp.zeros_like(sin), sin, zeros], axis=-1)
    return cosf, sina, sinb


def _cmp_to_slc(n_chunk, n_slc):
    i = np.arange(n_chunk)[:, None] * CMP_STRIDE
    j = np.arange(n_slc)[None, :] * SLC_BLOCK
    ov = np.clip(np.minimum(i + CMP_BLOCK, j + SLC_BLOCK) - np.maximum(i, j), 0, None) / CMP_BLOCK
    ov[n_chunk - 1] = 0.0
    return jnp.asarray(ov.T, dtype=BF16)


def kernel(x, mem, positions, norm_gain, w_in, ln_v_gain, ln_v_bias, w_spatial, b_spatial, cmp_pe_k, cmp_w1_k, cmp_w2_k, cmp_pe_v, cmp_w1_v, cmp_w2_v, mem_norm_gain, w_mem_kv, w_branch_a, w_branch_b, w_branch_c, w_out, final_norm_gain):
    depth = w_in.shape[0]
    b, s, d = x.shape
    assert s % NSA_TQ == 0 and s % PROJ_TM == 0 and s >= WINDOW + NSA_TQ and d == w_in.shape[1]

    cosf, sina, sinb = _rope_lane_tables(positions)
    mcs = _cmp_to_slc(s // CMP_STRIDE, s // SLC_BLOCK)

    gate_w = jnp.pad(w_in[:, :, _B_G:_B_KV], ((0, 0), (0, 0), (0, GATE_PAD - (_B_KV - _B_G))))
    w_p = jnp.concatenate([w_in[:, :, _B_Q:_B_Z], w_in[:, :, _B_KV:_C_Q], gate_w], axis=-1).astype(BF16)
    w_f = jnp.concatenate([w_in[:, :, _A_U:_B_Q], w_in[:, :, _B_Z:_B_G], w_in[:, :, _C_Q:_MERGE],
                           w_in[:, :, _MERGE:]], axis=-1).astype(BF16)
    wsp = (w_spatial * jnp.tril(jnp.ones((CHUNK, CHUNK), w_spatial.dtype))).astype(BF16)
    bsp = jnp.broadcast_to(b_spatial[..., None], b_spatial.shape + (A_WIDTH // A_GROUPS,))
    pe = jnp.stack([cmp_pe_k, cmp_pe_v], axis=1)
    w1 = jnp.stack([cmp_w1_k, cmp_w1_v], axis=1).astype(BF16)
    w2 = jnp.stack([cmp_w2_k, cmp_w2_v], axis=1).astype(BF16)
    wa, wb, wc, wo = (w.astype(BF16) for w in (w_branch_a, w_branch_b, w_branch_c, w_out))

    mkv = _memkv_call(mem, mem_norm_gain[None, :], w_mem_kv.astype(BF16))
    fgain = final_norm_gain[None, :]
    for l in range(depth):
        gain = norm_gain[l][None, :]
        q, gates_t, cmp_in, ksel, vsel_t, kwin, vwin_t = _proj_call(x, gain, w_p, l, cosf, sina, sinb)
        cmp, cmp_t = _compress_call(cmp_in, pe[l], w1[l], w2[l])
        ob = _nsa_call(q, gates_t, cmp, cmp_t, ksel, vsel_t, kwin, vwin_t, mcs)
        x = _fused_call(x, ob, gain, w_f, ln_v_gain[l][None, :], ln_v_bias[l][None, :], wsp, bsp,
                        mkv, l, wa, wb, wc, wo, fgain if l == depth - 1 else None)
    return x
```

```python
import functools

import numpy as np
import jax
import jax.numpy as jnp
from jax import lax
from jax.experimental import pallas as pl
from jax.experimental.pallas import tpu as pltpu

F32 = jnp.float32
BF16 = jnp.bfloat16

HEAD_DIM = 128
A_GROUPS = 4
A_WIDTH = 512
CHUNK = 128
NSA_HEADS = 8
NSA_KV_GROUPS = 2
HEADS_PER_GROUP = NSA_HEADS // NSA_KV_GROUPS
NSA_WIDTH = NSA_HEADS * HEAD_DIM
CMP_BLOCK = 32
CMP_STRIDE = 16
SLC_BLOCK = 64
N_SELECT = 16
WINDOW = 512
MEM_HEADS = 4
MEM_WIDTH = MEM_HEADS * HEAD_DIM
ROPE_DIM = HEAD_DIM // 4
ROPE_THETA = 500000.0
EPS = 1e-6
NEG_INF = -1e30
FORCED_SCORE = 1e4
ATT_SCALE = HEAD_DIM ** -0.5
LOG2_E = 1.4426950408889634

_OFF = np.cumsum((0, A_WIDTH, A_WIDTH, A_WIDTH, NSA_WIDTH, NSA_WIDTH, NSA_HEADS * 3, 3 * 2 * NSA_KV_GROUPS * HEAD_DIM,
                  MEM_WIDTH, MEM_WIDTH))
(_A_U, _A_V, _A_Z, _B_Q, _B_Z, _B_G, _B_KV, _C_Q, _C_Z, _MERGE) = (int(v) for v in _OFF)
GATE_PAD = 128
GATE_ROWS = 32
V_ROWS = HEAD_DIM + 16

PROJ_TM = 512
PROJ_SPLIT = 2
NSA_TQ = 256
NSA_KC = 256
CMP_VARIANTS = 4
FUSE_TM = 512
FUSE_SPLIT = 2
VMEM_LIMIT = 56 * 1024 * 1024


def _nt_dot(a, b):
    return lax.dot_general(a, b, (((1,), (1,)), ((), ())), preferred_element_type=F32)


def _dot(a, b):
    return jnp.dot(a, b, preferred_element_type=F32)


def _rmsnorm(x, gain):
    return x * lax.rsqrt(jnp.mean(x * x, axis=-1, keepdims=True) + EPS) * gain


def _silu(x):
    return x * jax.nn.sigmoid(x)


def _resident(block_shape, index_map):
    return pl.BlockSpec(block_shape, index_map, pipeline_mode=pl.Buffered(1))


def _proj_kernel(x_ref, gain_ref, w_ref, cos_ref, sina_ref, sinb_ref,
                 q_ref, gate_ref, cmp_ref, ksel_ref, vsel_ref, kwin_ref, vwin_ref, *, n_slc):
    tm = x_ref.shape[1]
    th = tm // PROJ_SPLIT
    g2 = NSA_KV_GROUPS
    kv0 = NSA_HEADS
    rows = [slice(k * th, (k + 1) * th) for k in range(PROJ_SPLIT)]
    projs = [_dot(_rmsnorm(x_ref[0, r, :], gain_ref[...]).astype(BF16), w_ref[...]) for r in rows]
    ones_row = (lax.broadcasted_iota(jnp.int32, (V_ROWS - HEAD_DIM, th), 0) == 0).astype(BF16)
    for k, (r, proj) in enumerate(zip(rows, projs)):
        cosf, sina, sinb = cos_ref[0, r, :], sina_ref[0, r, :], sinb_ref[0, r, :]

        def rope(t):
            return (t * cosf + pltpu.roll(t, HEAD_DIM - ROPE_DIM // 2, 1) * sina
                    + pltpu.roll(t, ROPE_DIM // 2, 1) * sinb)

        def col(j):
            return proj[:, j * HEAD_DIM:(j + 1) * HEAD_DIM]

        for hd in range(NSA_HEADS):
            q_ref[0, hd, :, r] = (rope(col(hd)) * (ATT_SCALE * LOG2_E)).T.astype(BF16)
        row = pl.program_id(1) * tm + k * th + lax.broadcasted_iota(jnp.int32, (th, n_slc), 0)
        blk = lax.broadcasted_iota(jnp.int32, (th, n_slc), 1)
        sel_onehot = (row // SLC_BLOCK == blk).astype(BF16)
        for g in range(g2):
            cmp_ref[0, g, r, :] = rope(col(kv0 + g))
            cmp_ref[0, g2 + g, r, :] = col(kv0 + g2 + g)
            ksel_ref[0, g, r, 0:HEAD_DIM] = rope(col(kv0 + 2 * g2 + g)).astype(BF16)
            ksel_ref[0, g, r, HEAD_DIM:] = sel_onehot
            kwin_ref[0, g, r, :] = rope(col(kv0 + 4 * g2 + g)).astype(BF16)
            for v_ref, j in ((vsel_ref, kv0 + 3 * g2 + g), (vwin_ref, kv0 + 5 * g2 + g)):
                v_ref[0, g, 0:HEAD_DIM, r] = col(j).T.astype(BF16)
                v_ref[0, g, HEAD_DIM:, r] = ones_row
        gate_ref[0, :, r] = jax.nn.sigmoid(proj[:, (kv0 + 6 * g2) * HEAD_DIM:]).T[0:GATE_ROWS]


def _proj_call(x, gain, w_p, layer, cosf, sina, sinb):
    b, s, d = x.shape
    tm = PROJ_TM
    n_slc = s // SLC_BLOCK
    g2 = NSA_KV_GROUPS
    wp_cols = w_p.shape[2]
    tok = lambda bi, i: (bi, i, 0)
    grp = lambda bi, i: (bi, 0, i, 0)
    grp_t = lambda bi, i: (bi, 0, 0, i)
    return pl.pallas_call(
        functools.partial(_proj_kernel, n_slc=n_slc),
        grid=(b, s // tm),
        in_specs=[
            pl.BlockSpec((1, tm, d), tok),
            pl.BlockSpec((1, d), lambda bi, i: (0, 0)),
            _resident((None, d, wp_cols), lambda bi, i: (layer, 0, 0)),
            pl.BlockSpec((1, tm, HEAD_DIM), tok),
            pl.BlockSpec((1, tm, HEAD_DIM), tok),
            pl.BlockSpec((1, tm, HEAD_DIM), tok),
        ],
        out_specs=[
            pl.BlockSpec((1, NSA_HEADS, HEAD_DIM, tm), grp_t),
            pl.BlockSpec((1, GATE_ROWS, tm), lambda bi, i: (bi, 0, i)),
            pl.BlockSpec((1, 2 * g2, tm, HEAD_DIM), grp),
            pl.BlockSpec((1, g2, tm, HEAD_DIM + n_slc), grp),
            pl.BlockSpec((1, g2, V_ROWS, tm), grp_t),
            pl.BlockSpec((1, g2, tm, HEAD_DIM), grp),
            pl.BlockSpec((1, g2, V_ROWS, tm), grp_t),
        ],
        out_shape=[
            jax.ShapeDtypeStruct((b, NSA_HEADS, HEAD_DIM, s), BF16),
            jax.ShapeDtypeStruct((b, GATE_ROWS, s), F32),
            jax.ShapeDtypeStruct((b, 2 * g2, s, HEAD_DIM), F32),
            jax.ShapeDtypeStruct((b, g2, s, HEAD_DIM + n_slc), BF16),
            jax.ShapeDtypeStruct((b, g2, V_ROWS, s), BF16),
            jax.ShapeDtypeStruct((b, g2, s, HEAD_DIM), BF16),
            jax.ShapeDtypeStruct((b, g2, V_ROWS, s), BF16),
        ],
        compiler_params=pltpu.CompilerParams(
            dimension_semantics=("arbitrary", "arbitrary"), vmem_limit_bytes=VMEM_LIMIT),
        name="nsa_proj",
    )(x, gain, w_p, cosf, sina, sinb)


def _compress_kernel(x_ref, pe_ref, w1_ref, w2_ref, o_ref, ot_ref):
    n_chunk = o_ref.shape[2]
    first = jnp.zeros((n_chunk, HEAD_DIM), F32)
    second = jnp.zeros((n_chunk, HEAD_DIM), F32)
    for c in range(CMP_STRIDE):
        rows = x_ref[0, 0, pl.ds(c, n_chunk, stride=CMP_STRIDE), :]
        first = first + _dot((rows + pe_ref[0, c:c + 1, :]).astype(BF16), w1_ref[0, c])
        second = second + _dot((rows + pe_ref[0, CMP_STRIDE + c:CMP_STRIDE + c + 1, :]).astype(BF16),
                               w1_ref[0, CMP_STRIDE + c])
    hidden = first + pltpu.roll(second, n_chunk - 1, 0)
    out = _dot(_silu(hidden).astype(BF16), w2_ref[0])
    o_ref[0, 0] = out.astype(BF16)
    ot_ref[0, 0] = out.T.astype(BF16)


def _compress_call(cmp_in, pe, w1, w2):
    b, n4, s, dh = cmp_in.shape
    g2 = NSA_KV_GROUPS
    n_chunk = s // CMP_STRIDE
    return pl.pallas_call(
        _compress_kernel,
        grid=(b, n4),
        in_specs=[
            pl.BlockSpec((1, 1, s, dh), lambda bi, j: (bi, j, 0, 0)),
            pl.BlockSpec((1, CMP_BLOCK, dh), lambda bi, j: (j // g2, 0, 0)),
            pl.BlockSpec((1, CMP_BLOCK, dh, dh), lambda bi, j: (j // g2, 0, 0, 0)),
            pl.BlockSpec((1, dh, dh), lambda bi, j: (j // g2, 0, 0)),
        ],
        out_specs=[pl.BlockSpec((1, 1, n_chunk, dh), lambda bi, j: (bi, j, 0, 0)),
                   pl.BlockSpec((1, 1, dh, n_chunk), lambda bi, j: (bi, j, 0, 0))],
        out_shape=[jax.ShapeDtypeStruct((b, n4, n_chunk, dh), BF16),
                   jax.ShapeDtypeStruct((b, n4, dh, n_chunk), BF16)],
        compiler_params=pltpu.CompilerParams(
            dimension_semantics=("arbitrary", "arbitrary"), vmem_limit_bytes=VMEM_LIMIT),
        name="nsa_compress",
    )(cmp_in, pe, w1, w2)


def _nsa_kernel(q_ref, gate_ref, cmp_ref, cmpt_ref, ksel_ref, vsel_ref, kwin_ref, vwin_ref, mcs_ref, o_ref,
                lhs_ref, m_ref, acc_ref, part_ref, *, seq):
    tq = q_ref.shape[3]
    assert tq == NSA_KC
    rows = HEADS_PER_GROUP * tq
    half = rows // 2
    n_chunk = seq // CMP_STRIDE
    n_cmp = n_chunk - 1
    n_slc = seq // SLC_BLOCK
    groups = range(NSA_KV_GROUPS)
    i = pl.program_id(1)
    t0 = i * tq
    tok = t0 + lax.broadcasted_iota(jnp.int32, (1, tq), 1)
    tpos = jnp.concatenate([tok] * HEADS_PER_GROUP, axis=1)

    def mask_bias(mask):
        return jnp.where(mask, 0.0, NEG_INF)

    def mask_heads(s, bias):
        return jnp.concatenate([s[:, r * tq:(r + 1) * tq] + bias for r in range(s.shape[1] // tq)], axis=1)

    def gate_row(g, branch):
        return jnp.concatenate(
            [gate_ref[0, 3 * (HEADS_PER_GROUP * g + r) + branch:3 * (HEADS_PER_GROUP * g + r) + branch + 1, :]
             for r in range(HEADS_PER_GROUP)], axis=1)

    def normalise(acc_t, gate):
        return acc_t[0:HEAD_DIM] * (gate / acc_t[HEAD_DIM:HEAD_DIM + 1])

    qgs = [jnp.concatenate([q_ref[0, HEADS_PER_GROUP * g + r] for r in range(HEADS_PER_GROUP)], axis=1)
           for g in groups]

    def run_pipelined(items):
        scores = [score_fn() for score_fn, _ in items]
        for k, (_, consume_fn) in enumerate(items):
            consume_fn(scores[k])
            scores[k] = None

    wlen = WINDOW + tq
    w0 = pl.multiple_of(jnp.maximum(t0 - WINDOW, 0), tq)
    rel = tok - (w0 + lax.broadcasted_iota(jnp.int32, (wlen, 1), 0))
    win_bias = mask_bias(lax.bitcast_convert_type(rel, jnp.uint32) < WINDOW)

    def win_consume(g, s_w):
        s_w = mask_heads(s_w, win_bias)
        p_w = jnp.exp2(s_w - jnp.max(s_w, axis=0, keepdims=True)).astype(BF16)
        part_ref[1, g] = normalise(_dot(vwin_ref[0, g, :, pl.ds(w0, wlen)], p_w), gate_row(g, 2))

    run_pipelined([(functools.partial(_dot, kwin_ref[0, g, pl.ds(w0, wlen), :], qgs[g]),
                    functools.partial(win_consume, g)) for g in groups])
    for g in groups:
        lhs_ref[g, 0:HEAD_DIM, :] = qgs[g]
        m_ref[g] = jnp.full((1, rows), NEG_INF, F32)
        acc_ref[g] = jnp.zeros((V_ROWS, rows), F32)

    def select_blocks(vis_chunk):
        vis_slc = vis_chunk * CMP_STRIDE // SLC_BLOCK
        nrow = lax.broadcasted_iota(jnp.int32, (vis_chunk, 1), 0)
        cmp_bias = mask_bias(nrow <= jnp.minimum((tok - (CMP_BLOCK - 1)) // CMP_STRIDE, n_cmp - 1))
        sees_block = tpos >= CMP_BLOCK - 1
        imps = [None] * NSA_KV_GROUPS

        def cmp_score(g):
            q_g = jnp.concatenate([q_ref[0, HEADS_PER_GROUP * g + r] for r in range(HEADS_PER_GROUP)], axis=1)
            return _dot(cmp_ref[0, g, 0:vis_chunk, :], q_g)

        def cmp_consume(g, s_c):
            s_c = mask_heads(s_c, cmp_bias)
            e_c = jnp.exp2(s_c - jnp.max(s_c, axis=0, keepdims=True))
            p_c = e_c * jnp.where(sees_block, 1.0 / jnp.sum(e_c, axis=0, keepdims=True), 0.0)
            part_ref[0, g] = (_dot(cmpt_ref[0, NSA_KV_GROUPS + g, :, 0:vis_chunk], p_c.astype(BF16))
                              * gate_row(g, 0))
            p_sum = p_c[:, 0:tq]
            for r in range(1, HEADS_PER_GROUP):
                p_sum = p_sum + p_c[:, r * tq:(r + 1) * tq]
            p_hi = p_sum.astype(BF16)
            p_lo = (p_sum - p_hi.astype(F32)).astype(BF16)
            mcs_t = mcs_ref[0:vis_slc, 0:vis_chunk]
            imps[g] = _dot(mcs_t, p_hi) + _dot(mcs_t, p_lo)

        run_pipelined([(functools.partial(cmp_score, g), functools.partial(cmp_consume, g)) for g in groups])

        jrow = lax.broadcasted_iota(jnp.int32, (vis_slc, tq), 0)
        cur = (t0 + lax.broadcasted_iota(jnp.int32, (vis_slc, tq), 1)) // SLC_BLOCK
        forced = (jrow == 0) | (jrow == cur) | (jrow == cur - 1)
        jrow_f = jrow.astype(F32)
        for g in groups:
            cand = jnp.where(forced, -jnp.inf, jnp.where(jrow <= cur, imps[g], -1.0))
            for _ in range(min(N_SELECT, n_slc) - 3):
                best = jnp.max(cand, axis=0, keepdims=True)
                first = jnp.min(jnp.where(cand == best, jrow_f, float(n_slc)), axis=0, keepdims=True)
                cand = jnp.where(jrow_f == first, -jnp.inf, cand)
            sel_bias = jnp.where(cand == -jnp.inf, 0.0, NEG_INF).astype(BF16)
            lhs_ref[g, HEAD_DIM:HEAD_DIM + vis_slc, :] = jnp.concatenate([sel_bias] * HEADS_PER_GROUP, axis=1)
            if vis_slc < n_slc:
                lhs_ref[g, HEAD_DIM + vis_slc:, :] = jnp.full((n_slc - vis_slc, rows), NEG_INF, BF16)

    chunk_step = n_chunk // CMP_VARIANTS
    variant = ((t0 + tq) // CMP_STRIDE - 1) // chunk_step
    for v in range(CMP_VARIANTS):
        pl.when(variant == v)(functools.partial(select_blocks, chunk_step * (v + 1)))

    def sel_score(g, hf, k0, width):
        return _dot(ksel_ref[0, g, pl.ds(k0, width), :], lhs_ref[g, :, hf * half:(hf + 1) * half])

    def sel_consume(g, hf, k0, width, causal_bias, s_s):
        lanes = slice(hf * half, (hf + 1) * half)
        if causal_bias is not None:
            s_s = mask_heads(s_s, causal_bias)
        m_old = m_ref[g, :, lanes]
        m_new = jnp.maximum(m_old, jnp.max(s_s, axis=0, keepdims=True))
        p_s = jnp.exp2(s_s - m_new).astype(BF16)
        acc_ref[g, :, lanes] = (jnp.exp2(m_old - m_new) * acc_ref[g, :, lanes]
                                + _dot(vsel_ref[0, g, :, pl.ds(k0, width)], p_s))
        m_ref[g, :, lanes] = m_new

    def sweep(first_chunk, n_chunks, last_masked=False):
        k0 = pl.multiple_of(first_chunk * NSA_KC, NSA_KC)
        width = n_chunks * NSA_KC
        causal_bias = None
        if last_masked:
            causal_bias = mask_bias(k0 + lax.broadcasted_iota(jnp.int32, (width, 1), 0) <= tok)
        run_pipelined([(functools.partial(sel_score, g, hf, k0, width),
                        functools.partial(sel_consume, g, hf, k0, width, causal_bias))
                       for g in groups for hf in range(2)])

    def octet_body(ko, carry):
        sweep(8 * ko, 8)
        return carry

    lax.fori_loop(0, i // 8, octet_body, 0)
    done = (i // 8) * 8

    @pl.when(i % 8 >= 4)
    def _():
        sweep(done, 4)

    done = (i // 4) * 4

    @pl.when(i % 4 >= 2)
    def _():
        sweep(done, 2)

    done = (i // 2) * 2

    @pl.when(i % 2 == 1)
    def _():
        sweep(done, 2, last_masked=True)

    @pl.when(i % 2 == 0)
    def _():
        sweep(done, 1, last_masked=True)

    for g in groups:
        mixed_t = (part_ref[0, g] + normalise(acc_ref[g], gate_row(g, 1))) + part_ref[1, g]
        for r in range(HEADS_PER_GROUP):
            hd = HEADS_PER_GROUP * g + r
            o_ref[0, :, hd * HEAD_DIM:(hd + 1) * HEAD_DIM] = mixed_t[:, r * tq:(r + 1) * tq].T


def _nsa_call(q, gates_t, cmp, cmp_t, ksel, vsel_t, kwin, vwin_t, mcs_t):
    b, _, _, s = q.shape
    tq = NSA_TQ
    g2 = NSA_KV_GROUPS
    n_chunk = s // CMP_STRIDE
    n_slc = s // SLC_BLOCK
    rows = HEADS_PER_GROUP * tq
    tok = lambda bi, i: (bi, i, 0)
    per_batch = lambda bi, i: (bi, 0, 0, 0)
    return pl.pallas_call(
        functools.partial(_nsa_kernel, seq=s),
        grid=(b, s // tq),
        in_specs=[
            pl.BlockSpec((1, NSA_HEADS, HEAD_DIM, tq), lambda bi, i: (bi, 0, 0, i)),
            pl.BlockSpec((1, GATE_ROWS, tq), lambda bi, i: (bi, 0, i)),
            _resident((1, 2 * g2, n_chunk, HEAD_DIM), per_batch),
            _resident((1, 2 * g2, HEAD_DIM, n_chunk), per_batch),
            _resident((1, g2, s, HEAD_DIM + n_slc), per_batch),
            _resident((1, g2, V_ROWS, s), per_batch),
            _resident((1, g2, s, HEAD_DIM), per_batch),
            _resident((1, g2, V_ROWS, s), per_batch),
            _resident((n_slc, n_chunk), lambda bi, i: (0, 0)),
        ],
        out_specs=pl.BlockSpec((1, tq, NSA_WIDTH), tok),
        out_shape=jax.ShapeDtypeStruct((b, s, NSA_WIDTH), F32),
        scratch_shapes=[
            pltpu.VMEM((g2, HEAD_DIM + n_slc, rows), BF16),
            pltpu.VMEM((g2, 1, rows), F32),
            pltpu.VMEM((g2, V_ROWS, rows), F32),
            pltpu.VMEM((2, g2, HEAD_DIM, rows), F32),
        ],
        compiler_params=pltpu.CompilerParams(
            dimension_semantics=("arbitrary", "arbitrary"), vmem_limit_bytes=VMEM_LIMIT),
        name="nsa_attention",
    )(q, gates_t, cmp, cmp_t, ksel, vsel_t, kwin, vwin_t, mcs_t)


def _memkv_kernel(mem_ref, gain_ref, w_ref, o_ref):
    mem_n = _rmsnorm(mem_ref[0], gain_ref[...]).astype(BF16)
    o_ref[0, 0] = _dot(mem_n, w_ref[0]).astype(BF16)


def _memkv_call(mem, gain, w_mem):
    depth = w_mem.shape[0]
    b, m, d = mem.shape
    return pl.pallas_call(
        _memkv_kernel,
        grid=(depth, b),
        in_specs=[
            pl.BlockSpec((1, m, d), lambda l, bi: (bi, 0, 0)),
            pl.BlockSpec((1, d), lambda l, bi: (0, 0)),
            pl.BlockSpec((1, d, 2 * MEM_WIDTH), lambda l, bi: (l, 0, 0)),
        ],
        out_specs=pl.BlockSpec((1, 1, m, 2 * MEM_WIDTH), lambda l, bi: (l, bi, 0, 0)),
        out_shape=jax.ShapeDtypeStruct((depth, b, m, 2 * MEM_WIDTH), BF16),
        compiler_params=pltpu.CompilerParams(
            dimension_semantics=("arbitrary", "arbitrary"), vmem_limit_bytes=VMEM_LIMIT),
        name="mem_kv",
    )(mem, gain, w_mem)


_F_AUVZ = 0
_F_BZ = 3 * A_WIDTH
_F_CQZ = _F_BZ + NSA_WIDTH
_F_MERGE = _F_CQZ + 2 * MEM_WIDTH


def _fused_kernel(x_ref, ob_ref, gain_ref, w_ref, lng_ref, lnb_ref, wsp_ref, bsp_ref, mkv_ref,
                  wa_ref, wb_ref, wc_ref, wo_ref, *fgain_and_out, final):
    fgain_ref, o_ref = fgain_and_out if final else (None,) + fgain_and_out
    tm, d = x_ref.shape[1], x_ref.shape[2]
    th = tm // FUSE_SPLIT

    def front(x):
        h = _rmsnorm(x, gain_ref[...]).astype(BF16)

        def proj(c0, width):
            return _dot(h, w_ref[:, c0:c0 + width])

        return (proj(_F_AUVZ, 3 * A_WIDTH), proj(_F_CQZ, 2 * MEM_WIDTH), proj(_F_BZ, NSA_WIDTH),
                [proj(_F_MERGE + k * d, d) for k in range(3)])

    def back(x, ob, uvz, cqz, zb, merge):
        u, v, z = uvz[:, 0:A_WIDTH], uvz[:, A_WIDTH:2 * A_WIDTH], uvz[:, 2 * A_WIDTH:]
        vc = v - jnp.mean(v, axis=-1, keepdims=True)
        vn = (vc * lax.rsqrt(jnp.mean(vc * vc, axis=-1, keepdims=True) + EPS) * lng_ref[...]
              + lnb_ref[...]).astype(BF16)
        gdim = A_WIDTH // A_GROUPS
        mixed = jnp.concatenate([
            jnp.concatenate([
                _dot(wsp_ref[gi], vn[c * CHUNK:(c + 1) * CHUNK, gi * gdim:(gi + 1) * gdim]) + bsp_ref[gi]
                for gi in range(A_GROUPS)], axis=1)
            for c in range(th // CHUNK)], axis=0)

        scores = [_nt_dot(cqz[:, hd * HEAD_DIM:(hd + 1) * HEAD_DIM].astype(BF16),
                          mkv_ref[0, 0, :, hd * HEAD_DIM:(hd + 1) * HEAD_DIM]) * ATT_SCALE
                  for hd in range(MEM_HEADS)]

        o_a = (u * mixed * _silu(z)).astype(BF16)
        acc = jax.nn.sigmoid(merge[0]) * _dot(o_a, wa_ref[...])

        o_b = (ob * _silu(zb)).astype(BF16)
        acc = acc + jax.nn.sigmoid(merge[1]) * _dot(o_b, wb_ref[...])

        heads = []
        for hd in range(MEM_HEADS):
            e_m = jnp.exp(scores[hd] - jnp.max(scores[hd], axis=1, keepdims=True))
            p_m = e_m * (1.0 / jnp.sum(e_m, axis=1, keepdims=True))
            heads.append(_dot(p_m.astype(BF16),
                              mkv_ref[0, 0, :, MEM_WIDTH + hd * HEAD_DIM:MEM_WIDTH + (hd + 1) * HEAD_DIM]))
        o_c = (jnp.concatenate(heads, axis=1) * _silu(cqz[:, MEM_WIDTH:])).astype(BF16)
        acc = acc + jax.nn.sigmoid(merge[2]) * _dot(o_c, wc_ref[...])

        x_new = x + _dot(acc.astype(BF16), wo_ref[...])
        if final:
            x_new = _rmsnorm(x_new, fgain_ref[...])
        return x_new

    rows = [slice(k * th, (k + 1) * th) for k in range(FUSE_SPLIT)]
    fronts = [front(x_ref[0, r, :]) for r in rows]
    for r, f in zip(rows, fronts):
        o_ref[0, r, :] = back(x_ref[0, r, :], ob_ref[0, r, :], *f)


def _fused_call(x, ob, gain, w_f, lng, lnb, wsp, bsp, mkv, layer, wa, wb, wc, wo, fgain=None):
    b, s, d = x.shape
    tm = FUSE_TM
    m = mkv.shape[2]
    tok = lambda bi, i: (bi, i, 0)
    c2 = lambda bi, i: (0, 0)
    final = fgain is not None

    def layer_weights(w):
        return _resident((None,) + w.shape[1:], lambda bi, i: (layer,) + (0,) * (w.ndim - 1))

    return pl.pallas_call(
        functools.partial(_fused_kernel, final=final),
        grid=(b, s // tm),
        in_specs=[
            pl.BlockSpec((1, tm, d), tok),
            pl.BlockSpec((1, tm, NSA_WIDTH), tok),
            pl.BlockSpec((1, d), c2),
            layer_weights(w_f),
            pl.BlockSpec((1, A_WIDTH), c2),
            pl.BlockSpec((1, A_WIDTH), c2),
            layer_weights(wsp),
            layer_weights(bsp),
            _resident((1, 1, m, 2 * MEM_WIDTH), lambda bi, i: (layer, bi, 0, 0)),
            layer_weights(wa),
            layer_weights(wb),
            layer_weights(wc),
            layer_weights(wo),
        ] + ([pl.BlockSpec((1, d), c2)] if final else []),
        out_specs=pl.BlockSpec((1, tm, d), tok),
        out_shape=jax.ShapeDtypeStruct((b, s, d), F32),
        compiler_params=pltpu.CompilerParams(
            dimension_semantics=("arbitrary", "arbitrary"), vmem_limit_bytes=VMEM_LIMIT),
        name="fused_mix",
    )(x, ob, gain, w_f, lng, lnb, wsp, bsp, mkv, wa, wb, wc, wo, *([fgain] if final else []))


def _rope_lane_tables(positions):
    inv_freq = ROPE_THETA ** (-jnp.arange(0, ROPE_DIM, 2, dtype=F32) / ROPE_DIM)
    ang = positions.astype(F32)[..., None] * inv_freq
    cos, sin = jnp.cos(ang), jnp.sin(ang)
    zeros = jnp.zeros(cos.shape[:-1] + (HEAD_DIM - ROPE_DIM,), F32)
    cosf = jnp.concatenate([cos, cos, zeros + 1.0], axis=-1)
    sina = jnp.concatenate([-sin, jnp.zeros_like(sin), zeros], axis=-1)
    sinb = jnp.concatenate([jnp.zeros_like(sin), sin, zeros], axis=-1)
    return cosf, sina, sinb


def _cmp_to_slc(n_chunk, n_slc):
    i = np.arange(n_chunk)[:, None] * CMP_STRIDE
    j = np.arange(n_slc)[None, :] * SLC_BLOCK
    ov = np.clip(np.minimum(i + CMP_BLOCK, j + SLC_BLOCK) - np.maximum(i, j), 0, None) / CMP_BLOCK
    ov[n_chunk - 1] = 0.0
    return jnp.asarray(ov.T, dtype=BF16)


def kernel(x, mem, positions, norm_gain, w_in, ln_v_gain, ln_v_bias, w_spatial, b_spatial, cmp_pe_k, cmp_w1_k, cmp_w2_k, cmp_pe_v, cmp_w1_v, cmp_w2_v, mem_norm_gain, w_mem_kv, w_branch_a, w_branch_b, w_branch_c, w_out, final_norm_gain):
    depth = w_in.shape[0]
    b, s, d = x.shape
    assert s % NSA_TQ == 0 and s % PROJ_TM == 0 and s >= WINDOW + NSA_TQ and d == w_in.shape[1]

    cosf, sina, sinb = _rope_lane_tables(positions)
    mcs = _cmp_to_slc(s // CMP_STRIDE, s // SLC_BLOCK)

    gate_w = jnp.pad(w_in[:, :, _B_G:_B_KV], ((0, 0), (0, 0), (0, GATE_PAD - (_B_KV - _B_G))))
    w_p = jnp.concatenate([w_in[:, :, _B_Q:_B_Z], w_in[:, :, _B_KV:_C_Q], gate_w], axis=-1).astype(BF16)
    w_f = jnp.concatenate([w_in[:, :, _A_U:_B_Q], w_in[:, :, _B_Z:_B_G], w_in[:, :, _C_Q:_MERGE],
                           w_in[:, :, _MERGE:]], axis=-1).astype(BF16)
    wsp = (w_spatial * jnp.tril(jnp.ones((CHUNK, CHUNK), w_spatial.dtype))).astype(BF16)
    bsp = jnp.broadcast_to(b_spatial[..., None], b_spatial.shape + (A_WIDTH // A_GROUPS,))
    pe = jnp.stack([cmp_pe_k, cmp_pe_v], axis=1)
    w1 = jnp.stack([cmp_w1_k, cmp_w1_v], axis=1).astype(BF16)
    w2 = jnp.stack([cmp_w2_k, cmp_w2_v], axis=1).astype(BF16)
    wa, wb, wc, wo = (w.astype(BF16) for w in (w_branch_a, w_branch_b, w_branch_c, w_out))

    mkv = _memkv_call(mem, mem_norm_gain[None, :], w_mem_kv.astype(BF16))
    fgain = final_norm_gain[None, :]
    for l in range(depth):
        gain = norm_gain[l][None, :]
        q, gates_t, cmp_in, ksel, vsel_t, kwin, vwin_t = _proj_call(x, gain, w_p, l, cosf, sina, sinb)
        cmp, cmp_t = _compress_call(cmp_in, pe[l], w1[l], w2[l])
        ob = _nsa_call(q, gates_t, cmp, cmp_t, ksel, vsel_t, kwin, vwin_t, mcs)
        x = _fused_call(x, ob, gain, w_f, ln_v_gain[l][None, :], ln_v_bias[l][None, :], wsp, bsp,
                        mkv, l, wa, wb, wc, wo, fgain if l == depth - 1 else None)
    return x
```

```python
import functools

import numpy as np
import jax
import jax.numpy as jnp
from jax import lax
from jax.experimental import pallas as pl
from jax.experimental.pallas import tpu as pltpu

F32 = jnp.float32
BF16 = jnp.bfloat16

HEAD_DIM = 128
A_GROUPS = 4
A_WIDTH = 512
CHUNK = 128
NSA_HEADS = 8
NSA_KV_GROUPS = 2
HEADS_PER_GROUP = NSA_HEADS // NSA_KV_GROUPS
NSA_WIDTH = NSA_HEADS * HEAD_DIM
CMP_BLOCK = 32
CMP_STRIDE = 16
SLC_BLOCK = 64
N_SELECT = 16
WINDOW = 512
MEM_HEADS = 4
MEM_WIDTH = MEM_HEADS * HEAD_DIM
ROPE_DIM = HEAD_DIM // 4
ROPE_THETA = 500000.0
EPS = 1e-6
NEG_INF = -1e30
FORCED_SCORE = 1e4
ATT_SCALE = HEAD_DIM ** -0.5
LOG2_E = 1.4426950408889634

_OFF = np.cumsum((0, A_WIDTH, A_WIDTH, A_WIDTH, NSA_WIDTH, NSA_WIDTH, NSA_HEADS * 3, 3 * 2 * NSA_KV_GROUPS * HEAD_DIM,
                  MEM_WIDTH, MEM_WIDTH))
(_A_U, _A_V, _A_Z, _B_Q, _B_Z, _B_G, _B_KV, _C_Q, _C_Z, _MERGE) = (int(v) for v in _OFF)
GATE_PAD = 128
GATE_ROWS = 32
V_ROWS = HEAD_DIM + 16

PROJ_TM = 512
PROJ_SPLIT = 2
NSA_TQ = 256
NSA_KC = 256
CMP_VARIANTS = 4
FUSE_TM = 512
FUSE_SPLIT = 2
VMEM_LIMIT = 56 * 1024 * 1024


def _nt_dot(a, b):
    return lax.dot_general(a, b, (((1,), (1,)), ((), ())), preferred_element_type=F32)


def _dot(a, b):
    return jnp.dot(a, b, preferred_element_type=F32)


def _rmsnorm(x, gain):
    return x * lax.rsqrt(jnp.mean(x * x, axis=-1, keepdims=True) + EPS) * gain


def _silu(x):
    return x * jax.nn.sigmoid(x)


def _resident(block_shape, index_map):
    return pl.BlockSpec(block_shape, index_map, pipeline_mode=pl.Buffered(1))


def _proj_kernel(x_ref, gain_ref, w_ref, cos_ref, sina_ref, sinb_ref,
                 q_ref, gate_ref, cmp_ref, ksel_ref, vsel_ref, kwin_ref, vwin_ref, *, n_slc):
    tm = x_ref.shape[1]
    th = tm // PROJ_SPLIT
    g2 = NSA_KV_GROUPS
    kv0 = NSA_HEADS
    rows = [slice(k * th, (k + 1) * th) for k in range(PROJ_SPLIT)]
    projs = [_dot(_rmsnorm(x_ref[0, r, :], gain_ref[...]).astype(BF16), w_ref[...]) for r in rows]
    ones_row = (lax.broadcasted_iota(jnp.int32, (V_ROWS - HEAD_DIM, th), 0) == 0).astype(BF16)
    for k, (r, proj) in enumerate(zip(rows, projs)):
        cosf, sina, sinb = cos_ref[0, r, :], sina_ref[0, r, :], sinb_ref[0, r, :]

        def rope(t):
            return (t * cosf + pltpu.roll(t, HEAD_DIM - ROPE_DIM // 2, 1) * sina
                    + pltpu.roll(t, ROPE_DIM // 2, 1) * sinb)

        def col(j):
            return proj[:, j * HEAD_DIM:(j + 1) * HEAD_DIM]

        for hd in range(NSA_HEADS):
            q_ref[0, hd, :, r] = (rope(col(hd)) * (ATT_SCALE * LOG2_E)).T.astype(BF16)
        row = pl.program_id(1) * tm + k * th + lax.broadcasted_iota(jnp.int32, (th, n_slc), 0)
        blk = lax.broadcasted_iota(jnp.int32, (th, n_slc), 1)
        sel_onehot = (row // SLC_BLOCK == blk).astype(BF16)
        for g in range(g2):
            cmp_ref[0, g, r, :] = rope(col(kv0 + g))
            cmp_ref[0, g2 + g, r, :] = col(kv0 + g2 + g)
            ksel_ref[0, g, r, 0:HEAD_DIM] = rope(col(kv0 + 2 * g2 + g)).astype(BF16)
            ksel_ref[0, g, r, HEAD_DIM:] = sel_onehot
            kwin_ref[0, g, r, :] = rope(col(kv0 + 4 * g2 + g)).astype(BF16)
            for v_ref, j in ((vsel_ref, kv0 + 3 * g2 + g), (vwin_ref, kv0 + 5 * g2 + g)):
                v_ref[0, g, 0:HEAD_DIM, r] = col(j).T.astype(BF16)
                v_ref[0, g, HEAD_DIM:, r] = ones_row
        gate_ref[0, :, r] = jax.nn.sigmoid(proj[:, (kv0 + 6 * g2) * HEAD_DIM:]).T[0:GATE_ROWS]


def _proj_call(x, gain, w_p, layer, cosf, sina, sinb):
    b, s, d = x.shape
    tm = PROJ_TM
    n_slc = s // SLC_BLOCK
    g2 = NSA_KV_GROUPS
    wp_cols = w_p.shape[2]
    tok = lambda bi, i: (bi, i, 0)
    grp = lambda bi, i: (bi, 0, i, 0)
    grp_t = lambda bi, i: (bi, 0, 0, i)
    return pl.pallas_call(
        functools.partial(_proj_kernel, n_slc=n_slc),
        grid=(b, s // tm),
        in_specs=[
            pl.BlockSpec((1, tm, d), tok),
            pl.BlockSpec((1, d), lambda bi, i: (0, 0)),
            _resident((None, d, wp_cols), lambda bi, i: (layer, 0, 0)),
            pl.BlockSpec((1, tm, HEAD_DIM), tok),
            pl.BlockSpec((1, tm, HEAD_DIM), tok),
            pl.BlockSpec((1, tm, HEAD_DIM), tok),
        ],
        out_specs=[
            pl.BlockSpec((1, NSA_HEADS, HEAD_DIM, tm), grp_t),
            pl.BlockSpec((1, GATE_ROWS, tm), lambda bi, i: (bi, 0, i)),
            pl.BlockSpec((1, 2 * g2, tm, HEAD_DIM), grp),
            pl.BlockSpec((1, g2, tm, HEAD_DIM + n_slc), grp),
            pl.BlockSpec((1, g2, V_ROWS, tm), grp_t),
            pl.BlockSpec((1, g2, tm, HEAD_DIM), grp),
            pl.BlockSpec((1, g2, V_ROWS, tm), grp_t),
        ],
        out_shape=[
            jax.ShapeDtypeStruct((b, NSA_HEADS, HEAD_DIM, s), BF16),
            jax.ShapeDtypeStruct((b, GATE_ROWS, s), F32),
            jax.ShapeDtypeStruct((b, 2 * g2, s, HEAD_DIM), F32),
            jax.ShapeDtypeStruct((b, g2, s, HEAD_DIM + n_slc), BF16),
            jax.ShapeDtypeStruct((b, g2, V_ROWS, s), BF16),
            jax.ShapeDtypeStruct((b, g2, s, HEAD_DIM), BF16),
            jax.ShapeDtypeStruct((b, g2, V_ROWS, s), BF16),
        ],
        compiler_params=pltpu.CompilerParams(
            dimension_semantics=("arbitrary", "arbitrary"), vmem_limit_bytes=VMEM_LIMIT),
        name="nsa_proj",
    )(x, gain, w_p, cosf, sina, sinb)


def _compress_kernel(x_ref, pe_ref, w1_ref, w2_ref, o_ref, ot_ref):
    n_chunk = o_ref.shape[2]
    first = jnp.zeros((n_chunk, HEAD_DIM), F32)
    second = jnp.zeros((n_chunk, HEAD_DIM), F32)
    for c in range(CMP_STRIDE):
        rows = x_ref[0, 0, pl.ds(c, n_chunk, stride=CMP_STRIDE), :]
        first = first + _dot((rows + pe_ref[0, c:c + 1, :]).astype(BF16), w1_ref[0, c])
        second = second + _dot((rows + pe_ref[0, CMP_STRIDE + c:CMP_STRIDE + c + 1, :]).astype(BF16),
                               w1_ref[0, CMP_STRIDE + c])
    hidden = first + pltpu.roll(second, n_chunk - 1, 0)
    out = _dot(_silu(hidden).astype(BF16), w2_ref[0])
    o_ref[0, 0] = out.astype(BF16)
    ot_ref[0, 0] = out.T.astype(BF16)


def _compress_call(cmp_in, pe, w1, w2):
    b, n4, s, dh = cmp_in.shape
    g2 = NSA_KV_GROUPS
    n_chunk = s // CMP_STRIDE
    return pl.pallas_call(
        _compress_kernel,
        grid=(b, n4),
        in_specs=[
            pl.BlockSpec((1, 1, s, dh), lambda bi, j: (bi, j, 0, 0)),
            pl.BlockSpec((1, CMP_BLOCK, dh), lambda bi, j: (j // g2, 0, 0)),
            pl.BlockSpec((1, CMP_BLOCK, dh, dh), lambda bi, j: (j // g2, 0, 0, 0)),
            pl.BlockSpec((1, dh, dh), lambda bi, j: (j // g2, 0, 0)),
        ],
        out_specs=[pl.BlockSpec((1, 1, n_chunk, dh), lambda bi, j: (bi, j, 0, 0)),
                   pl.BlockSpec((1, 1, dh, n_chunk), lambda bi, j: (bi, j, 0, 0))],
        out_shape=[jax.ShapeDtypeStruct((b, n4, n_chunk, dh), BF16),
                   jax.ShapeDtypeStruct((b, n4, dh, n_chunk), BF16)],
        compiler_params=pltpu.CompilerParams(
            dimension_semantics=("arbitrary", "arbitrary"), vmem_limit_bytes=VMEM_LIMIT),
        name="nsa_compress",
    )(cmp_in, pe, w1, w2)


def _nsa_kernel(q_ref, gate_ref, cmp_ref, cmpt_ref, ksel_ref, vsel_ref, kwin_ref, vwin_ref, mcs_ref, o_ref,
                lhs_ref, m_ref, acc_ref, part_ref, *, seq):
    tq = q_ref.shape[3]
    assert tq == NSA_KC
    rows = HEADS_PER_GROUP * tq
    half = rows // 2
    n_chunk = seq // CMP_STRIDE
    n_cmp = n_chunk - 1
    n_slc = seq // SLC_BLOCK
    groups = range(NSA_KV_GROUPS)
    i = pl.program_id(1)
    t0 = i * tq
    tok = t0 + lax.broadcasted_iota(jnp.int32, (1, tq), 1)
    tpos = jnp.concatenate([tok] * HEADS_PER_GROUP, axis=1)

    def mask_bias(mask):
        return jnp.where(mask, 0.0, NEG_INF)

    def mask_heads(s, bias):
        return jnp.concatenate([s[:, r * tq:(r + 1) * tq] + bias for r in range(s.shape[1] // tq)], axis=1)

    def gate_row(g, branch):
        return jnp.concatenate(
            [gate_ref[0, 3 * (HEADS_PER_GROUP * g + r) + branch:3 * (HEADS_PER_GROUP * g + r) + branch + 1, :]
             for r in range(HEADS_PER_GROUP)], axis=1)

    def normalise(acc_t, gate):
        return acc_t[0:HEAD_DIM] * (gate / acc_t[HEAD_DIM:HEAD_DIM + 1])

    def run_pipelined(items):
        scores = [score_fn() for score_fn, _ in items]
        for k, (_, consume_fn) in enumerate(items):
            consume_fn(scores[k])
            scores[k] = None

    def q_group(g):
        return jnp.concatenate([q_ref[0, HEADS_PER_GROUP * g + r] for r in range(HEADS_PER_GROUP)], axis=1)

    def window_items():
        wlen = WINDOW + tq
        w0 = pl.multiple_of(jnp.maximum(t0 - WINDOW, 0), tq)
        rel = tok - (w0 + lax.broadcasted_iota(jnp.int32, (wlen, 1), 0))
        win_bias = mask_bias(lax.bitcast_convert_type(rel, jnp.uint32) < WINDOW)

        def win_score(g):
            return _dot(kwin_ref[0, g, pl.ds(w0, wlen), :], q_group(g))

        def win_consume(g, s_w):
            s_w = mask_heads(s_w, win_bias)
            p_w = jnp.exp2(s_w - jnp.max(s_w, axis=0, keepdims=True)).astype(BF16)
            part_ref[1, g] = normalise(_dot(vwin_ref[0, g, :, pl.ds(w0, wlen)], p_w), gate_row(g, 2))

        return [(functools.partial(win_score, g), functools.partial(win_consume, g)) for g in groups]

    for g in groups:
        lhs_ref[g, 0:HEAD_DIM, :] = q_group(g)
        m_ref[g] = jnp.full((1, rows), NEG_INF, F32)
        acc_ref[g] = jnp.zeros((V_ROWS, rows), F32)

    def select_blocks(vis_chunk):
        vis_slc = vis_chunk * CMP_STRIDE // SLC_BLOCK
        nrow = lax.broadcasted_iota(jnp.int32, (vis_chunk, 1), 0)
        cmp_bias = mask_bias(nrow <= jnp.minimum((tok - (CMP_BLOCK - 1)) // CMP_STRIDE, n_cmp - 1))
        sees_block = tpos >= CMP_BLOCK - 1
        imps = [None] * NSA_KV_GROUPS

        def cmp_score(g):
            return _dot(cmp_ref[0, g, 0:vis_chunk, :], q_group(g))

        def cmp_consume(g, s_c):
            s_c = mask_heads(s_c, cmp_bias)
            e_c = jnp.exp2(s_c - jnp.max(s_c, axis=0, keepdims=True))
            p_c = e_c * jnp.where(sees_block, 1.0 / jnp.sum(e_c, axis=0, keepdims=True), 0.0)
            part_ref[0, g] = (_dot(cmpt_ref[0, NSA_KV_GROUPS + g, :, 0:vis_chunk], p_c.astype(BF16))
                              * gate_row(g, 0))
            p_sum = p_c[:, 0:tq]
            for r in range(1, HEADS_PER_GROUP):
                p_sum = p_sum + p_c[:, r * tq:(r + 1) * tq]
            p_hi = p_sum.astype(BF16)
            p_lo = (p_sum - p_hi.astype(F32)).astype(BF16)
            mcs_t = mcs_ref[0:vis_slc, 0:vis_chunk]
            imps[g] = _dot(mcs_t, p_hi) + _dot(mcs_t, p_lo)

        run_pipelined([(functools.partial(cmp_score, g), functools.partial(cmp_consume, g)) for g in groups]
                      + window_items())

        jrow = lax.broadcasted_iota(jnp.int32, (vis_slc, tq), 0)
        cur = (t0 + lax.broadcasted_iota(jnp.int32, (vis_slc, tq), 1)) // SLC_BLOCK
        forced = (jrow == 0) | (jrow == cur) | (jrow == cur - 1)
        jrow_f = jrow.astype(F32)
        for g in groups:
            cand = jnp.where(forced, -jnp.inf, jnp.where(jrow <= cur, imps[g], -1.0))
            for _ in range(min(N_SELECT, n_slc) - 3):
                best = jnp.max(cand, axis=0, keepdims=True)
                first = jnp.min(jnp.where(cand == best, jrow_f, float(n_slc)), axis=0, keepdims=True)
                cand = jnp.where(jrow_f == first, -jnp.inf, cand)
            sel_bias = jnp.where(cand == -jnp.inf, 0.0, NEG_INF).astype(BF16)
            lhs_ref[g, HEAD_DIM:HEAD_DIM + vis_slc, :] = jnp.concatenate([sel_bias] * HEADS_PER_GROUP, axis=1)
            if vis_slc < n_slc:
                lhs_ref[g, HEAD_DIM + vis_slc:, :] = jnp.full((n_slc - vis_slc, rows), NEG_INF, BF16)

    chunk_step = n_chunk // CMP_VARIANTS
    variant = ((t0 + tq) // CMP_STRIDE - 1) // chunk_step
    for v in range(CMP_VARIANTS):
        pl.when(variant == v)(functools.partial(select_blocks, chunk_step * (v + 1)))

    def sel_score(g, hf, k0, width):
        return _dot(ksel_ref[0, g, pl.ds(k0, width), :], lhs_ref[g, :, hf * half:(hf + 1) * half])

    def sel_consume(g, hf, k0, width, causal_bias, s_s):
        lanes = slice(hf * half, (hf + 1) * half)
        if causal_bias is not None:
            s_s = mask_heads(s_s, causal_bias)
        m_old = m_ref[g, :, lanes]
        m_new = jnp.maximum(m_old, jnp.max(s_s, axis=0, keepdims=True))
        p_s = jnp.exp2(s_s - m_new).astype(BF16)
        acc_ref[g, :, lanes] = (jnp.exp2(m_old - m_new) * acc_ref[g, :, lanes]
                                + _dot(vsel_ref[0, g, :, pl.ds(k0, width)], p_s))
        m_ref[g, :, lanes] = m_new

    def sweep(first_chunk, n_chunks, last_masked=False):
        k0 = pl.multiple_of(first_chunk * NSA_KC, NSA_KC)
        width = n_chunks * NSA_KC
        causal_bias = None
        if last_masked:
            causal_bias = mask_bias(k0 + lax.broadcasted_iota(jnp.int32, (width, 1), 0) <= tok)
        run_pipelined([(functools.partial(sel_score, g, hf, k0, width),
                        functools.partial(sel_consume, g, hf, k0, width, causal_bias))
                       for g in groups for hf in range(2)])

    def octet_body(ko, carry):
        sweep(8 * ko, 8)
        return carry

    lax.fori_loop(0, i // 8, octet_body, 0)
    done = (i // 8) * 8

    @pl.when(i % 8 >= 4)
    def _():
        sweep(done, 4)

    done = (i // 4) * 4

    @pl.when(i % 4 >= 2)
    def _():
        sweep(done, 2)

    done = (i // 2) * 2

    @pl.when(i % 2 == 1)
    def _():
        sweep(done, 2, last_masked=True)

    @pl.when(i % 2 == 0)
    def _():
        sweep(done, 1, last_masked=True)

    for g in groups:
        mixed_t = (part_ref[0, g] + normalise(acc_ref[g], gate_row(g, 1))) + part_ref[1, g]
        for r in range(HEADS_PER_GROUP):
            hd = HEADS_PER_GROUP * g + r
            o_ref[0, :, hd * HEAD_DIM:(hd + 1) * HEAD_DIM] = mixed_t[:, r * tq:(r + 1) * tq].T


def _nsa_call(q, gates_t, cmp, cmp_t, ksel, vsel_t, kwin, vwin_t, mcs_t):
    b, _, _, s = q.shape
    tq = NSA_TQ
    g2 = NSA_KV_GROUPS
    n_chunk = s // CMP_STRIDE
    n_slc = s // SLC_BLOCK
    rows = HEADS_PER_GROUP * tq
    tok = lambda bi, i: (bi, i, 0)
    per_batch = lambda bi, i: (bi, 0, 0, 0)
    return pl.pallas_call(
        functools.partial(_nsa_kernel, seq=s),
        grid=(b, s // tq),
        in_specs=[
            pl.BlockSpec((1, NSA_HEADS, HEAD_DIM, tq), lambda bi, i: (bi, 0, 0, i)),
            pl.BlockSpec((1, GATE_ROWS, tq), lambda bi, i: (bi, 0, i)),
            _resident((1, 2 * g2, n_chunk, HEAD_DIM), per_batch),
            _resident((1, 2 * g2, HEAD_DIM, n_chunk), per_batch),
            _resident((1, g2, s, HEAD_DIM + n_slc), per_batch),
            _resident((1, g2, V_ROWS, s), per_batch),
            _resident((1, g2, s, HEAD_DIM), per_batch),
            _resident((1, g2, V_ROWS, s), per_batch),
            _resident((n_slc, n_chunk), lambda bi, i: (0, 0)),
        ],
        out_specs=pl.BlockSpec((1, tq, NSA_WIDTH), tok),
        out_shape=jax.ShapeDtypeStruct((b, s, NSA_WIDTH), F32),
        scratch_shapes=[
            pltpu.VMEM((g2, HEAD_DIM + n_slc, rows), BF16),
            pltpu.VMEM((g2, 1, rows), F32),
            pltpu.VMEM((g2, V_ROWS, rows), F32),
            pltpu.VMEM((2, g2, HEAD_DIM, rows), F32),
        ],
        compiler_params=pltpu.CompilerParams(
            dimension_semantics=("arbitrary", "arbitrary"), vmem_limit_bytes=VMEM_LIMIT),
        name="nsa_attention",
    )(q, gates_t, cmp, cmp_t, ksel, vsel_t, kwin, vwin_t, mcs_t)


def _memkv_kernel(mem_ref, gain_ref, w_ref, o_ref):
    mem_n = _rmsnorm(mem_ref[0], gain_ref[...]).astype(BF16)
    o_ref[0, 0] = _dot(mem_n, w_ref[0]).astype(BF16)


def _memkv_call(mem, gain, w_mem):
    depth = w_mem.shape[0]
    b, m, d = mem.shape
    return pl.pallas_call(
        _memkv_kernel,
        grid=(depth, b),
        in_specs=[
            pl.BlockSpec((1, m, d), lambda l, bi: (bi, 0, 0)),
            pl.BlockSpec((1, d), lambda l, bi: (0, 0)),
            pl.BlockSpec((1, d, 2 * MEM_WIDTH), lambda l, bi: (l, 0, 0)),
        ],
        out_specs=pl.BlockSpec((1, 1, m, 2 * MEM_WIDTH), lambda l, bi: (l, bi, 0, 0)),
        out_shape=jax.ShapeDtypeStruct((depth, b, m, 2 * MEM_WIDTH), BF16),
        compiler_params=pltpu.CompilerParams(
            dimension_semantics=("arbitrary", "arbitrary"), vmem_limit_bytes=VMEM_LIMIT),
        name="mem_kv",
    )(mem, gain, w_mem)


_F_AUVZ = 0
_F_BZ = 3 * A_WIDTH
_F_CQZ = _F_BZ + NSA_WIDTH
_F_MERGE = _F_CQZ + 2 * MEM_WIDTH


def _fused_kernel(x_ref, ob_ref, gain_ref, w_ref, lng_ref, lnb_ref, wsp_ref, bsp_ref, mkv_ref,
                  wa_ref, wb_ref, wc_ref, wo_ref, *fgain_and_out, final):
    fgain_ref, o_ref = fgain_and_out if final else (None,) + fgain_and_out
    tm, d = x_ref.shape[1], x_ref.shape[2]
    th = tm // FUSE_SPLIT

    def front(x):
        h = _rmsnorm(x, gain_ref[...]).astype(BF16)

        def proj(c0, width):
            return _dot(h, w_ref[:, c0:c0 + width])

        return (proj(_F_AUVZ, 3 * A_WIDTH), proj(_F_CQZ, 2 * MEM_WIDTH), proj(_F_BZ, NSA_WIDTH),
                [proj(_F_MERGE + k * d, d) for k in range(3)])

    def back(x, ob, uvz, cqz, zb, merge):
        u, v, z = uvz[:, 0:A_WIDTH], uvz[:, A_WIDTH:2 * A_WIDTH], uvz[:, 2 * A_WIDTH:]
        vc = v - jnp.mean(v, axis=-1, keepdims=True)
        vn = (vc * lax.rsqrt(jnp.mean(vc * vc, axis=-1, keepdims=True) + EPS) * lng_ref[...]
              + lnb_ref[...]).astype(BF16)
        gdim = A_WIDTH // A_GROUPS
        mixed = jnp.concatenate([
            jnp.concatenate([
                _dot(wsp_ref[gi], vn[c * CHUNK:(c + 1) * CHUNK, gi * gdim:(gi + 1) * gdim]) + bsp_ref[gi]
                for gi in range(A_GROUPS)], axis=1)
            for c in range(th // CHUNK)], axis=0)

        scores = [_nt_dot(cqz[:, hd * HEAD_DIM:(hd + 1) * HEAD_DIM].astype(BF16),
                          mkv_ref[0, 0, :, hd * HEAD_DIM:(hd + 1) * HEAD_DIM]) * ATT_SCALE
                  for hd in range(MEM_HEADS)]

        o_a = (u * mixed * _silu(z)).astype(BF16)
        acc = jax.nn.sigmoid(merge[0]) * _dot(o_a, wa_ref[...])

        o_b = (ob * _silu(zb)).astype(BF16)
        acc = acc + jax.nn.sigmoid(merge[1]) * _dot(o_b, wb_ref[...])

        heads = []
        for hd in range(MEM_HEADS):
            e_m = jnp.exp(scores[hd] - jnp.max(scores[hd], axis=1, keepdims=True))
            p_m = e_m * (1.0 / jnp.sum(e_m, axis=1, keepdims=True))
            heads.append(_dot(p_m.astype(BF16),
                              mkv_ref[0, 0, :, MEM_WIDTH + hd * HEAD_DIM:MEM_WIDTH + (hd + 1) * HEAD_DIM]))
        o_c = (jnp.concatenate(heads, axis=1) * _silu(cqz[:, MEM_WIDTH:])).astype(BF16)
        acc = acc + jax.nn.sigmoid(merge[2]) * _dot(o_c, wc_ref[...])

        x_new = x + _dot(acc.astype(BF16), wo_ref[...])
        if final:
            x_new = _rmsnorm(x_new, fgain_ref[...])
        return x_new

    rows = [slice(k * th, (k + 1) * th) for k in range(FUSE_SPLIT)]
    fronts = [front(x_ref[0, r, :]) for r in rows]
    for r, f in zip(rows, fronts):
        o_ref[0, r, :] = back(x_ref[0, r, :], ob_ref[0, r, :], *f)


def _fused_call(x, ob, gain, w_f, lng, lnb, wsp, bsp, mkv, layer, wa, wb, wc, wo, fgain=None):
    b, s, d = x.shape
    tm = FUSE_TM
    m = mkv.shape[2]
    tok = lambda bi, i: (bi, i, 0)
    c2 = lambda bi, i: (0, 0)
    final = fgain is not None

    def layer_weights(w):
        return _resident((None,) + w.shape[1:], lambda bi, i: (layer,) + (0,) * (w.ndim - 1))

    return pl.pallas_call(
        functools.partial(_fused_kernel, final=final),
        grid=(b, s // tm),
        in_specs=[
            pl.BlockSpec((1, tm, d), tok),
            pl.BlockSpec((1, tm, NSA_WIDTH), tok),
            pl.BlockSpec((1, d), c2),
            layer_weights(w_f),
            pl.BlockSpec((1, A_WIDTH), c2),
            pl.BlockSpec((1, A_WIDTH), c2),
            layer_weights(wsp),
            layer_weights(bsp),
            _resident((1, 1, m, 2 * MEM_WIDTH), lambda bi, i: (layer, bi, 0, 0)),
            layer_weights(wa),
            layer_weights(wb),
            layer_weights(wc),
            layer_weights(wo),
        ] + ([pl.BlockSpec((1, d), c2)] if final else []),
        out_specs=pl.BlockSpec((1, tm, d), tok),
        out_shape=jax.ShapeDtypeStruct((b, s, d), F32),
        compiler_params=pltpu.CompilerParams(
            dimension_semantics=("arbitrary", "arbitrary"), vmem_limit_bytes=VMEM_LIMIT),
        name="fused_mix",
    )(x, ob, gain, w_f, lng, lnb, wsp, bsp, mkv, wa, wb, wc, wo, *([fgain] if final else []))


def _rope_lane_tables(positions):
    inv_freq = ROPE_THETA ** (-jnp.arange(0, ROPE_DIM, 2, dtype=F32) / ROPE_DIM)
    ang = positions.astype(F32)[..., None] * inv_freq
    cos, sin = jnp.cos(ang), jnp.sin(ang)
    zeros = jnp.zeros(cos.shape[:-1] + (HEAD_DIM - ROPE_DIM,), F32)
    cosf = jnp.concatenate([cos, cos, zeros + 1.0], axis=-1)
    sina = jnp.concatenate([-sin, jnp.zeros_like(sin), zeros], axis=-1)
    sinb = jnp.concatenate([jnp.zeros_like(sin), sin, zeros], axis=-1)
    return cosf, sina, sinb


def _cmp_to_slc(n_chunk, n_slc):
    i = np.arange(n_chunk)[:, None] * CMP_STRIDE
    j = np.arange(n_slc)[None, :] * SLC_BLOCK
    ov = np.clip(np.minimum(i + CMP_BLOCK, j + SLC_BLOCK) - np.maximum(i, j), 0, None) / CMP_BLOCK
    ov[n_chunk - 1] = 0.0
    return jnp.asarray(ov.T, dtype=BF16)


def kernel(x, mem, positions, norm_gain, w_in, ln_v_gain, ln_v_bias, w_spatial, b_spatial, cmp_pe_k, cmp_w1_k, cmp_w2_k, cmp_pe_v, cmp_w1_v, cmp_w2_v, mem_norm_gain, w_mem_kv, w_branch_a, w_branch_b, w_branch_c, w_out, final_norm_gain):
    depth = w_in.shape[0]
    b, s, d = x.shape
    assert s % NSA_TQ == 0 and s % PROJ_TM == 0 and s >= WINDOW + NSA_TQ and d == w_in.shape[1]

    cosf, sina, sinb = _rope_lane_tables(positions)
    mcs = _cmp_to_slc(s // CMP_STRIDE, s // SLC_BLOCK)

    gate_w = jnp.pad(w_in[:, :, _B_G:_B_KV], ((0, 0), (0, 0), (0, GATE_PAD - (_B_KV - _B_G))))
    w_p = jnp.concatenate([w_in[:, :, _B_Q:_B_Z], w_in[:, :, _B_KV:_C_Q], gate_w], axis=-1).astype(BF16)
    w_f = jnp.concatenate([w_in[:, :, _A_U:_B_Q], w_in[:, :, _B_Z:_B_G], w_in[:, :, _C_Q:_MERGE],
                           w_in[:, :, _MERGE:]], axis=-1).astype(BF16)
    wsp = (w_spatial * jnp.tril(jnp.ones((CHUNK, CHUNK), w_spatial.dtype))).astype(BF16)
    bsp = jnp.broadcast_to(b_spatial[..., None], b_spatial.shape + (A_WIDTH // A_GROUPS,))
    pe = jnp.stack([cmp_pe_k, cmp_pe_v], axis=1)
    w1 = jnp.stack([cmp_w1_k, cmp_w1_v], axis=1).astype(BF16)
    w2 = jnp.stack([cmp_w2_k, cmp_w2_v], axis=1).astype(BF16)
    wa, wb, wc, wo = (w.astype(BF16) for w in (w_branch_a, w_branch_b, w_branch_c, w_out))

    mkv = _memkv_call(mem, mem_norm_gain[None, :], w_mem_kv.astype(BF16))
    fgain = final_norm_gain[None, :]
    for l in range(depth):
        gain = norm_gain[l][None, :]
        q, gates_t, cmp_in, ksel, vsel_t, kwin, vwin_t = _proj_call(x, gain, w_p, l, cosf, sina, sinb)
        cmp, cmp_t = _compress_call(cmp_in, pe[l], w1[l], w2[l])
        ob = _nsa_call(q, gates_t, cmp, cmp_t, ksel, vsel_t, kwin, vwin_t, mcs)
        x = _fused_call(x, ob, gain, w_f, ln_v_gain[l][None, :], ln_v_bias[l][None, :], wsp, bsp,
                        mkv, l, wa, wb, wc, wo, fgain if l == depth - 1 else None)
    return x
```

```python
import functools

import numpy as np
import jax
import jax.numpy as jnp
from jax import lax
from jax.experimental import pallas as pl
from jax.experimental.pallas import tpu as pltpu

F32 = jnp.float32
BF16 = jnp.bfloat16

HEAD_DIM = 128
A_GROUPS = 4
A_WIDTH = 512
CHUNK = 128
NSA_HEADS = 8
NSA_KV_GROUPS = 2
HEADS_PER_GROUP = NSA_HEADS // NSA_KV_GROUPS
NSA_WIDTH = NSA_HEADS * HEAD_DIM
CMP_BLOCK = 32
CMP_STRIDE = 16
SLC_BLOCK = 64
N_SELECT = 16
WINDOW = 512
MEM_HEADS = 4
MEM_WIDTH = MEM_HEADS * HEAD_DIM
ROPE_DIM = HEAD_DIM // 4
ROPE_THETA = 500000.0
EPS = 1e-6
NEG_INF = -1e30
FORCED_SCORE = 1e4
ATT_SCALE = HEAD_DIM ** -0.5
LOG2_E = 1.4426950408889634

_OFF = np.cumsum((0, A_WIDTH, A_WIDTH, A_WIDTH, NSA_WIDTH, NSA_WIDTH, NSA_HEADS * 3, 3 * 2 * NSA_KV_GROUPS * HEAD_DIM,
                  MEM_WIDTH, MEM_WIDTH))
(_A_U, _A_V, _A_Z, _B_Q, _B_Z, _B_G, _B_KV, _C_Q, _C_Z, _MERGE) = (int(v) for v in _OFF)
GATE_PAD = 128
GATE_ROWS = 32
V_ROWS = HEAD_DIM + 16

PROJ_TM = 512
PROJ_SPLIT = 2
NSA_TQ = 256
NSA_KC = 256
CMP_VARIANTS = 4
FUSE_TM = 512
FUSE_SPLIT = 2
VMEM_LIMIT = 56 * 1024 * 1024


def _nt_dot(a, b):
    return lax.dot_general(a, b, (((1,), (1,)), ((), ())), preferred_element_type=F32)


def _dot(a, b):
    return jnp.dot(a, b, preferred_element_type=F32)


def _rmsnorm(x, gain):
    return x * lax.rsqrt(jnp.mean(x * x, axis=-1, keepdims=True) + EPS) * gain


def _silu(x):
    return x * jax.nn.sigmoid(x)


def _resident(block_shape, index_map):
    return pl.BlockSpec(block_shape, index_map, pipeline_mode=pl.Buffered(1))


def _proj_kernel(x_ref, gain_ref, w_ref, cos_ref, sina_ref, sinb_ref,
                 q_ref, gate_ref, cmp_ref, ksel_ref, vsel_ref, kwin_ref, vwin_ref, *, n_slc):
    tm = x_ref.shape[1]
    th = tm // PROJ_SPLIT
    g2 = NSA_KV_GROUPS
    kv0 = NSA_HEADS
    rows = [slice(k * th, (k + 1) * th) for k in range(PROJ_SPLIT)]
    projs = [_dot(_rmsnorm(x_ref[0, r, :], gain_ref[...]).astype(BF16), w_ref[...]) for r in rows]
    ones_row = (lax.broadcasted_iota(jnp.int32, (V_ROWS - HEAD_DIM, th), 0) == 0).astype(BF16)
    for k, (r, proj) in enumerate(zip(rows, projs)):
        cosf, sina, sinb = cos_ref[0, r, :], sina_ref[0, r, :], sinb_ref[0, r, :]

        def rope(t):
            return (t * cosf + pltpu.roll(t, HEAD_DIM - ROPE_DIM // 2, 1) * sina
                    + pltpu.roll(t, ROPE_DIM // 2, 1) * sinb)

        def col(j):
            return proj[:, j * HEAD_DIM:(j + 1) * HEAD_DIM]

        for hd in range(NSA_HEADS):
            q_ref[0, hd, :, r] = (rope(col(hd)) * (ATT_SCALE * LOG2_E)).T.astype(BF16)
        row = pl.program_id(1) * tm + k * th + lax.broadcasted_iota(jnp.int32, (th, n_slc), 0)
        blk = lax.broadcasted_iota(jnp.int32, (th, n_slc), 1)
        sel_onehot = (row // SLC_BLOCK == blk).astype(BF16)
        for g in range(g2):
            cmp_ref[0, g, r, :] = rope(col(kv0 + g))
            cmp_ref[0, g2 + g, r, :] = col(kv0 + g2 + g)
            ksel_ref[0, g, r, 0:HEAD_DIM] = rope(col(kv0 + 2 * g2 + g)).astype(BF16)
            ksel_ref[0, g, r, HEAD_DIM:] = sel_onehot
            kwin_ref[0, g, r, :] = rope(col(kv0 + 4 * g2 + g)).astype(BF16)
            for v_ref, j in ((vsel_ref, kv0 + 3 * g2 + g), (vwin_ref, kv0 + 5 * g2 + g)):
                v_ref[0, g, 0:HEAD_DIM, r] = col(j).T.astype(BF16)
                v_ref[0, g, HEAD_DIM:, r] = ones_row
        gate_ref[0, :, r] = jax.nn.sigmoid(proj[:, (kv0 + 6 * g2) * HEAD_DIM:]).T[0:GATE_ROWS]


def _proj_call(x, gain, w_p, layer, cosf, sina, sinb):
    b, s, d = x.shape
    tm = PROJ_TM
    n_slc = s // SLC_BLOCK
    g2 = NSA_KV_GROUPS
    wp_cols = w_p.shape[2]
    tok = lambda bi, i: (bi, i, 0)
    grp = lambda bi, i: (bi, 0, i, 0)
    grp_t = lambda bi, i: (bi, 0, 0, i)
    return pl.pallas_call(
        functools.partial(_proj_kernel, n_slc=n_slc),
        grid=(b, s // tm),
        in_specs=[
            pl.BlockSpec((1, tm, d), tok),
            pl.BlockSpec((1, d), lambda bi, i: (0, 0)),
            _resident((None, d, wp_cols), lambda bi, i: (layer, 0, 0)),
            pl.BlockSpec((1, tm, HEAD_DIM), tok),
            pl.BlockSpec((1, tm, HEAD_DIM), tok),
            pl.BlockSpec((1, tm, HEAD_DIM), tok),
        ],
        out_specs=[
            pl.BlockSpec((1, NSA_HEADS, HEAD_DIM, tm), grp_t),
            pl.BlockSpec((1, GATE_ROWS, tm), lambda bi, i: (bi, 0, i)),
            pl.BlockSpec((1, 2 * g2, tm, HEAD_DIM), grp),
            pl.BlockSpec((1, g2, tm, HEAD_DIM + n_slc), grp),
            pl.BlockSpec((1, g2, V_ROWS, tm), grp_t),
            pl.BlockSpec((1, g2, tm, HEAD_DIM), grp),
            pl.BlockSpec((1, g2, V_ROWS, tm), grp_t),
        ],
        out_shape=[
            jax.ShapeDtypeStruct((b, NSA_HEADS, HEAD_DIM, s), BF16),
            jax.ShapeDtypeStruct((b, GATE_ROWS, s), F32),
            jax.ShapeDtypeStruct((b, 2 * g2, s, HEAD_DIM), F32),
            jax.ShapeDtypeStruct((b, g2, s, HEAD_DIM + n_slc), BF16),
            jax.ShapeDtypeStruct((b, g2, V_ROWS, s), BF16),
            jax.ShapeDtypeStruct((b, g2, s, HEAD_DIM), BF16),
            jax.ShapeDtypeStruct((b, g2, V_ROWS, s), BF16),
        ],
        compiler_params=pltpu.CompilerParams(
            dimension_semantics=("arbitrary", "arbitrary"), vmem_limit_bytes=VMEM_LIMIT),
        name="nsa_proj",
    )(x, gain, w_p, cosf, sina, sinb)


def _compress_kernel(x_ref, pe_ref, w1_ref, w2_ref, o_ref, ot_ref):
    n_chunk = o_ref.shape[2]
    tokens = [x_ref[0, 0, pl.ds(c, n_chunk, stride=CMP_STRIDE), :] for c in range(CMP_STRIDE)]

    def half_block(first_pos):
        lhs = jnp.concatenate([(tokens[c] + pe_ref[0, first_pos + c:first_pos + c + 1, :]).astype(BF16)
                               for c in range(CMP_STRIDE)], axis=1)
        return _dot(lhs, w1_ref[0, first_pos // CMP_STRIDE])

    first, second = half_block(0), half_block(CMP_STRIDE)
    hidden = first + pltpu.roll(second, n_chunk - 1, 0)
    out = _dot(_silu(hidden).astype(BF16), w2_ref[0])
    o_ref[0, 0] = out.astype(BF16)
    ot_ref[0, 0] = out.T.astype(BF16)


def _compress_call(cmp_in, pe, w1, w2):
    b, n4, s, dh = cmp_in.shape
    g2 = NSA_KV_GROUPS
    n_chunk = s // CMP_STRIDE
    return pl.pallas_call(
        _compress_kernel,
        grid=(b, n4),
        in_specs=[
            pl.BlockSpec((1, 1, s, dh), lambda bi, j: (bi, j, 0, 0)),
            pl.BlockSpec((1, CMP_BLOCK, dh), lambda bi, j: (j // g2, 0, 0)),
            pl.BlockSpec((1, 2, CMP_STRIDE * dh, dh), lambda bi, j: (j // g2, 0, 0, 0)),
            pl.BlockSpec((1, dh, dh), lambda bi, j: (j // g2, 0, 0)),
        ],
        out_specs=[pl.BlockSpec((1, 1, n_chunk, dh), lambda bi, j: (bi, j, 0, 0)),
                   pl.BlockSpec((1, 1, dh, n_chunk), lambda bi, j: (bi, j, 0, 0))],
        out_shape=[jax.ShapeDtypeStruct((b, n4, n_chunk, dh), BF16),
                   jax.ShapeDtypeStruct((b, n4, dh, n_chunk), BF16)],
        compiler_params=pltpu.CompilerParams(
            dimension_semantics=("arbitrary", "arbitrary"), vmem_limit_bytes=VMEM_LIMIT),
        name="nsa_compress",
    )(cmp_in, pe, w1, w2)


def _nsa_kernel(q_ref, gate_ref, cmp_ref, cmpt_ref, ksel_ref, vsel_ref, kwin_ref, vwin_ref, mcs_ref, o_ref,
                lhs_ref, m_ref, acc_ref, part_ref, *, seq):
    tq = q_ref.shape[3]
    assert tq == NSA_KC
    rows = HEADS_PER_GROUP * tq
    half = rows // 2
    n_chunk = seq // CMP_STRIDE
    n_cmp = n_chunk - 1
    n_slc = seq // SLC_BLOCK
    groups = range(NSA_KV_GROUPS)
    i = pl.program_id(1)
    t0 = i * tq
    tok = t0 + lax.broadcasted_iota(jnp.int32, (1, tq), 1)
    tpos = jnp.concatenate([tok] * HEADS_PER_GROUP, axis=1)

    def mask_bias(mask):
        return jnp.where(mask, 0.0, NEG_INF)

    def mask_heads(s, bias):
        return jnp.concatenate([s[:, r * tq:(r + 1) * tq] + bias for r in range(s.shape[1] // tq)], axis=1)

    def gate_row(g, branch):
        return jnp.concatenate(
            [gate_ref[0, 3 * (HEADS_PER_GROUP * g + r) + branch:3 * (HEADS_PER_GROUP * g + r) + branch + 1, :]
             for r in range(HEADS_PER_GROUP)], axis=1)

    def normalise(acc_t, gate):
        return acc_t[0:HEAD_DIM] * (gate / acc_t[HEAD_DIM:HEAD_DIM + 1])

    def run_pipelined(items):
        scores = [score_fn() for score_fn, _ in items]
        for k, (_, consume_fn) in enumerate(items):
            consume_fn(scores[k])
            scores[k] = None

    def q_group(g):
        return jnp.concatenate([q_ref[0, HEADS_PER_GROUP * g + r] for r in range(HEADS_PER_GROUP)], axis=1)

    def window_items():
        wlen = WINDOW + tq
        w0 = pl.multiple_of(jnp.maximum(t0 - WINDOW, 0), tq)
        rel = tok - (w0 + lax.broadcasted_iota(jnp.int32, (wlen, 1), 0))
        win_bias = mask_bias(lax.bitcast_convert_type(rel, jnp.uint32) < WINDOW)

        def win_score(g):
            return _dot(kwin_ref[0, g, pl.ds(w0, wlen), :], q_group(g))

        def win_consume(g, s_w):
            s_w = mask_heads(s_w, win_bias)
            p_w = jnp.exp2(s_w - jnp.max(s_w, axis=0, keepdims=True)).astype(BF16)
            part_ref[1, g] = normalise(_dot(vwin_ref[0, g, :, pl.ds(w0, wlen)], p_w), gate_row(g, 2))

        return [(functools.partial(win_score, g), functools.partial(win_consume, g)) for g in groups]

    for g in groups:
        lhs_ref[g, 0:HEAD_DIM, :] = q_group(g)
        m_ref[g] = jnp.full((1, rows), NEG_INF, F32)
        acc_ref[g] = jnp.zeros((V_ROWS, rows), F32)

    def select_blocks(vis_chunk):
        vis_slc = vis_chunk * CMP_STRIDE // SLC_BLOCK
        nrow = lax.broadcasted_iota(jnp.int32, (vis_chunk, 1), 0)
        cmp_bias = mask_bias(nrow <= jnp.minimum((tok - (CMP_BLOCK - 1)) // CMP_STRIDE, n_cmp - 1))
        sees_block = tpos >= CMP_BLOCK - 1
        imps = [None] * NSA_KV_GROUPS

        def cmp_score(g):
            return _dot(cmp_ref[0, g, 0:vis_chunk, :], q_group(g))

        def cmp_consume(g, s_c):
            s_c = mask_heads(s_c, cmp_bias)
            e_c = jnp.exp2(s_c - jnp.max(s_c, axis=0, keepdims=True))
            p_c = e_c * jnp.where(sees_block, 1.0 / jnp.sum(e_c, axis=0, keepdims=True), 0.0)
            part_ref[0, g] = (_dot(cmpt_ref[0, NSA_KV_GROUPS + g, :, 0:vis_chunk], p_c.astype(BF16))
                              * gate_row(g, 0))
            p_sum = p_c[:, 0:tq]
            for r in range(1, HEADS_PER_GROUP):
                p_sum = p_sum + p_c[:, r * tq:(r + 1) * tq]
            p_hi = p_sum.astype(BF16)
            p_lo = (p_sum - p_hi.astype(F32)).astype(BF16)
            mcs_t = mcs_ref[0:vis_slc, 0:vis_chunk]
            imps[g] = _dot(mcs_t, p_hi) + _dot(mcs_t, p_lo)

        run_pipelined([(functools.partial(cmp_score, g), functools.partial(cmp_consume, g)) for g in groups]
                      + window_items())

        jrow = lax.broadcasted_iota(jnp.int32, (vis_slc, tq), 0)
        cur = (t0 + lax.broadcasted_iota(jnp.int32, (vis_slc, tq), 1)) // SLC_BLOCK
        forced = (jrow == 0) | (jrow == cur) | (jrow == cur - 1)
        jrow_f = jrow.astype(F32)
        for g in groups:
            cand = jnp.where(forced, -jnp.inf, jnp.where(jrow <= cur, imps[g], -1.0))
            for _ in range(min(N_SELECT, n_slc) - 3):
                best = jnp.max(cand, axis=0, keepdims=True)
                first = jnp.min(jnp.where(cand == best, jrow_f, float(n_slc)), axis=0, keepdims=True)
                cand = jnp.where(jrow_f == first, -jnp.inf, cand)
            sel_bias = jnp.where(cand == -jnp.inf, 0.0, NEG_INF).astype(BF16)
            lhs_ref[g, HEAD_DIM:HEAD_DIM + vis_slc, :] = jnp.concatenate([sel_bias] * HEADS_PER_GROUP, axis=1)
            if vis_slc < n_slc:
                lhs_ref[g, HEAD_DIM + vis_slc:, :] = jnp.full((n_slc - vis_slc, rows), NEG_INF, BF16)

    chunk_step = n_chunk // CMP_VARIANTS
    variant = ((t0 + tq) // CMP_STRIDE - 1) // chunk_step
    for v in range(CMP_VARIANTS):
        pl.when(variant == v)(functools.partial(select_blocks, chunk_step * (v + 1)))

    def sel_score(g, hf, k0, width):
        return _dot(ksel_ref[0, g, pl.ds(k0, width), :], lhs_ref[g, :, hf * half:(hf + 1) * half])

    def sel_consume(g, hf, k0, width, causal_bias, s_s):
        lanes = slice(hf * half, (hf + 1) * half)
        if causal_bias is not None:
            s_s = mask_heads(s_s, causal_bias)
        m_old = m_ref[g, :, lanes]
        m_new = jnp.maximum(m_old, jnp.max(s_s, axis=0, keepdims=True))
        p_s = jnp.exp2(s_s - m_new).astype(BF16)
        acc_ref[g, :, lanes] = (jnp.exp2(m_old - m_new) * acc_ref[g, :, lanes]
                                + _dot(vsel_ref[0, g, :, pl.ds(k0, width)], p_s))
        m_ref[g, :, lanes] = m_new

    def sweep(first_chunk, n_chunks, last_masked=False):
        k0 = pl.multiple_of(first_chunk * NSA_KC, NSA_KC)
        width = n_chunks * NSA_KC
        causal_bias = None
        if last_masked:
            causal_bias = mask_bias(k0 + lax.broadcasted_iota(jnp.int32, (width, 1), 0) <= tok)
        run_pipelined([(functools.partial(sel_score, g, hf, k0, width),
                        functools.partial(sel_consume, g, hf, k0, width, causal_bias))
                       for g in groups for hf in range(2)])

    def octet_body(ko, carry):
        sweep(8 * ko, 8)
        return carry

    lax.fori_loop(0, i // 8, octet_body, 0)
    done = (i // 8) * 8

    @pl.when(i % 8 >= 4)
    def _():
        sweep(done, 4)

    done = (i // 4) * 4

    @pl.when(i % 4 >= 2)
    def _():
        sweep(done, 2)

    done = (i // 2) * 2

    @pl.when(i % 2 == 1)
    def _():
        sweep(done, 2, last_masked=True)

    @pl.when(i % 2 == 0)
    def _():
        sweep(done, 1, last_masked=True)

    for g in groups:
        mixed_t = (part_ref[0, g] + normalise(acc_ref[g], gate_row(g, 1))) + part_ref[1, g]
        for r in range(HEADS_PER_GROUP):
            hd = HEADS_PER_GROUP * g + r
            o_ref[0, :, hd * HEAD_DIM:(hd + 1) * HEAD_DIM] = mixed_t[:, r * tq:(r + 1) * tq].T


def _nsa_call(q, gates_t, cmp, cmp_t, ksel, vsel_t, kwin, vwin_t, mcs_t):
    b, _, _, s = q.shape
    tq = NSA_TQ
    g2 = NSA_KV_GROUPS
    n_chunk = s // CMP_STRIDE
    n_slc = s // SLC_BLOCK
    rows = HEADS_PER_GROUP * tq
    tok = lambda bi, i: (bi, i, 0)
    per_batch = lambda bi, i: (bi, 0, 0, 0)
    return pl.pallas_call(
        functools.partial(_nsa_kernel, seq=s),
        grid=(b, s // tq),
        in_specs=[
            pl.BlockSpec((1, NSA_HEADS, HEAD_DIM, tq), lambda bi, i: (bi, 0, 0, i)),
            pl.BlockSpec((1, GATE_ROWS, tq), lambda bi, i: (bi, 0, i)),
            _resident((1, 2 * g2, n_chunk, HEAD_DIM), per_batch),
            _resident((1, 2 * g2, HEAD_DIM, n_chunk), per_batch),
            _resident((1, g2, s, HEAD_DIM + n_slc), per_batch),
            _resident((1, g2, V_ROWS, s), per_batch),
            _resident((1, g2, s, HEAD_DIM), per_batch),
            _resident((1, g2, V_ROWS, s), per_batch),
            _resident((n_slc, n_chunk), lambda bi, i: (0, 0)),
        ],
        out_specs=pl.BlockSpec((1, tq, NSA_WIDTH), tok),
        out_shape=jax.ShapeDtypeStruct((b, s, NSA_WIDTH), F32),
        scratch_shapes=[
            pltpu.VMEM((g2, HEAD_DIM + n_slc, rows), BF16),
            pltpu.VMEM((g2, 1, rows), F32),
            pltpu.VMEM((g2, V_ROWS, rows), F32),
            pltpu.VMEM((2, g2, HEAD_DIM, rows), F32),
        ],
        compiler_params=pltpu.CompilerParams(
            dimension_semantics=("arbitrary", "arbitrary"), vmem_limit_bytes=VMEM_LIMIT),
        name="nsa_attention",
    )(q, gates_t, cmp, cmp_t, ksel, vsel_t, kwin, vwin_t, mcs_t)


def _memkv_kernel(mem_ref, gain_ref, w_ref, o_ref):
    mem_n = _rmsnorm(mem_ref[0], gain_ref[...]).astype(BF16)
    o_ref[0, 0] = _dot(mem_n, w_ref[0]).astype(BF16)


def _memkv_call(mem, gain, w_mem):
    depth = w_mem.shape[0]
    b, m, d = mem.shape
    return pl.pallas_call(
        _memkv_kernel,
        grid=(depth, b),
        in_specs=[
            pl.BlockSpec((1, m, d), lambda l, bi: (bi, 0, 0)),
            pl.BlockSpec((1, d), lambda l, bi: (0, 0)),
            pl.BlockSpec((1, d, 2 * MEM_WIDTH), lambda l, bi: (l, 0, 0)),
        ],
        out_specs=pl.BlockSpec((1, 1, m, 2 * MEM_WIDTH), lambda l, bi: (l, bi, 0, 0)),
        out_shape=jax.ShapeDtypeStruct((depth, b, m, 2 * MEM_WIDTH), BF16),
        compiler_params=pltpu.CompilerParams(
            dimension_semantics=("arbitrary", "arbitrary"), vmem_limit_bytes=VMEM_LIMIT),
        name="mem_kv",
    )(mem, gain, w_mem)


_F_AUVZ = 0
_F_BZ = 3 * A_WIDTH
_F_CQZ = _F_BZ + NSA_WIDTH
_F_MERGE = _F_CQZ + 2 * MEM_WIDTH


def _fused_kernel(x_ref, ob_ref, gain_ref, w_ref, lng_ref, lnb_ref, wsp_ref, bsp_ref, mkv_ref,
                  wa_ref, wb_ref, wc_ref, wo_ref, *fgain_and_out, final):
    fgain_ref, o_ref = fgain_and_out if final else (None,) + fgain_and_out
    tm, d = x_ref.shape[1], x_ref.shape[2]
    th = tm // FUSE_SPLIT

    def front(x):
        h = _rmsnorm(x, gain_ref[...]).astype(BF16)

        def proj(c0, width):
            return _dot(h, w_ref[:, c0:c0 + width])

        return (proj(_F_AUVZ, 3 * A_WIDTH), proj(_F_CQZ, 2 * MEM_WIDTH), proj(_F_BZ, NSA_WIDTH),
                [proj(_F_MERGE + k * d, d) for k in range(3)])

    def back(x, ob, uvz, cqz, zb, merge):
        u, v, z = uvz[:, 0:A_WIDTH], uvz[:, A_WIDTH:2 * A_WIDTH], uvz[:, 2 * A_WIDTH:]
        vc = v - jnp.mean(v, axis=-1, keepdims=True)
        vn = (vc * lax.rsqrt(jnp.mean(vc * vc, axis=-1, keepdims=True) + EPS) * lng_ref[...]
              + lnb_ref[...]).astype(BF16)
        gdim = A_WIDTH // A_GROUPS
        mixed = jnp.concatenate([
            jnp.concatenate([
                _dot(wsp_ref[gi], vn[c * CHUNK:(c + 1) * CHUNK, gi * gdim:(gi + 1) * gdim]) + bsp_ref[gi]
                for gi in range(A_GROUPS)], axis=1)
            for c in range(th // CHUNK)], axis=0)

        scores = [_nt_dot(cqz[:, hd * HEAD_DIM:(hd + 1) * HEAD_DIM].astype(BF16),
                          mkv_ref[0, 0, :, hd * HEAD_DIM:(hd + 1) * HEAD_DIM]) * ATT_SCALE
                  for hd in range(MEM_HEADS)]

        o_a = (u * mixed * _silu(z)).astype(BF16)
        acc = jax.nn.sigmoid(merge[0]) * _dot(o_a, wa_ref[...])

        o_b = (ob * _silu(zb)).astype(BF16)
        acc = acc + jax.nn.sigmoid(merge[1]) * _dot(o_b, wb_ref[...])

        heads = []
        for hd in range(MEM_HEADS):
            e_m = jnp.exp(scores[hd] - jnp.max(scores[hd], axis=1, keepdims=True))
            p_m = e_m * (1.0 / jnp.sum(e_m, axis=1, keepdims=True))
            heads.append(_dot(p_m.astype(BF16),
                              mkv_ref[0, 0, :, MEM_WIDTH + hd * HEAD_DIM:MEM_WIDTH + (hd + 1) * HEAD_DIM]))
        o_c = (jnp.concatenate(heads, axis=1) * _silu(cqz[:, MEM_WIDTH:])).astype(BF16)
        acc = acc + jax.nn.sigmoid(merge[2]) * _dot(o_c, wc_ref[...])

        x_new = x + _dot(acc.astype(BF16), wo_ref[...])
        if final:
            x_new = _rmsnorm(x_new, fgain_ref[...])
        return x_new

    rows = [slice(k * th, (k + 1) * th) for k in range(FUSE_SPLIT)]
    fronts = [front(x_ref[0, r, :]) for r in rows]
    for r, f in zip(rows, fronts):
        o_ref[0, r, :] = back(x_ref[0, r, :], ob_ref[0, r, :], *f)


def _fused_call(x, ob, gain, w_f, lng, lnb, wsp, bsp, mkv, layer, wa, wb, wc, wo, fgain=None):
    b, s, d = x.shape
    tm = FUSE_TM
    m = mkv.shape[2]
    tok = lambda bi, i: (bi, i, 0)
    c2 = lambda bi, i: (0, 0)
    final = fgain is not None

    def layer_weights(w):
        return _resident((None,) + w.shape[1:], lambda bi, i: (layer,) + (0,) * (w.ndim - 1))

    return pl.pallas_call(
        functools.partial(_fused_kernel, final=final),
        grid=(b, s // tm),
        in_specs=[
            pl.BlockSpec((1, tm, d), tok),
            pl.BlockSpec((1, tm, NSA_WIDTH), tok),
            pl.BlockSpec((1, d), c2),
            layer_weights(w_f),
            pl.BlockSpec((1, A_WIDTH), c2),
            pl.BlockSpec((1, A_WIDTH), c2),
            layer_weights(wsp),
            layer_weights(bsp),
            _resident((1, 1, m, 2 * MEM_WIDTH), lambda bi, i: (layer, bi, 0, 0)),
            layer_weights(wa),
            layer_weights(wb),
            layer_weights(wc),
            layer_weights(wo),
        ] + ([pl.BlockSpec((1, d), c2)] if final else []),
        out_specs=pl.BlockSpec((1, tm, d), tok),
        out_shape=jax.ShapeDtypeStruct((b, s, d), F32),
        compiler_params=pltpu.CompilerParams(
            dimension_semantics=("arbitrary", "arbitrary"), vmem_limit_bytes=VMEM_LIMIT),
        name="fused_mix",
    )(x, ob, gain, w_f, lng, lnb, wsp, bsp, mkv, wa, wb, wc, wo, *([fgain] if final else []))


def _rope_lane_tables(positions):
    inv_freq = ROPE_THETA ** (-jnp.arange(0, ROPE_DIM, 2, dtype=F32) / ROPE_DIM)
    ang = positions.astype(F32)[..., None] * inv_freq
    cos, sin = jnp.cos(ang), jnp.sin(ang)
    zeros = jnp.zeros(cos.shape[:-1] + (HEAD_DIM - ROPE_DIM,), F32)
    cosf = jnp.concatenate([cos, cos, zeros + 1.0], axis=-1)
    sina = jnp.concatenate([-sin, jnp.zeros_like(sin), zeros], axis=-1)
    sinb = jnp.concatenate([jnp.zeros_like(sin), sin, zeros], axis=-1)
    return cosf, sina, sinb


def _cmp_to_slc(n_chunk, n_slc):
    i = np.arange(n_chunk)[:, None] * CMP_STRIDE
    j = np.arange(n_slc)[None, :] * SLC_BLOCK
    ov = np.clip(np.minimum(i + CMP_BLOCK, j + SLC_BLOCK) - np.maximum(i, j), 0, None) / CMP_BLOCK
    ov[n_chunk - 1] = 0.0
    return jnp.asarray(ov.T, dtype=BF16)


def kernel(x, mem, positions, norm_gain, w_in, ln_v_gain, ln_v_bias, w_spatial, b_spatial, cmp_pe_k, cmp_w1_k, cmp_w2_k, cmp_pe_v, cmp_w1_v, cmp_w2_v, mem_norm_gain, w_mem_kv, w_branch_a, w_branch_b, w_branch_c, w_out, final_norm_gain):
    depth = w_in.shape[0]
    b, s, d = x.shape
    assert s % NSA_TQ == 0 and s % PROJ_TM == 0 and s >= WINDOW + NSA_TQ and d == w_in.shape[1]

    cosf, sina, sinb = _rope_lane_tables(positions)
    mcs = _cmp_to_slc(s // CMP_STRIDE, s // SLC_BLOCK)

    gate_w = jnp.pad(w_in[:, :, _B_G:_B_KV], ((0, 0), (0, 0), (0, GATE_PAD - (_B_KV - _B_G))))
    w_p = jnp.concatenate([w_in[:, :, _B_Q:_B_Z], w_in[:, :, _B_KV:_C_Q], gate_w], axis=-1).astype(BF16)
    w_f = jnp.concatenate([w_in[:, :, _A_U:_B_Q], w_in[:, :, _B_Z:_B_G], w_in[:, :, _C_Q:_MERGE],
                           w_in[:, :, _MERGE:]], axis=-1).astype(BF16)
    wsp = (w_spatial * jnp.tril(jnp.ones((CHUNK, CHUNK), w_spatial.dtype))).astype(BF16)
    bsp = jnp.broadcast_to(b_spatial[..., None], b_spatial.shape + (A_WIDTH // A_GROUPS,))
    pe = jnp.stack([cmp_pe_k, cmp_pe_v], axis=1)
    w1 = jnp.stack([cmp_w1_k, cmp_w1_v], axis=1).astype(BF16).reshape(
        depth, 2, CMP_BLOCK // CMP_STRIDE, CMP_STRIDE * HEAD_DIM, HEAD_DIM)
    w2 = jnp.stack([cmp_w2_k, cmp_w2_v], axis=1).astype(BF16)
    wa, wb, wc, wo = (w.astype(BF16) for w in (w_branch_a, w_branch_b, w_branch_c, w_out))

    mkv = _memkv_call(mem, mem_norm_gain[None, :], w_mem_kv.astype(BF16))
    fgain = final_norm_gain[None, :]
    for l in range(depth):
        gain = norm_gain[l][None, :]
        q, gates_t, cmp_in, ksel, vsel_t, kwin, vwin_t = _proj_call(x, gain, w_p, l, cosf, sina, sinb)
        cmp, cmp_t = _compress_call(cmp_in, pe[l], w1[l], w2[l])
        ob = _nsa_call(q, gates_t, cmp, cmp_t, ksel, vsel_t, kwin, vwin_t, mcs)
        x = _fused_call(x, ob, gain, w_f, ln_v_gain[l][None, :], ln_v_bias[l][None, :], wsp, bsp,
                        mkv, l, wa, wb, wc, wo, fgain if l == depth - 1 else None)
    return x
```

```python
import functools

import numpy as np
import jax
import jax.numpy as jnp
from jax import lax
from jax.experimental import pallas as pl
from jax.experimental.pallas import tpu as pltpu

F32 = jnp.float32
BF16 = jnp.bfloat16

HEAD_DIM = 128
A_GROUPS = 4
A_WIDTH = 512
CHUNK = 128
NSA_HEADS = 8
NSA_KV_GROUPS = 2
HEADS_PER_GROUP = NSA_HEADS // NSA_KV_GROUPS
NSA_WIDTH = NSA_HEADS * HEAD_DIM
CMP_BLOCK = 32
CMP_STRIDE = 16
SLC_BLOCK = 64
N_SELECT = 16
WINDOW = 512
MEM_HEADS = 4
MEM_WIDTH = MEM_HEADS * HEAD_DIM
ROPE_DIM = HEAD_DIM // 4
ROPE_THETA = 500000.0
EPS = 1e-6
NEG_INF = -1e30
FORCED_SCORE = 1e4
ATT_SCALE = HEAD_DIM ** -0.5
LOG2_E = 1.4426950408889634

_OFF = np.cumsum((0, A_WIDTH, A_WIDTH, A_WIDTH, NSA_WIDTH, NSA_WIDTH, NSA_HEADS * 3, 3 * 2 * NSA_KV_GROUPS * HEAD_DIM,
                  MEM_WIDTH, MEM_WIDTH))
(_A_U, _A_V, _A_Z, _B_Q, _B_Z, _B_G, _B_KV, _C_Q, _C_Z, _MERGE) = (int(v) for v in _OFF)
GATE_PAD = 128
GATE_ROWS = 32
V_ROWS = HEAD_DIM + 16

PROJ_TM = 512
PROJ_SPLIT = 2
NSA_TQ = 256
NSA_KC = 256
CMP_VARIANTS = 4
FUSE_TM = 512
FUSE_SPLIT = 2
VMEM_LIMIT = 56 * 1024 * 1024


def _nt_dot(a, b):
    return lax.dot_general(a, b, (((1,), (1,)), ((), ())), preferred_element_type=F32)


def _dot(a, b):
    return jnp.dot(a, b, preferred_element_type=F32)


def _rmsnorm(x, gain):
    return x * lax.rsqrt(jnp.mean(x * x, axis=-1, keepdims=True) + EPS) * gain


def _silu(x):
    return x * jax.nn.sigmoid(x)


def _resident(block_shape, index_map):
    return pl.BlockSpec(block_shape, index_map, pipeline_mode=pl.Buffered(1))


def _proj_kernel(x_ref, gain_ref, w_ref, rope_ref,
                 q_ref, gate_ref, cmp_ref, ksel_ref, vsel_ref, kwin_ref, vwin_ref, *, n_slc):
    tm = x_ref.shape[1]
    th = tm // PROJ_SPLIT
    g2 = NSA_KV_GROUPS
    kv0 = NSA_HEADS
    rows = [slice(k * th, (k + 1) * th) for k in range(PROJ_SPLIT)]
    projs = [_dot(_rmsnorm(x_ref[0, r, :], gain_ref[...]).astype(BF16), w_ref[...]) for r in rows]
    ones_row = (lax.broadcasted_iota(jnp.int32, (V_ROWS - HEAD_DIM, th), 0) == 0).astype(BF16)
    for k, (r, proj) in enumerate(zip(rows, projs)):
        tab = rope_ref[0, r, :]
        lane = lax.broadcasted_iota(jnp.int32, (1, HEAD_DIM), 1)
        lo, hi = lane < ROPE_DIM // 2, (lane >= ROPE_DIM // 2) & (lane < ROPE_DIM)
        cosf = jnp.where(lo, tab, jnp.where(hi, pltpu.roll(tab, ROPE_DIM // 2, 1), 1.0))
        sina = jnp.where(lo, -pltpu.roll(tab, HEAD_DIM - ROPE_DIM // 2, 1), 0.0)
        sinb = jnp.where(hi, tab, 0.0)

        def rope(t):
            return (t * cosf + pltpu.roll(t, HEAD_DIM - ROPE_DIM // 2, 1) * sina
                    + pltpu.roll(t, ROPE_DIM // 2, 1) * sinb)

        def col(j):
            return proj[:, j * HEAD_DIM:(j + 1) * HEAD_DIM]

        for hd in range(NSA_HEADS):
            q_ref[0, hd, :, r] = (rope(col(hd)) * (ATT_SCALE * LOG2_E)).T.astype(BF16)
        row = pl.program_id(1) * tm + k * th + lax.broadcasted_iota(jnp.int32, (th, n_slc), 0)
        blk = lax.broadcasted_iota(jnp.int32, (th, n_slc), 1)
        sel_onehot = (row // SLC_BLOCK == blk).astype(BF16)
        for g in range(g2):
            cmp_ref[0, g, r, :] = rope(col(kv0 + g))
            cmp_ref[0, g2 + g, r, :] = col(kv0 + g2 + g)
            ksel_ref[0, g, r, 0:HEAD_DIM] = rope(col(kv0 + 2 * g2 + g)).astype(BF16)
            ksel_ref[0, g, r, HEAD_DIM:] = sel_onehot
            kwin_ref[0, g, r, :] = rope(col(kv0 + 4 * g2 + g)).astype(BF16)
            for v_ref, j in ((vsel_ref, kv0 + 3 * g2 + g), (vwin_ref, kv0 + 5 * g2 + g)):
                v_ref[0, g, 0:HEAD_DIM, r] = col(j).T.astype(BF16)
                v_ref[0, g, HEAD_DIM:, r] = ones_row
        gate_ref[0, :, r] = jax.nn.sigmoid(proj[:, (kv0 + 6 * g2) * HEAD_DIM:]).T[0:GATE_ROWS]


def _proj_call(x, gain, w_p, layer, rope_tab):
    b, s, d = x.shape
    tm = PROJ_TM
    n_slc = s // SLC_BLOCK
    g2 = NSA_KV_GROUPS
    wp_cols = w_p.shape[2]
    tok = lambda bi, i: (bi, i, 0)
    grp = lambda bi, i: (bi, 0, i, 0)
    grp_t = lambda bi, i: (bi, 0, 0, i)
    return pl.pallas_call(
        functools.partial(_proj_kernel, n_slc=n_slc),
        grid=(b, s // tm),
        in_specs=[
            pl.BlockSpec((1, tm, d), tok),
            pl.BlockSpec((1, d), lambda bi, i: (0, 0)),
            _resident((None, d, wp_cols), lambda bi, i: (layer, 0, 0)),
            pl.BlockSpec((1, tm, HEAD_DIM), tok),
        ],
        out_specs=[
            pl.BlockSpec((1, NSA_HEADS, HEAD_DIM, tm), grp_t),
            pl.BlockSpec((1, GATE_ROWS, tm), lambda bi, i: (bi, 0, i)),
            pl.BlockSpec((1, 2 * g2, tm, HEAD_DIM), grp),
            pl.BlockSpec((1, g2, tm, HEAD_DIM + n_slc), grp),
            pl.BlockSpec((1, g2, V_ROWS, tm), grp_t),
            pl.BlockSpec((1, g2, tm, HEAD_DIM), grp),
            pl.BlockSpec((1, g2, V_ROWS, tm), grp_t),
        ],
        out_shape=[
            jax.ShapeDtypeStruct((b, NSA_HEADS, HEAD_DIM, s), BF16),
            jax.ShapeDtypeStruct((b, GATE_ROWS, s), F32),
            jax.ShapeDtypeStruct((b, 2 * g2, s, HEAD_DIM), F32),
            jax.ShapeDtypeStruct((b, g2, s, HEAD_DIM + n_slc), BF16),
            jax.ShapeDtypeStruct((b, g2, V_ROWS, s), BF16),
            jax.ShapeDtypeStruct((b, g2, s, HEAD_DIM), BF16),
            jax.ShapeDtypeStruct((b, g2, V_ROWS, s), BF16),
        ],
        compiler_params=pltpu.CompilerParams(
            dimension_semantics=("arbitrary", "arbitrary"), vmem_limit_bytes=VMEM_LIMIT),
        name="nsa_proj",
    )(x, gain, w_p, rope_tab)


def _compress_kernel(x_ref, pe_ref, w1_ref, w2_ref, o_ref, ot_ref):
    n_chunk = o_ref.shape[2]
    tokens = [x_ref[0, 0, pl.ds(c, n_chunk, stride=CMP_STRIDE), :] for c in range(CMP_STRIDE)]

    def half_block(first_pos):
        lhs = jnp.concatenate([(tokens[c] + pe_ref[0, first_pos + c:first_pos + c + 1, :]).astype(BF16)
                               for c in range(CMP_STRIDE)], axis=1)
        return _dot(lhs, w1_ref[0, first_pos // CMP_STRIDE])

    first, second = half_block(0), half_block(CMP_STRIDE)
    hidden = first + pltpu.roll(second, n_chunk - 1, 0)
    out = _dot(_silu(hidden).astype(BF16), w2_ref[0])
    o_ref[0, 0] = out.astype(BF16)
    ot_ref[0, 0] = out.T.astype(BF16)


def _compress_call(cmp_in, pe, w1, w2):
    b, n4, s, dh = cmp_in.shape
    g2 = NSA_KV_GROUPS
    n_chunk = s // CMP_STRIDE
    return pl.pallas_call(
        _compress_kernel,
        grid=(b, n4),
        in_specs=[
            pl.BlockSpec((1, 1, s, dh), lambda bi, j: (bi, j, 0, 0)),
            pl.BlockSpec((1, CMP_BLOCK, dh), lambda bi, j: (j // g2, 0, 0)),
            pl.BlockSpec((1, 2, CMP_STRIDE * dh, dh), lambda bi, j: (j // g2, 0, 0, 0)),
            pl.BlockSpec((1, dh, dh), lambda bi, j: (j // g2, 0, 0)),
        ],
        out_specs=[pl.BlockSpec((1, 1, n_chunk, dh), lambda bi, j: (bi, j, 0, 0)),
                   pl.BlockSpec((1, 1, dh, n_chunk), lambda bi, j: (bi, j, 0, 0))],
        out_shape=[jax.ShapeDtypeStruct((b, n4, n_chunk, dh), BF16),
                   jax.ShapeDtypeStruct((b, n4, dh, n_chunk), BF16)],
        compiler_params=pltpu.CompilerParams(
            dimension_semantics=("arbitrary", "arbitrary"), vmem_limit_bytes=VMEM_LIMIT),
        name="nsa_compress",
    )(cmp_in, pe, w1, w2)


def _nsa_kernel(q_ref, gate_ref, cmp_ref, cmpt_ref, ksel_ref, vsel_ref, kwin_ref, vwin_ref, mcs_ref, o_ref,
                lhs_ref, m_ref, acc_ref, part_ref, *, seq):
    tq = q_ref.shape[3]
    assert tq == NSA_KC
    rows = HEADS_PER_GROUP * tq
    half = rows // 2
    n_chunk = seq // CMP_STRIDE
    n_cmp = n_chunk - 1
    n_slc = seq // SLC_BLOCK
    groups = range(NSA_KV_GROUPS)
    i = pl.program_id(1)
    t0 = i * tq
    tok = t0 + lax.broadcasted_iota(jnp.int32, (1, tq), 1)
    tpos = jnp.concatenate([tok] * HEADS_PER_GROUP, axis=1)

    def mask_bias(mask):
        return jnp.where(mask, 0.0, NEG_INF)

    def mask_heads(s, bias):
        return jnp.concatenate([s[:, r * tq:(r + 1) * tq] + bias for r in range(s.shape[1] // tq)], axis=1)

    def gate_row(g, branch):
        return jnp.concatenate(
            [gate_ref[0, 3 * (HEADS_PER_GROUP * g + r) + branch:3 * (HEADS_PER_GROUP * g + r) + branch + 1, :]
             for r in range(HEADS_PER_GROUP)], axis=1)

    def normalise(acc_t, gate):
        return acc_t[0:HEAD_DIM] * (gate / acc_t[HEAD_DIM:HEAD_DIM + 1])

    def run_pipelined(items):
        scores = [score_fn() for score_fn, _ in items]
        for k, (_, consume_fn) in enumerate(items):
            consume_fn(scores[k])
            scores[k] = None

    def q_group(g):
        return jnp.concatenate([q_ref[0, HEADS_PER_GROUP * g + r] for r in range(HEADS_PER_GROUP)], axis=1)

    def window_items():
        wlen = WINDOW + tq
        w0 = pl.multiple_of(jnp.maximum(t0 - WINDOW, 0), tq)
        rel = tok - (w0 + lax.broadcasted_iota(jnp.int32, (wlen, 1), 0))
        win_bias = mask_bias(lax.bitcast_convert_type(rel, jnp.uint32) < WINDOW)

        def win_score(g):
            return _dot(kwin_ref[0, g, pl.ds(w0, wlen), :], q_group(g))

        def win_consume(g, s_w):
            s_w = mask_heads(s_w, win_bias)
            p_w = jnp.exp2(s_w - jnp.max(s_w, axis=0, keepdims=True)).astype(BF16)
            part_ref[1, g] = normalise(_dot(vwin_ref[0, g, :, pl.ds(w0, wlen)], p_w), gate_row(g, 2))

        return [(functools.partial(win_score, g), functools.partial(win_consume, g)) for g in groups]

    for g in groups:
        lhs_ref[g, 0:HEAD_DIM, :] = q_group(g)
        m_ref[g] = jnp.full((1, rows), NEG_INF, F32)
        acc_ref[g] = jnp.zeros((V_ROWS, rows), F32)

    def select_blocks(vis_chunk):
        vis_slc = vis_chunk * CMP_STRIDE // SLC_BLOCK
        nrow = lax.broadcasted_iota(jnp.int32, (vis_chunk, 1), 0)
        cmp_bias = mask_bias(nrow <= jnp.minimum((tok - (CMP_BLOCK - 1)) // CMP_STRIDE, n_cmp - 1))
        sees_block = tpos >= CMP_BLOCK - 1
        imps = [None] * NSA_KV_GROUPS

        def cmp_score(g):
            return _dot(cmp_ref[0, g, 0:vis_chunk, :], q_group(g))

        def cmp_consume(g, s_c):
            s_c = mask_heads(s_c, cmp_bias)
            e_c = jnp.exp2(s_c - jnp.max(s_c, axis=0, keepdims=True))
            p_c = e_c * jnp.where(sees_block, 1.0 / jnp.sum(e_c, axis=0, keepdims=True), 0.0)
            part_ref[0, g] = (_dot(cmpt_ref[0, NSA_KV_GROUPS + g, :, 0:vis_chunk], p_c.astype(BF16))
                              * gate_row(g, 0))
            p_sum = p_c[:, 0:tq]
            for r in range(1, HEADS_PER_GROUP):
                p_sum = p_sum + p_c[:, r * tq:(r + 1) * tq]
            p_hi = p_sum.astype(BF16)
            p_lo = (p_sum - p_hi.astype(F32)).astype(BF16)
            mcs_t = mcs_ref[0:vis_slc, 0:vis_chunk]
            imps[g] = _dot(mcs_t, p_hi) + _dot(mcs_t, p_lo)

        run_pipelined([(functools.partial(cmp_score, g), functools.partial(cmp_consume, g)) for g in groups]
                      + window_items())

        jrow = lax.broadcasted_iota(jnp.int32, (vis_slc, tq), 0)
        cur = (t0 + lax.broadcasted_iota(jnp.int32, (vis_slc, tq), 1)) // SLC_BLOCK
        forced = (jrow == 0) | (jrow == cur) | (jrow == cur - 1)
        jrow_f = jrow.astype(F32)
        for g in groups:
            cand = jnp.where(forced, -jnp.inf, jnp.where(jrow <= cur, imps[g], -1.0))
            for _ in range(min(N_SELECT, n_slc) - 3):
                best = jnp.max(cand, axis=0, keepdims=True)
                first = jnp.min(jnp.where(cand == best, jrow_f, float(n_slc)), axis=0, keepdims=True)
                cand = jnp.where(jrow_f == first, -jnp.inf, cand)
            sel_bias = jnp.where(cand == -jnp.inf, 0.0, NEG_INF).astype(BF16)
            lhs_ref[g, HEAD_DIM:HEAD_DIM + vis_slc, :] = jnp.concatenate([sel_bias] * HEADS_PER_GROUP, axis=1)
            if vis_slc < n_slc:
                lhs_ref[g, HEAD_DIM + vis_slc:, :] = jnp.full((n_slc - vis_slc, rows), NEG_INF, BF16)

    chunk_step = n_chunk // CMP_VARIANTS
    variant = ((t0 + tq) // CMP_STRIDE - 1) // chunk_step
    for v in range(CMP_VARIANTS):
        pl.when(variant == v)(functools.partial(select_blocks, chunk_step * (v + 1)))

    def sel_score(g, hf, k0, width):
        return _dot(ksel_ref[0, g, pl.ds(k0, width), :], lhs_ref[g, :, hf * half:(hf + 1) * half])

    def sel_consume(g, hf, k0, width, causal_bias, s_s):
        lanes = slice(hf * half, (hf + 1) * half)
        if causal_bias is not None:
            s_s = mask_heads(s_s, causal_bias)
        m_old = m_ref[g, :, lanes]
        m_new = jnp.maximum(m_old, jnp.max(s_s, axis=0, keepdims=True))
        p_s = jnp.exp2(s_s - m_new).astype(BF16)
        acc_ref[g, :, lanes] = (jnp.exp2(m_old - m_new) * acc_ref[g, :, lanes]
                                + _dot(vsel_ref[0, g, :, pl.ds(k0, width)], p_s))
        m_ref[g, :, lanes] = m_new

    def sweep(first_chunk, n_chunks, last_masked=False):
        k0 = pl.multiple_of(first_chunk * NSA_KC, NSA_KC)
        width = n_chunks * NSA_KC
        causal_bias = None
        if last_masked:
            causal_bias = mask_bias(k0 + lax.broadcasted_iota(jnp.int32, (width, 1), 0) <= tok)
        run_pipelined([(functools.partial(sel_score, g, hf, k0, width),
                        functools.partial(sel_consume, g, hf, k0, width, causal_bias))
                       for g in groups for hf in range(2)])

    def octet_body(ko, carry):
        sweep(8 * ko, 8)
        return carry

    lax.fori_loop(0, i // 8, octet_body, 0)
    done = (i // 8) * 8

    @pl.when(i % 8 >= 4)
    def _():
        sweep(done, 4)

    done = (i // 4) * 4

    @pl.when(i % 4 >= 2)
    def _():
        sweep(done, 2)

    done = (i // 2) * 2

    @pl.when(i % 2 == 1)
    def _():
        sweep(done, 2, last_masked=True)

    @pl.when(i % 2 == 0)
    def _():
        sweep(done, 1, last_masked=True)

    for g in groups:
        mixed_t = (part_ref[0, g] + normalise(acc_ref[g], gate_row(g, 1))) + part_ref[1, g]
        for r in range(HEADS_PER_GROUP):
            hd = HEADS_PER_GROUP * g + r
            o_ref[0, :, hd * HEAD_DIM:(hd + 1) * HEAD_DIM] = mixed_t[:, r * tq:(r + 1) * tq].T


def _nsa_call(q, gates_t, cmp, cmp_t, ksel, vsel_t, kwin, vwin_t, mcs_t):
    b, _, _, s = q.shape
    tq = NSA_TQ
    g2 = NSA_KV_GROUPS
    n_chunk = s // CMP_STRIDE
    n_slc = s // SLC_BLOCK
    rows = HEADS_PER_GROUP * tq
    tok = lambda bi, i: (bi, i, 0)
    per_batch = lambda bi, i: (bi, 0, 0, 0)
    return pl.pallas_call(
        functools.partial(_nsa_kernel, seq=s),
        grid=(b, s // tq),
        in_specs=[
            pl.BlockSpec((1, NSA_HEADS, HEAD_DIM, tq), lambda bi, i: (bi, 0, 0, i)),
            pl.BlockSpec((1, GATE_ROWS, tq), lambda bi, i: (bi, 0, i)),
            _resident((1, 2 * g2, n_chunk, HEAD_DIM), per_batch),
            _resident((1, 2 * g2, HEAD_DIM, n_chunk), per_batch),
            _resident((1, g2, s, HEAD_DIM + n_slc), per_batch),
            _resident((1, g2, V_ROWS, s), per_batch),
            _resident((1, g2, s, HEAD_DIM), per_batch),
            _resident((1, g2, V_ROWS, s), per_batch),
            _resident((n_slc, n_chunk), lambda bi, i: (0, 0)),
        ],
        out_specs=pl.BlockSpec((1, tq, NSA_WIDTH), tok),
        out_shape=jax.ShapeDtypeStruct((b, s, NSA_WIDTH), F32),
        scratch_shapes=[
            pltpu.VMEM((g2, HEAD_DIM + n_slc, rows), BF16),
            pltpu.VMEM((g2, 1, rows), F32),
            pltpu.VMEM((g2, V_ROWS, rows), F32),
            pltpu.VMEM((2, g2, HEAD_DIM, rows), F32),
        ],
        compiler_params=pltpu.CompilerParams(
            dimension_semantics=("arbitrary", "arbitrary"), vmem_limit_bytes=VMEM_LIMIT),
        name="nsa_attention",
    )(q, gates_t, cmp, cmp_t, ksel, vsel_t, kwin, vwin_t, mcs_t)


def _memkv_kernel(mem_ref, gain_ref, w_ref, o_ref):
    mem_n = _rmsnorm(mem_ref[0], gain_ref[...]).astype(BF16)
    o_ref[0, 0] = _dot(mem_n, w_ref[0]).astype(BF16)


def _memkv_call(mem, gain, w_mem):
    depth = w_mem.shape[0]
    b, m, d = mem.shape
    return pl.pallas_call(
        _memkv_kernel,
        grid=(depth, b),
        in_specs=[
            pl.BlockSpec((1, m, d), lambda l, bi: (bi, 0, 0)),
            pl.BlockSpec((1, d), lambda l, bi: (0, 0)),
            pl.BlockSpec((1, d, 2 * MEM_WIDTH), lambda l, bi: (l, 0, 0)),
        ],
        out_specs=pl.BlockSpec((1, 1, m, 2 * MEM_WIDTH), lambda l, bi: (l, bi, 0, 0)),
        out_shape=jax.ShapeDtypeStruct((depth, b, m, 2 * MEM_WIDTH), BF16),
        compiler_params=pltpu.CompilerParams(
            dimension_semantics=("arbitrary", "arbitrary"), vmem_limit_bytes=VMEM_LIMIT),
        name="mem_kv",
    )(mem, gain, w_mem)


_F_AUVZ = 0
_F_BZ = 3 * A_WIDTH
_F_CQZ = _F_BZ + NSA_WIDTH
_F_MERGE = _F_CQZ + 2 * MEM_WIDTH


def _fused_kernel(x_ref, ob_ref, gain_ref, w_ref, lng_ref, lnb_ref, wsp_ref, bsp_ref, mkv_ref,
                  wa_ref, wb_ref, wc_ref, wo_ref, *fgain_and_out, final):
    fgain_ref, o_ref = fgain_and_out if final else (None,) + fgain_and_out
    tm, d = x_ref.shape[1], x_ref.shape[2]
    th = tm // FUSE_SPLIT

    def front(x):
        h = _rmsnorm(x, gain_ref[...]).astype(BF16)

        def proj(c0, width):
            return _dot(h, w_ref[:, c0:c0 + width])

        return (proj(_F_AUVZ, 3 * A_WIDTH), proj(_F_CQZ, 2 * MEM_WIDTH), proj(_F_BZ, NSA_WIDTH),
                [proj(_F_MERGE + k * d, d) for k in range(3)])

    def back(x, ob, uvz, cqz, zb, merge):
        u, v, z = uvz[:, 0:A_WIDTH], uvz[:, A_WIDTH:2 * A_WIDTH], uvz[:, 2 * A_WIDTH:]
        vc = v - jnp.mean(v, axis=-1, keepdims=True)
        vn = (vc * lax.rsqrt(jnp.mean(vc * vc, axis=-1, keepdims=True) + EPS) * lng_ref[...]
              + lnb_ref[...]).astype(BF16)
        gdim = A_WIDTH // A_GROUPS
        mixed = jnp.concatenate([
            jnp.concatenate([
                _dot(wsp_ref[gi], vn[c * CHUNK:(c + 1) * CHUNK, gi * gdim:(gi + 1) * gdim]) + bsp_ref[gi]
                for gi in range(A_GROUPS)], axis=1)
            for c in range(th // CHUNK)], axis=0)

        scores = [_nt_dot(cqz[:, hd * HEAD_DIM:(hd + 1) * HEAD_DIM].astype(BF16),
                          mkv_ref[0, 0, :, hd * HEAD_DIM:(hd + 1) * HEAD_DIM]) * ATT_SCALE
                  for hd in range(MEM_HEADS)]

        o_a = (u * mixed * _silu(z)).astype(BF16)
        acc = jax.nn.sigmoid(merge[0]) * _dot(o_a, wa_ref[...])

        o_b = (ob * _silu(zb)).astype(BF16)
        acc = acc + jax.nn.sigmoid(merge[1]) * _dot(o_b, wb_ref[...])

        heads = []
        for hd in range(MEM_HEADS):
            e_m = jnp.exp(scores[hd] - jnp.max(scores[hd], axis=1, keepdims=True))
            p_m = e_m * (1.0 / jnp.sum(e_m, axis=1, keepdims=True))
            heads.append(_dot(p_m.astype(BF16),
                              mkv_ref[0, 0, :, MEM_WIDTH + hd * HEAD_DIM:MEM_WIDTH + (hd + 1) * HEAD_DIM]))
        o_c = (jnp.concatenate(heads, axis=1) * _silu(cqz[:, MEM_WIDTH:])).astype(BF16)
        acc = acc + jax.nn.sigmoid(merge[2]) * _dot(o_c, wc_ref[...])

        x_new = x + _dot(acc.astype(BF16), wo_ref[...])
        if final:
            x_new = _rmsnorm(x_new, fgain_ref[...])
        return x_new

    rows = [slice(k * th, (k + 1) * th) for k in range(FUSE_SPLIT)]
    fronts = [front(x_ref[0, r, :]) for r in rows]
    for r, f in zip(rows, fronts):
        o_ref[0, r, :] = back(x_ref[0, r, :], ob_ref[0, r, :], *f)


def _fused_call(x, ob, gain, w_f, lng, lnb, wsp, bsp, mkv, layer, wa, wb, wc, wo, fgain=None):
    b, s, d = x.shape
    tm = FUSE_TM
    m = mkv.shape[2]
    tok = lambda bi, i: (bi, i, 0)
    c2 = lambda bi, i: (0, 0)
    final = fgain is not None

    def layer_weights(w):
        return _resident((None,) + w.shape[1:], lambda bi, i: (layer,) + (0,) * (w.ndim - 1))

    return pl.pallas_call(
        functools.partial(_fused_kernel, final=final),
        grid=(b, s // tm),
        in_specs=[
            pl.BlockSpec((1, tm, d), tok),
            pl.BlockSpec((1, tm, NSA_WIDTH), tok),
            pl.BlockSpec((1, d), c2),
            layer_weights(w_f),
            pl.BlockSpec((1, A_WIDTH), c2),
            pl.BlockSpec((1, A_WIDTH), c2),
            layer_weights(wsp),
            layer_weights(bsp),
            _resident((1, 1, m, 2 * MEM_WIDTH), lambda bi, i: (layer, bi, 0, 0)),
            layer_weights(wa),
            layer_weights(wb),
            layer_weights(wc),
            layer_weights(wo),
        ] + ([pl.BlockSpec((1, d), c2)] if final else []),
        out_specs=pl.BlockSpec((1, tm, d), tok),
        out_shape=jax.ShapeDtypeStruct((b, s, d), F32),
        compiler_params=pltpu.CompilerParams(
            dimension_semantics=("arbitrary", "arbitrary"), vmem_limit_bytes=VMEM_LIMIT),
        name="fused_mix",
    )(x, ob, gain, w_f, lng, lnb, wsp, bsp, mkv, wa, wb, wc, wo, *([fgain] if final else []))


def _rope_lane_table(positions):
    inv_freq = ROPE_THETA ** (-jnp.arange(0, ROPE_DIM, 2, dtype=F32) / ROPE_DIM)
    ang = positions.astype(F32)[..., None] * inv_freq
    zeros = jnp.zeros(ang.shape[:-1] + (HEAD_DIM - ROPE_DIM,), F32)
    return jnp.concatenate([jnp.cos(ang), jnp.sin(ang), zeros], axis=-1)


def _cmp_to_slc(n_chunk, n_slc):
    i = np.arange(n_chunk)[:, None] * CMP_STRIDE
    j = np.arange(n_slc)[None, :] * SLC_BLOCK
    ov = np.clip(np.minimum(i + CMP_BLOCK, j + SLC_BLOCK) - np.maximum(i, j), 0, None) / CMP_BLOCK
    ov[n_chunk - 1] = 0.0
    return jnp.asarray(ov.T, dtype=BF16)


def kernel(x, mem, positions, norm_gain, w_in, ln_v_gain, ln_v_bias, w_spatial, b_spatial, cmp_pe_k, cmp_w1_k, cmp_w2_k, cmp_pe_v, cmp_w1_v, cmp_w2_v, mem_norm_gain, w_mem_kv, w_branch_a, w_branch_b, w_branch_c, w_out, final_norm_gain):
    depth = w_in.shape[0]
    b, s, d = x.shape
    assert s % NSA_TQ == 0 and s % PROJ_TM == 0 and s >= WINDOW + NSA_TQ and d == w_in.shape[1]

    rope_tab = _rope_lane_table(positions)
    mcs = _cmp_to_slc(s // CMP_STRIDE, s // SLC_BLOCK)

    gate_w = jnp.pad(w_in[:, :, _B_G:_B_KV], ((0, 0), (0, 0), (0, GATE_PAD - (_B_KV - _B_G))))
    w_p = jnp.concatenate([w_in[:, :, _B_Q:_B_Z], w_in[:, :, _B_KV:_C_Q], gate_w], axis=-1).astype(BF16)
    w_f = jnp.concatenate([w_in[:, :, _A_U:_B_Q], w_in[:, :, _B_Z:_B_G], w_in[:, :, _C_Q:_MERGE],
                           w_in[:, :, _MERGE:]], axis=-1).astype(BF16)
    wsp = (w_spatial * jnp.tril(jnp.ones((CHUNK, CHUNK), w_spatial.dtype))).astype(BF16)
    bsp = jnp.broadcast_to(b_spatial[..., None], b_spatial.shape + (A_WIDTH // A_GROUPS,))
    pe = jnp.stack([cmp_pe_k, cmp_pe_v], axis=1)
    w1 = jnp.stack([cmp_w1_k, cmp_w1_v], axis=1).astype(BF16).reshape(
        depth, 2, CMP_BLOCK // CMP_STRIDE, CMP_STRIDE * HEAD_DIM, HEAD_DIM)
    w2 = jnp.stack([cmp_w2_k, cmp_w2_v], axis=1).astype(BF16)
    wa, wb, wc, wo = (w.astype(BF16) for w in (w_branch_a, w_branch_b, w_branch_c, w_out))

    mkv = _memkv_call(mem, mem_norm_gain[None, :], w_mem_kv.astype(BF16))
    fgain = final_norm_gain[None, :]
    for l in range(depth):
        gain = norm_gain[l][None, :]
        q, gates_t, cmp_in, ksel, vsel_t, kwin, vwin_t = _proj_call(x, gain, w_p, l, rope_tab)
        cmp, cmp_t = _compress_call(cmp_in, pe[l], w1[l], w2[l])
        ob = _nsa_call(q, gates_t, cmp, cmp_t, ksel, vsel_t, kwin, vwin_t, mcs)
        x = _fused_call(x, ob, gain, w_f, ln_v_gain[l][None, :], ln_v_bias[l][None, :], wsp, bsp,
                        mkv, l, wa, wb, wc, wo, fgain if l == depth - 1 else None)
    return x
```

```python
import functools

import numpy as np
import jax
import jax.numpy as jnp
from jax import lax
from jax.experimental import pallas as pl
from jax.experimental.pallas import tpu as pltpu

F32 = jnp.float32
BF16 = jnp.bfloat16

HEAD_DIM = 128
A_GROUPS = 4
A_WIDTH = 512
CHUNK = 128
NSA_HEADS = 8
NSA_KV_GROUPS = 2
HEADS_PER_GROUP = NSA_HEADS // NSA_KV_GROUPS
NSA_WIDTH = NSA_HEADS * HEAD_DIM
CMP_BLOCK = 32
CMP_STRIDE = 16
SLC_BLOCK = 64
N_SELECT = 16
WINDOW = 512
MEM_HEADS = 4
MEM_WIDTH = MEM_HEADS * HEAD_DIM
ROPE_DIM = HEAD_DIM // 4
ROPE_THETA = 500000.0
EPS = 1e-6
NEG_INF = -1e30
FORCED_SCORE = 1e4
ATT_SCALE = HEAD_DIM ** -0.5
LOG2_E = 1.4426950408889634

_OFF = np.cumsum((0, A_WIDTH, A_WIDTH, A_WIDTH, NSA_WIDTH, NSA_WIDTH, NSA_HEADS * 3, 3 * 2 * NSA_KV_GROUPS * HEAD_DIM,
                  MEM_WIDTH, MEM_WIDTH))
(_A_U, _A_V, _A_Z, _B_Q, _B_Z, _B_G, _B_KV, _C_Q, _C_Z, _MERGE) = (int(v) for v in _OFF)
GATE_PAD = 128
GATE_ROWS = 32
V_ROWS = HEAD_DIM + 16

PROJ_TM = 512
PROJ_SPLIT = 2
NSA_TQ = 256
NSA_KC = 256
CMP_VARIANTS = 4
FUSE_TM = 512
FUSE_SPLIT = 2
VMEM_LIMIT = 56 * 1024 * 1024


def _nt_dot(a, b):
    return lax.dot_general(a, b, (((1,), (1,)), ((), ())), preferred_element_type=F32)


def _dot(a, b):
    return jnp.dot(a, b, preferred_element_type=F32)


def _rmsnorm(x, gain):
    return x * lax.rsqrt(jnp.mean(x * x, axis=-1, keepdims=True) + EPS) * gain


def _silu(x):
    return x * jax.nn.sigmoid(x)


def _resident(block_shape, index_map):
    return pl.BlockSpec(block_shape, index_map, pipeline_mode=pl.Buffered(1))


def _proj_kernel(x_ref, gain_ref, w_ref, rope_ref,
                 q_ref, gate_ref, cmp_ref, ksel_ref, vsel_ref, kwin_ref, vwin_ref, *, n_slc):
    tm = x_ref.shape[1]
    th = tm // PROJ_SPLIT
    g2 = NSA_KV_GROUPS
    kv0 = NSA_HEADS
    rows = [slice(k * th, (k + 1) * th) for k in range(PROJ_SPLIT)]
    projs = [_dot(_rmsnorm(x_ref[0, r, :], gain_ref[...]).astype(BF16), w_ref[...]) for r in rows]
    ones_row = (lax.broadcasted_iota(jnp.int32, (V_ROWS - HEAD_DIM, th), 0) == 0).astype(BF16)
    for k, (r, proj) in enumerate(zip(rows, projs)):
        tab = rope_ref[0, r, :]
        lane = lax.broadcasted_iota(jnp.int32, (1, HEAD_DIM), 1)
        lo, hi = lane < ROPE_DIM // 2, (lane >= ROPE_DIM // 2) & (lane < ROPE_DIM)
        cosf = jnp.where(lo, tab, jnp.where(hi, pltpu.roll(tab, ROPE_DIM // 2, 1), 1.0))
        sina = jnp.where(lo, -pltpu.roll(tab, HEAD_DIM - ROPE_DIM // 2, 1), 0.0)
        sinb = jnp.where(hi, tab, 0.0)

        def rope(t):
            return (t * cosf + pltpu.roll(t, HEAD_DIM - ROPE_DIM // 2, 1) * sina
                    + pltpu.roll(t, ROPE_DIM // 2, 1) * sinb)

        def col(j):
            return proj[:, j * HEAD_DIM:(j + 1) * HEAD_DIM]

        for hd in range(NSA_HEADS):
            q_ref[0, hd, :, r] = (rope(col(hd)) * (ATT_SCALE * LOG2_E)).T.astype(BF16)
        row = pl.program_id(1) * tm + k * th + lax.broadcasted_iota(jnp.int32, (th, n_slc), 0)
        blk = lax.broadcasted_iota(jnp.int32, (th, n_slc), 1)
        sel_onehot = (row // SLC_BLOCK == blk).astype(BF16)
        for g in range(g2):
            cmp_ref[0, g, r, :] = rope(col(kv0 + g))
            cmp_ref[0, g2 + g, r, :] = col(kv0 + g2 + g)
            ksel_ref[0, g, r, 0:HEAD_DIM] = rope(col(kv0 + 2 * g2 + g)).astype(BF16)
            ksel_ref[0, g, r, HEAD_DIM:] = sel_onehot
            kwin_ref[0, g, r, :] = rope(col(kv0 + 4 * g2 + g)).astype(BF16)
            for v_ref, j in ((vsel_ref, kv0 + 3 * g2 + g), (vwin_ref, kv0 + 5 * g2 + g)):
                v_ref[0, g, 0:HEAD_DIM, r] = col(j).T.astype(BF16)
                v_ref[0, g, HEAD_DIM:, r] = ones_row
        gate_ref[0, :, r] = jax.nn.sigmoid(proj[:, (kv0 + 6 * g2) * HEAD_DIM:]).T[0:GATE_ROWS]


def _proj_call(x, gain, w_p, layer, rope_tab):
    b, s, d = x.shape
    tm = PROJ_TM
    n_slc = s // SLC_BLOCK
    g2 = NSA_KV_GROUPS
    wp_cols = w_p.shape[2]
    tok = lambda bi, i: (bi, i, 0)
    grp = lambda bi, i: (bi, 0, i, 0)
    grp_t = lambda bi, i: (bi, 0, 0, i)
    return pl.pallas_call(
        functools.partial(_proj_kernel, n_slc=n_slc),
        grid=(b, s // tm),
        in_specs=[
            pl.BlockSpec((1, tm, d), tok),
            pl.BlockSpec((1, d), lambda bi, i: (0, 0)),
            _resident((None, d, wp_cols), lambda bi, i: (layer, 0, 0)),
            pl.BlockSpec((1, tm, HEAD_DIM), tok),
        ],
        out_specs=[
            pl.BlockSpec((1, NSA_HEADS, HEAD_DIM, tm), grp_t),
            pl.BlockSpec((1, GATE_ROWS, tm), lambda bi, i: (bi, 0, i)),
            pl.BlockSpec((1, 2 * g2, tm, HEAD_DIM), grp),
            pl.BlockSpec((1, g2, tm, HEAD_DIM + n_slc), grp),
            pl.BlockSpec((1, g2, V_ROWS, tm), grp_t),
            pl.BlockSpec((1, g2, tm, HEAD_DIM), grp),
            pl.BlockSpec((1, g2, V_ROWS, tm), grp_t),
        ],
        out_shape=[
            jax.ShapeDtypeStruct((b, NSA_HEADS, HEAD_DIM, s), BF16),
            jax.ShapeDtypeStruct((b, GATE_ROWS, s), F32),
            jax.ShapeDtypeStruct((b, 2 * g2, s, HEAD_DIM), F32),
            jax.ShapeDtypeStruct((b, g2, s, HEAD_DIM + n_slc), BF16),
            jax.ShapeDtypeStruct((b, g2, V_ROWS, s), BF16),
            jax.ShapeDtypeStruct((b, g2, s, HEAD_DIM), BF16),
            jax.ShapeDtypeStruct((b, g2, V_ROWS, s), BF16),
        ],
        compiler_params=pltpu.CompilerParams(
            dimension_semantics=("arbitrary", "arbitrary"), vmem_limit_bytes=VMEM_LIMIT),
        name="nsa_proj",
    )(x, gain, w_p, rope_tab)


def _compress_kernel(x_ref, pe_ref, w1_ref, w2_ref, o_ref, ot_ref):
    n_chunk = o_ref.shape[2]
    tokens = [x_ref[0, 0, pl.ds(c, n_chunk, stride=CMP_STRIDE), :] for c in range(CMP_STRIDE)]

    def half_block(first_pos):
        lhs = jnp.concatenate([(tokens[c] + pe_ref[0, first_pos + c:first_pos + c + 1, :]).astype(BF16)
                               for c in range(CMP_STRIDE)], axis=1)
        return _dot(lhs, w1_ref[0, first_pos // CMP_STRIDE])

    first, second = half_block(0), half_block(CMP_STRIDE)
    hidden = first + pltpu.roll(second, n_chunk - 1, 0)
    out = _dot(_silu(hidden).astype(BF16), w2_ref[0])
    o_ref[0, 0] = out.astype(BF16)
    ot_ref[0, 0] = out.T.astype(BF16)


def _compress_call(cmp_in, pe, w1, w2):
    b, n4, s, dh = cmp_in.shape
    g2 = NSA_KV_GROUPS
    n_chunk = s // CMP_STRIDE
    return pl.pallas_call(
        _compress_kernel,
        grid=(b, n4),
        in_specs=[
            pl.BlockSpec((1, 1, s, dh), lambda bi, j: (bi, j, 0, 0)),
            pl.BlockSpec((1, CMP_BLOCK, dh), lambda bi, j: (j // g2, 0, 0)),
            pl.BlockSpec((1, 2, CMP_STRIDE * dh, dh), lambda bi, j: (j // g2, 0, 0, 0)),
            pl.BlockSpec((1, dh, dh), lambda bi, j: (j // g2, 0, 0)),
        ],
        out_specs=[pl.BlockSpec((1, 1, n_chunk, dh), lambda bi, j: (bi, j, 0, 0)),
                   pl.BlockSpec((1, 1, dh, n_chunk), lambda bi, j: (bi, j, 0, 0))],
        out_shape=[jax.ShapeDtypeStruct((b, n4, n_chunk, dh), BF16),
                   jax.ShapeDtypeStruct((b, n4, dh, n_chunk), BF16)],
        compiler_params=pltpu.CompilerParams(
            dimension_semantics=("arbitrary", "arbitrary"), vmem_limit_bytes=VMEM_LIMIT),
        name="nsa_compress",
    )(cmp_in, pe, w1, w2)


def _nsa_kernel(q_ref, gate_ref, cmp_ref, cmpt_ref, ksel_ref, vsel_ref, kwin_ref, vwin_ref, mcs_ref, o_ref,
                lhs_ref, m_ref, acc_ref, part_ref, *, seq):
    tq = q_ref.shape[3]
    assert tq == NSA_KC
    rows = HEADS_PER_GROUP * tq
    half = rows // 2
    n_chunk = seq // CMP_STRIDE
    n_cmp = n_chunk - 1
    n_slc = seq // SLC_BLOCK
    groups = range(NSA_KV_GROUPS)
    i = pl.program_id(1)
    t0 = i * tq
    tok = t0 + lax.broadcasted_iota(jnp.int32, (1, tq), 1)
    tpos = jnp.concatenate([tok] * HEADS_PER_GROUP, axis=1)

    def mask_bias(mask):
        return jnp.where(mask, 0.0, NEG_INF)

    def mask_heads(s, bias):
        return jnp.concatenate([s[:, r * tq:(r + 1) * tq] + bias for r in range(s.shape[1] // tq)], axis=1)

    def gate_row(g, branch):
        return jnp.concatenate(
            [gate_ref[0, 3 * (HEADS_PER_GROUP * g + r) + branch:3 * (HEADS_PER_GROUP * g + r) + branch + 1, :]
             for r in range(HEADS_PER_GROUP)], axis=1)

    def normalise(acc_t, gate):
        return acc_t[0:HEAD_DIM] * (gate / acc_t[HEAD_DIM:HEAD_DIM + 1])

    def run_pipelined(items):
        scores = [score_fn() for score_fn, _ in items]
        for k, (_, consume_fn) in enumerate(items):
            consume_fn(scores[k])
            scores[k] = None

    def q_group(g):
        return jnp.concatenate([q_ref[0, HEADS_PER_GROUP * g + r] for r in range(HEADS_PER_GROUP)], axis=1)

    def window_items():
        wlen = WINDOW + tq
        w0 = pl.multiple_of(jnp.maximum(t0 - WINDOW, 0), tq)
        rel = tok - (w0 + lax.broadcasted_iota(jnp.int32, (wlen, 1), 0))
        win_bias = mask_bias(lax.bitcast_convert_type(rel, jnp.uint32) < WINDOW)

        def win_score(g, hf):
            return _dot(kwin_ref[0, g, pl.ds(w0, wlen), :], q_group(g)[:, hf * half:(hf + 1) * half])

        def win_consume(g, hf, s_w):
            lanes = slice(hf * half, (hf + 1) * half)
            s_w = mask_heads(s_w, win_bias)
            p_w = jnp.exp2(s_w - jnp.max(s_w, axis=0, keepdims=True)).astype(BF16)
            part_ref[1, g, :, lanes] = normalise(_dot(vwin_ref[0, g, :, pl.ds(w0, wlen)], p_w),
                                                 gate_row(g, 2)[:, lanes])

        return [(functools.partial(win_score, g, hf), functools.partial(win_consume, g, hf))
                for g in groups for hf in range(2)]

    for g in groups:
        lhs_ref[g, 0:HEAD_DIM, :] = q_group(g)
        m_ref[g] = jnp.full((1, rows), NEG_INF, F32)
        acc_ref[g] = jnp.zeros((V_ROWS, rows), F32)

    def select_blocks(vis_chunk):
        vis_slc = vis_chunk * CMP_STRIDE // SLC_BLOCK
        nrow = lax.broadcasted_iota(jnp.int32, (vis_chunk, 1), 0)
        cmp_bias = mask_bias(nrow <= jnp.minimum((tok - (CMP_BLOCK - 1)) // CMP_STRIDE, n_cmp - 1))
        sees_block = tpos >= CMP_BLOCK - 1
        imps = [None] * NSA_KV_GROUPS

        head_sums = [None] * NSA_KV_GROUPS

        def cmp_score(g, hf):
            return _dot(cmp_ref[0, g, 0:vis_chunk, :], q_group(g)[:, hf * half:(hf + 1) * half])

        def cmp_consume(g, hf, s_c):
            lanes = slice(hf * half, (hf + 1) * half)
            s_c = mask_heads(s_c, cmp_bias)
            e_c = jnp.exp2(s_c - jnp.max(s_c, axis=0, keepdims=True))
            p_c = e_c * jnp.where(sees_block[:, lanes], 1.0 / jnp.sum(e_c, axis=0, keepdims=True), 0.0)
            part_ref[0, g, :, lanes] = (_dot(cmpt_ref[0, NSA_KV_GROUPS + g, :, 0:vis_chunk], p_c.astype(BF16))
                                        * gate_row(g, 0)[:, lanes])
            p_sum = p_c[:, 0:tq] if head_sums[g] is None else head_sums[g] + p_c[:, 0:tq]
            for r in range(1, half // tq):
                p_sum = p_sum + p_c[:, r * tq:(r + 1) * tq]
            head_sums[g] = p_sum
            if hf == 0:
                return
            p_hi = p_sum.astype(BF16)
            p_lo = (p_sum - p_hi.astype(F32)).astype(BF16)
            mcs_t = mcs_ref[0:vis_slc, 0:vis_chunk]
            imps[g] = _dot(mcs_t, p_hi) + _dot(mcs_t, p_lo)

        run_pipelined([(functools.partial(cmp_score, g, hf), functools.partial(cmp_consume, g, hf))
                       for g in groups for hf in range(2)] + window_items())

        jrow = lax.broadcasted_iota(jnp.int32, (vis_slc, tq), 0)
        cur = (t0 + lax.broadcasted_iota(jnp.int32, (vis_slc, tq), 1)) // SLC_BLOCK
        forced = (jrow == 0) | (jrow == cur) | (jrow == cur - 1)
        jrow_f = jrow.astype(F32)
        for g in groups:
            cand = jnp.where(forced, -jnp.inf, jnp.where(jrow <= cur, imps[g], -1.0))
            for _ in range(min(N_SELECT, n_slc) - 3):
                best = jnp.max(cand, axis=0, keepdims=True)
                first = jnp.min(jnp.where(cand == best, jrow_f, float(n_slc)), axis=0, keepdims=True)
                cand = jnp.where(jrow_f == first, -jnp.inf, cand)
            sel_bias = jnp.where(cand == -jnp.inf, 0.0, NEG_INF).astype(BF16)
            lhs_ref[g, HEAD_DIM:HEAD_DIM + vis_slc, :] = jnp.concatenate([sel_bias] * HEADS_PER_GROUP, axis=1)
            if vis_slc < n_slc:
                lhs_ref[g, HEAD_DIM + vis_slc:, :] = jnp.full((n_slc - vis_slc, rows), NEG_INF, BF16)

    chunk_step = n_chunk // CMP_VARIANTS
    variant = ((t0 + tq) // CMP_STRIDE - 1) // chunk_step
    for v in range(CMP_VARIANTS):
        pl.when(variant == v)(functools.partial(select_blocks, chunk_step * (v + 1)))

    def sel_score(g, hf, k0, width):
        return _dot(ksel_ref[0, g, pl.ds(k0, width), :], lhs_ref[g, :, hf * half:(hf + 1) * half])

    def sel_consume(g, hf, k0, width, causal_bias, s_s):
        lanes = slice(hf * half, (hf + 1) * half)
        if causal_bias is not None:
            s_s = mask_heads(s_s, causal_bias)
        m_old = m_ref[g, :, lanes]
        m_new = jnp.maximum(m_old, jnp.max(s_s, axis=0, keepdims=True))
        p_s = jnp.exp2(s_s - m_new).astype(BF16)
        acc_ref[g, :, lanes] = (jnp.exp2(m_old - m_new) * acc_ref[g, :, lanes]
                                + _dot(vsel_ref[0, g, :, pl.ds(k0, width)], p_s))
        m_ref[g, :, lanes] = m_new

    def sweep(first_chunk, n_chunks, last_masked=False):
        k0 = pl.multiple_of(first_chunk * NSA_KC, NSA_KC)
        width = n_chunks * NSA_KC
        causal_bias = None
        if last_masked:
            causal_bias = mask_bias(k0 + lax.broadcasted_iota(jnp.int32, (width, 1), 0) <= tok)
        run_pipelined([(functools.partial(sel_score, g, hf, k0, width),
                        functools.partial(sel_consume, g, hf, k0, width, causal_bias))
                       for g in groups for hf in range(2)])

    def octet_body(ko, carry):
        sweep(8 * ko, 8)
        return carry

    lax.fori_loop(0, i // 8, octet_body, 0)
    done = (i // 8) * 8

    @pl.when(i % 8 >= 4)
    def _():
        sweep(done, 4)

    done = (i // 4) * 4

    @pl.when(i % 4 >= 2)
    def _():
        sweep(done, 2)

    done = (i // 2) * 2

    @pl.when(i % 2 == 1)
    def _():
        sweep(done, 2, last_masked=True)

    @pl.when(i % 2 == 0)
    def _():
        sweep(done, 1, last_masked=True)

    for g in groups:
        mixed_t = (part_ref[0, g] + normalise(acc_ref[g], gate_row(g, 1))) + part_ref[1, g]
        for r in range(HEADS_PER_GROUP):
            hd = HEADS_PER_GROUP * g + r
            o_ref[0, :, hd * HEAD_DIM:(hd + 1) * HEAD_DIM] = mixed_t[:, r * tq:(r + 1) * tq].T


def _nsa_call(q, gates_t, cmp, cmp_t, ksel, vsel_t, kwin, vwin_t, mcs_t):
    b, _, _, s = q.shape
    tq = NSA_TQ
    g2 = NSA_KV_GROUPS
    n_chunk = s // CMP_STRIDE
    n_slc = s // SLC_BLOCK
    rows = HEADS_PER_GROUP * tq
    tok = lambda bi, i: (bi, i, 0)
    per_batch = lambda bi, i: (bi, 0, 0, 0)
    return pl.pallas_call(
        functools.partial(_nsa_kernel, seq=s),
        grid=(b, s // tq),
        in_specs=[
            pl.BlockSpec((1, NSA_HEADS, HEAD_DIM, tq), lambda bi, i: (bi, 0, 0, i)),
            pl.BlockSpec((1, GATE_ROWS, tq), lambda bi, i: (bi, 0, i)),
            _resident((1, 2 * g2, n_chunk, HEAD_DIM), per_batch),
            _resident((1, 2 * g2, HEAD_DIM, n_chunk), per_batch),
            _resident((1, g2, s, HEAD_DIM + n_slc), per_batch),
            _resident((1, g2, V_ROWS, s), per_batch),
            _resident((1, g2, s, HEAD_DIM), per_batch),
            _resident((1, g2, V_ROWS, s), per_batch),
            _resident((n_slc, n_chunk), lambda bi, i: (0, 0)),
        ],
        out_specs=pl.BlockSpec((1, tq, NSA_WIDTH), tok),
        out_shape=jax.ShapeDtypeStruct((b, s, NSA_WIDTH), F32),
        scratch_shapes=[
            pltpu.VMEM((g2, HEAD_DIM + n_slc, rows), BF16),
            pltpu.VMEM((g2, 1, rows), F32),
            pltpu.VMEM((g2, V_ROWS, rows), F32),
            pltpu.VMEM((2, g2, HEAD_DIM, rows), F32),
        ],
        compiler_params=pltpu.CompilerParams(
            dimension_semantics=("arbitrary", "arbitrary"), vmem_limit_bytes=VMEM_LIMIT),
        name="nsa_attention",
    )(q, gates_t, cmp, cmp_t, ksel, vsel_t, kwin, vwin_t, mcs_t)


def _memkv_kernel(mem_ref, gain_ref, w_ref, o_ref):
    mem_n = _rmsnorm(mem_ref[0], gain_ref[...]).astype(BF16)
    o_ref[0, 0] = _dot(mem_n, w_ref[0]).astype(BF16)


def _memkv_call(mem, gain, w_mem):
    depth = w_mem.shape[0]
    b, m, d = mem.shape
    return pl.pallas_call(
        _memkv_kernel,
        grid=(depth, b),
        in_specs=[
            pl.BlockSpec((1, m, d), lambda l, bi: (bi, 0, 0)),
            pl.BlockSpec((1, d), lambda l, bi: (0, 0)),
            pl.BlockSpec((1, d, 2 * MEM_WIDTH), lambda l, bi: (l, 0, 0)),
        ],
        out_specs=pl.BlockSpec((1, 1, m, 2 * MEM_WIDTH), lambda l, bi: (l, bi, 0, 0)),
        out_shape=jax.ShapeDtypeStruct((depth, b, m, 2 * MEM_WIDTH), BF16),
        compiler_params=pltpu.CompilerParams(
            dimension_semantics=("arbitrary", "arbitrary"), vmem_limit_bytes=VMEM_LIMIT),
        name="mem_kv",
    )(mem, gain, w_mem)


_F_AUVZ = 0
_F_BZ = 3 * A_WIDTH
_F_CQZ = _F_BZ + NSA_WIDTH
_F_MERGE = _F_CQZ + 2 * MEM_WIDTH


def _fused_kernel(x_ref, ob_ref, gain_ref, w_ref, lng_ref, lnb_ref, wsp_ref, bsp_ref, mkv_ref,
                  wa_ref, wb_ref, wc_ref, wo_ref, *fgain_and_out, final):
    fgain_ref, o_ref = fgain_and_out if final else (None,) + fgain_and_out
    tm, d = x_ref.shape[1], x_ref.shape[2]
    th = tm // FUSE_SPLIT

    def front(x):
        h = _rmsnorm(x, gain_ref[...]).astype(BF16)

        def proj(c0, width):
            return _dot(h, w_ref[:, c0:c0 + width])

        return (proj(_F_AUVZ, 3 * A_WIDTH), proj(_F_CQZ, 2 * MEM_WIDTH), proj(_F_BZ, NSA_WIDTH),
                [proj(_F_MERGE + k * d, d) for k in range(3)])

    def back(x, ob, uvz, cqz, zb, merge):
        u, v, z = uvz[:, 0:A_WIDTH], uvz[:, A_WIDTH:2 * A_WIDTH], uvz[:, 2 * A_WIDTH:]
        vc = v - jnp.mean(v, axis=-1, keepdims=True)
        vn = (vc * lax.rsqrt(jnp.mean(vc * vc, axis=-1, keepdims=True) + EPS) * lng_ref[...]
              + lnb_ref[...]).astype(BF16)
        gdim = A_WIDTH // A_GROUPS
        mixed = jnp.concatenate([
            jnp.concatenate([
                _dot(wsp_ref[gi], vn[c * CHUNK:(c + 1) * CHUNK, gi * gdim:(gi + 1) * gdim]) + bsp_ref[gi]
                for gi in range(A_GROUPS)], axis=1)
            for c in range(th // CHUNK)], axis=0)

        scores = [_nt_dot(cqz[:, hd * HEAD_DIM:(hd + 1) * HEAD_DIM].astype(BF16),
                          mkv_ref[0, 0, :, hd * HEAD_DIM:(hd + 1) * HEAD_DIM]) * ATT_SCALE
                  for hd in range(MEM_HEADS)]

        o_a = (u * mixed * _silu(z)).astype(BF16)
        acc = jax.nn.sigmoid(merge[0]) * _dot(o_a, wa_ref[...])

        o_b = (ob * _silu(zb)).astype(BF16)
        acc = acc + jax.nn.sigmoid(merge[1]) * _dot(o_b, wb_ref[...])

        heads = []
        for hd in range(MEM_HEADS):
            e_m = jnp.exp(scores[hd] - jnp.max(scores[hd], axis=1, keepdims=True))
            p_m = e_m * (1.0 / jnp.sum(e_m, axis=1, keepdims=True))
            heads.append(_dot(p_m.astype(BF16),
                              mkv_ref[0, 0, :, MEM_WIDTH + hd * HEAD_DIM:MEM_WIDTH + (hd + 1) * HEAD_DIM]))
        o_c = (jnp.concatenate(heads, axis=1) * _silu(cqz[:, MEM_WIDTH:])).astype(BF16)
        acc = acc + jax.nn.sigmoid(merge[2]) * _dot(o_c, wc_ref[...])

        x_new = x + _dot(acc.astype(BF16), wo_ref[...])
        if final:
            x_new = _rmsnorm(x_new, fgain_ref[...])
        return x_new

    rows = [slice(k * th, (k + 1) * th) for k in range(FUSE_SPLIT)]
    fronts = [front(x_ref[0, r, :]) for r in rows]
    for r, f in zip(rows, fronts):
        o_ref[0, r, :] = back(x_ref[0, r, :], ob_ref[0, r, :], *f)


def _fused_call(x, ob, gain, w_f, lng, lnb, wsp, bsp, mkv, layer, wa, wb, wc, wo, fgain=None):
    b, s, d = x.shape
    tm = FUSE_TM
    m = mkv.shape[2]
    tok = lambda bi, i: (bi, i, 0)
    c2 = lambda bi, i: (0, 0)
    final = fgain is not None

    def layer_weights(w):
        return _resident((None,) + w.shape[1:], lambda bi, i: (layer,) + (0,) * (w.ndim - 1))

    return pl.pallas_call(
        functools.partial(_fused_kernel, final=final),
        grid=(b, s // tm),
        in_specs=[
            pl.BlockSpec((1, tm, d), tok),
            pl.BlockSpec((1, tm, NSA_WIDTH), tok),
            pl.BlockSpec((1, d), c2),
            layer_weights(w_f),
            pl.BlockSpec((1, A_WIDTH), c2),
            pl.BlockSpec((1, A_WIDTH), c2),
            layer_weights(wsp),
            layer_weights(bsp),
            _resident((1, 1, m, 2 * MEM_WIDTH), lambda bi, i: (layer, bi, 0, 0)),
            layer_weights(wa),
            layer_weights(wb),
            layer_weights(wc),
            layer_weights(wo),
        ] + ([pl.BlockSpec((1, d), c2)] if final else []),
        out_specs=pl.BlockSpec((1, tm, d), tok),
        out_shape=jax.ShapeDtypeStruct((b, s, d), F32),
        compiler_params=pltpu.CompilerParams(
            dimension_semantics=("arbitrary", "arbitrary"), vmem_limit_bytes=VMEM_LIMIT),
        name="fused_mix",
    )(x, ob, gain, w_f, lng, lnb, wsp, bsp, mkv, wa, wb, wc, wo, *([fgain] if final else []))


def _rope_lane_table(positions):
    inv_freq = ROPE_THETA ** (-jnp.arange(0, ROPE_DIM, 2, dtype=F32) / ROPE_DIM)
    ang = positions.astype(F32)[..., None] * inv_freq
    zeros = jnp.zeros(ang.shape[:-1] + (HEAD_DIM - ROPE_DIM,), F32)
    return jnp.concatenate([jnp.cos(ang), jnp.sin(ang), zeros], axis=-1)


def _cmp_to_slc(n_chunk, n_slc):
    i = np.arange(n_chunk)[:, None] * CMP_STRIDE
    j = np.arange(n_slc)[None, :] * SLC_BLOCK
    ov = np.clip(np.minimum(i + CMP_BLOCK, j + SLC_BLOCK) - np.maximum(i, j), 0, None) / CMP_BLOCK
    ov[n_chunk - 1] = 0.0
    return jnp.asarray(ov.T, dtype=BF16)


def kernel(x, mem, positions, norm_gain, w_in, ln_v_gain, ln_v_bias, w_spatial, b_spatial, cmp_pe_k, cmp_w1_k, cmp_w2_k, cmp_pe_v, cmp_w1_v, cmp_w2_v, mem_norm_gain, w_mem_kv, w_branch_a, w_branch_b, w_branch_c, w_out, final_norm_gain):
    depth = w_in.shape[0]
    b, s, d = x.shape
    assert s % NSA_TQ == 0 and s % PROJ_TM == 0 and s >= WINDOW + NSA_TQ and d == w_in.shape[1]

    rope_tab = _rope_lane_table(positions)
    mcs = _cmp_to_slc(s // CMP_STRIDE, s // SLC_BLOCK)

    gate_w = jnp.pad(w_in[:, :, _B_G:_B_KV], ((0, 0), (0, 0), (0, GATE_PAD - (_B_KV - _B_G))))
    w_p = jnp.concatenate([w_in[:, :, _B_Q:_B_Z], w_in[:, :, _B_KV:_C_Q], gate_w], axis=-1).astype(BF16)
    w_f = jnp.concatenate([w_in[:, :, _A_U:_B_Q], w_in[:, :, _B_Z:_B_G], w_in[:, :, _C_Q:_MERGE],
                           w_in[:, :, _MERGE:]], axis=-1).astype(BF16)
    wsp = (w_spatial * jnp.tril(jnp.ones((CHUNK, CHUNK), w_spatial.dtype))).astype(BF16)
    bsp = jnp.broadcast_to(b_spatial[..., None], b_spatial.shape + (A_WIDTH // A_GROUPS,))
    pe = jnp.stack([cmp_pe_k, cmp_pe_v], axis=1)
    w1 = jnp.stack([cmp_w1_k, cmp_w1_v], axis=1).astype(BF16).reshape(
        depth, 2, CMP_BLOCK // CMP_STRIDE, CMP_STRIDE * HEAD_DIM, HEAD_DIM)
    w2 = jnp.stack([cmp_w2_k, cmp_w2_v], axis=1).astype(BF16)
    wa, wb, wc, wo = (w.astype(BF16) for w in (w_branch_a, w_branch_b, w_branch_c, w_out))

    mkv = _memkv_call(mem, mem_norm_gain[None, :], w_mem_kv.astype(BF16))
    fgain = final_norm_gain[None, :]
    for l in range(depth):
        gain = norm_gain[l][None, :]
        q, gates_t, cmp_in, ksel, vsel_t, kwin, vwin_t = _proj_call(x, gain, w_p, l, rope_tab)
        cmp, cmp_t = _compress_call(cmp_in, pe[l], w1[l], w2[l])
        ob = _nsa_call(q, gates_t, cmp, cmp_t, ksel, vsel_t, kwin, vwin_t, mcs)
        x = _fused_call(x, ob, gain, w_f, ln_v_gain[l][None, :], ln_v_bias[l][None, :], wsp, bsp,
                        mkv, l, wa, wb, wc, wo, fgain if l == depth - 1 else None)
    return x
```
